```python
import math
import jax, jax.numpy as jnp
from jax import lax
import numpy as np

D_MODEL = 2048
BATCH = 4
SEQ = 4096
DEPTH = 2
DEC_BATCH = 8
DEC_SEQ = 32
PAST_LEN = 2048

CHUNK = 64
MIX_WIDTH = D_MODEL
EPS = 1e-6
F32 = jnp.float32
A_GROUPS = 4
A_HEAD = 128
A_WIDTH = A_GROUPS * A_HEAD
A_CHUNK = 128
B_HEADS = 4
B_DK = 128
B_DV = 128
B_WIDTH = B_HEADS * B_DV
B_QKV = 2 * B_HEADS * B_DK + B_WIDTH
B_CONV = 4
C_HEADS = 8
C_NOPE = 128
C_ROPE = 64
C_VDIM = 128
C_WIDTH = C_HEADS * C_VDIM
Q_LORA = 512
KV_LORA = 256
ROPE_THETA = 10000.0
C_SCALE = (C_NOPE + C_ROPE) ** -0.5
Q_BLOCK = 128
D_FF = 4 * D_MODEL
OFF_AU = 0
OFF_AV = OFF_AU + A_WIDTH
OFF_BQKV = OFF_AV + A_WIDTH
OFF_BZ = OFF_BQKV + B_QKV
OFF_BBETA = OFF_BZ + B_WIDTH
OFF_BALPHA = OFF_BBETA + B_HEADS
OFF_CQ = OFF_BALPHA + B_HEADS
OFF_CKV = OFF_CQ + Q_LORA
OFF_CKR = OFF_CKV + KV_LORA
IN_COLS = OFF_CKR + C_ROPE

kernel_name = "hymba_gmlp_gdn_mla_stream_step"


def rms_norm(x, w):
    xf = x.astype(F32)
    y = xf * lax.rsqrt(jnp.mean(xf * xf, axis=-1, keepdims=True) + EPS)
    return (y * w.astype(F32)).astype(x.dtype)


def layer_norm(x, w, b):
    xf = x.astype(F32)
    mu = jnp.mean(xf, axis=-1, keepdims=True)
    var = jnp.mean(jnp.square(xf - mu), axis=-1, keepdims=True)
    return ((xf - mu) * lax.rsqrt(var + 1e-5) * w.astype(F32) + b.astype(F32)).astype(x.dtype)


def l2norm(x):
    xf = x.astype(F32)
    return xf * lax.rsqrt(jnp.sum(xf * xf, axis=-1, keepdims=True) + EPS)


def rope(x, pos):
    half = C_ROPE // 2
    inv = ROPE_THETA ** (-jnp.arange(half, dtype=F32) / half)
    ang = pos.astype(F32)[:, None] * inv[None, :]
    shape = (1, pos.shape[0]) + (1,) * (x.ndim - 3) + (half,)
    cos = jnp.cos(ang).reshape(shape).astype(x.dtype)
    sin = jnp.sin(ang).reshape(shape).astype(x.dtype)
    x1, x2 = x[..., :half], x[..., half:]
    return jnp.concatenate([x1 * cos - x2 * sin, x1 * sin + x2 * cos], axis=-1)


def chunk_mlp_mix(u, v, ln_w, ln_b, w_s, b_s):
    bsz, L, _ = u.shape
    lc = min(L, A_CHUNK)
    n = L // lc
    vn = layer_norm(v, ln_w, ln_b)
    idx = jnp.arange(lc)
    allowed = (idx[:, None] // CHUNK) >= (idx[None, :] // CHUNK)
    w = jnp.where(allowed[None], w_s[:, :lc, :lc], 0.0)
    f = jnp.einsum('gij,bnjgc->bnigc', w, vn.reshape(bsz, n, lc, A_GROUPS, A_HEAD))
    f = f + b_s[:, :lc].T[None, None, :, :, None]
    return u * f.reshape(bsz, L, A_WIDTH), vn


def causal_conv(x, buf, w):
    L = x.shape[1]
    xp = jnp.concatenate([buf, x], axis=1)
    y = xp[:, 0:L] * w[0]
    for i in range(1, B_CONV):
        y = y + xp[:, i:i + L] * w[i]
    return jax.nn.silu(y), xp[:, -(B_CONV - 1):]


def gated_delta_rule(q, k, v, g, beta, s0, chunk):
    bsz, L, H, dk = q.shape
    dv = v.shape[-1]
    n = L // chunk

    def to_chunks(t):
        return t.astype(F32).reshape(bsz, n, chunk, H, -1).transpose(0, 3, 1, 2, 4)

    qc = to_chunks(q) * (dk ** -0.5)
    kc = to_chunks(k)
    vc = to_chunks(v)
    gc = jnp.cumsum(g.astype(F32).reshape(bsz, n, chunk, H).transpose(0, 3, 1, 2), axis=-1)
    bc = beta.astype(F32).reshape(bsz, n, chunk, H).transpose(0, 3, 1, 2)[..., None]
    idx = jnp.arange(chunk)
    causal = idx[:, None] >= idx[None, :]
    strict = idx[:, None] > idx[None, :]
    decay = jnp.exp(jnp.where(causal, gc[..., :, None] - gc[..., None, :], -jnp.inf))
    kb = kc * bc
    lower = jnp.where(strict, jnp.einsum('bhnid,bhnjd->bhnij', kb, kc) * decay, 0.0)
    a = lower + jnp.eye(chunk, dtype=F32)
    rhs = jnp.concatenate([vc * bc, kb * jnp.exp(gc)[..., None]], axis=-1)
    sol = lax.linalg.triangular_solve(a, rhs, left_side=True, lower=True, unit_diagonal=True)
    u, w = sol[..., :dv], sol[..., dv:]
    qk = jnp.where(causal, jnp.einsum('bhnid,bhnjd->bhnij', qc, kc) * decay, 0.0)

    def step(S, xs):
        q_i, k_i, u_i, w_i, qk_i, g_i = xs
        v_new = u_i - jnp.einsum('bhck,bhkv->bhcv', w_i, S)
        o_i = (jnp.einsum('bhck,bhkv->bhcv', q_i * jnp.exp(g_i)[..., None], S)
               + jnp.einsum('bhij,bhjv->bhiv', qk_i, v_new))
        g_last = g_i[..., -1:]
        S = S * jnp.exp(g_last)[..., None] + jnp.einsum(
            'bhck,bhcv->bhkv', k_i * jnp.exp(g_last - g_i)[..., None], v_new)
        return S, o_i

    xs = tuple(jnp.moveaxis(t, 2, 0) for t in (qc, kc, u, w, qk, gc))
    s_fin, o = lax.scan(step, s0.astype(F32), xs)
    o = o.transpose(1, 0, 3, 2, 4).reshape(bsz, L, H, dv)
    return o, s_fin


def gated_deltanet(proj, conv_buf, s0, conv_w, a_log, dt_bias, norm_w, chunk):
    bsz, L, _ = proj.shape
    qkv, new_buf = causal_conv(proj[..., OFF_BQKV:OFF_BZ], conv_buf, conv_w)
    hk = B_HEADS * B_DK
    q = l2norm(qkv[..., :hk].reshape(bsz, L, B_HEADS, B_DK))
    k = l2norm(qkv[..., hk:2 * hk].reshape(bsz, L, B_HEADS, B_DK))
    v = qkv[..., 2 * hk:].reshape(bsz, L, B_HEADS, B_DV)
    z = proj[..., OFF_BZ:OFF_BBETA].reshape(bsz, L, B_HEADS, B_DV)
    beta = jax.nn.sigmoid(proj[..., OFF_BBETA:OFF_BALPHA].astype(F32))
    g = -jnp.exp(a_log.astype(F32)) * jax.nn.softplus(proj[..., OFF_BALPHA:OFF_CQ].astype(F32) + dt_bias.astype(F32))
    o, s_new = gated_delta_rule(q, k, v, g, beta, s0, chunk)
    o = o * lax.rsqrt(jnp.mean(o * o, axis=-1, keepdims=True) + EPS) * norm_w.astype(F32) * jax.nn.silu(z.astype(F32))
    return o.astype(proj.dtype).reshape(bsz, L, B_WIDTH), s_new.astype(s0.dtype), new_buf


def mla_project(proj, pos, q_norm_w, w_uq, kv_norm_w, w_uk):
    cq = rms_norm(proj[..., OFF_CQ:OFF_CKV], q_norm_w)
    q = jnp.einsum('bsc,chd->bshd', cq, w_uq)
    q_lat = jnp.einsum('bshd,chd->bshc', q[..., :C_NOPE], w_uk)
    q_rope = rope(q[..., C_NOPE:], pos)
    ckv = rms_norm(proj[..., OFF_CKV:OFF_CKR], kv_norm_w)
    kr = rope(proj[..., OFF_CKR:IN_COLS], pos)
    return q_lat, q_rope, ckv, kr


def mla_attend(q_lat, q_rope, ckv, kr, w_uv, mask):
    s = (jnp.einsum('bqhc,bkc->bhqk', q_lat, ckv).astype(F32)
         + jnp.einsum('bqhr,bkr->bhqk', q_rope, kr).astype(F32)) * C_SCALE
    if mask is not None:
        s = jnp.where(mask, s, -jnp.inf)
    p = jax.nn.softmax(s, axis=-1).astype(ckv.dtype)
    o = jnp.einsum('bhqk,bkc->bqhc', p, ckv)
    return jnp.einsum('bqhc,chv->bqhv', o, w_uv)


def mla_prompt(q_lat, q_rope, ckv, kr, w_uv, pos):
    bsz, L = q_lat.shape[0], q_lat.shape[1]
    nb = L // Q_BLOCK

    def block(args):
        ql, qr, qp = args
        mask = (qp[:, None] // CHUNK) >= (pos[None, :] // CHUNK)
        return mla_attend(ql, qr, ckv, kr, w_uv, mask)

    ql = q_lat.reshape(bsz, nb, Q_BLOCK, C_HEADS, KV_LORA).swapaxes(0, 1)
    qr = q_rope.reshape(bsz, nb, Q_BLOCK, C_HEADS, C_ROPE).swapaxes(0, 1)
    out = lax.map(block, (ql, qr, pos.reshape(nb, Q_BLOCK)))
    return out.swapaxes(0, 1).reshape(bsz, L, C_WIDTH)


def layer(x, pos, lw, past):
    bsz, L, _ = x.shape
    h = rms_norm(x, lw['attn_norm_w'])
    proj = jnp.einsum('bld,de->ble', h, lw['w_in'])
    u = jax.nn.gelu(proj[..., OFF_AU:OFF_AV], approximate=False)
    v = jax.nn.gelu(proj[..., OFF_AV:OFF_BQKV], approximate=False)
    ya, va = chunk_mlp_mix(u, v, lw['a_ln_w'], lw['a_ln_b'], lw['a_w_s'], lw['a_b_s'])
    if past is None:
        conv_buf = jnp.zeros((bsz, B_CONV - 1, B_QKV), x.dtype)
        s0 = jnp.zeros((bsz, B_HEADS, B_DK, B_DV), x.dtype)
        chunk = CHUNK
    else:
        ckv_past, kr_past, s0, conv_buf = past
        chunk = L
    yb, s_new, buf_new = gated_deltanet(proj, conv_buf, s0, lw['b_conv_w'], lw['b_a_log'],
                                        lw['b_dt_bias'], lw['b_norm_w'], chunk)
    q_lat, q_rope, ckv, kr = mla_project(proj, pos, lw['c_q_norm_w'], lw['c_w_uq'], lw['c_kv_norm_w'], lw['c_w_uk'])
    if past is None:
        yc = mla_prompt(q_lat, q_rope, ckv, kr, lw['c_w_uv'], pos)
    else:
        keys_c = jnp.concatenate([ckv_past, ckv], axis=1)
        keys_r = jnp.concatenate([kr_past, kr], axis=1)
        yc = mla_attend(q_lat, q_rope, keys_c, keys_r, lw['c_w_uv'], None).reshape(bsz, L, C_WIDTH)
    mix = jnp.concatenate([ya, yb, yc], axis=-1)
    x = x + jnp.einsum('ble,ed->bld', mix, lw['w_out'])
    h2 = rms_norm(x, lw['mlp_norm_w'])
    hid = jnp.square(jax.nn.relu(jnp.einsum('bld,df->blf', h2, lw['w_up'])))
    x = x + jnp.einsum('blf,fd->bld', hid, lw['w_down'])
    return x, (ckv, kr, s_new, buf_new, va)


def setup_inputs(seed: int = 0) -> dict:
    key = jax.random.key(seed)
    ks = jax.random.split(key, 32)

    def nrm(i, shape, scale):
        return scale * jax.random.normal(ks[i], shape, F32)

    dt = jnp.exp(jax.random.uniform(ks[14], (DEPTH, B_HEADS), F32, math.log(1e-3), math.log(1e-1)))
    return {
        'x_prompt': nrm(0, (BATCH, SEQ, D_MODEL), 1.0),
        'x_sample': nrm(1, (DEC_BATCH, DEC_SEQ, D_MODEL), 1.0),
        'cache_mla_ckv': nrm(2, (DEPTH, DEC_BATCH, PAST_LEN, KV_LORA), 1.0),
        'cache_mla_krope': nrm(3, (DEPTH, DEC_BATCH, PAST_LEN, C_ROPE), 1.0),
        'state_gdn': nrm(4, (DEPTH, DEC_BATCH, B_HEADS, B_DK, B_DV), 0.1),
        'state_gdn_conv': nrm(5, (DEPTH, DEC_BATCH, B_CONV - 1, B_QKV), 1.0),
        'attn_norm_w': 1.0 + nrm(6, (DEPTH, D_MODEL), 0.1),
        'w_in': nrm(7, (DEPTH, D_MODEL, IN_COLS), D_MODEL ** -0.5),
        'a_ln_w': 1.0 + nrm(8, (DEPTH, A_WIDTH), 0.1),
        'a_ln_b': nrm(9, (DEPTH, A_WIDTH), 0.02),
        'a_w_s': nrm(10, (DEPTH, A_GROUPS, A_CHUNK, A_CHUNK), A_CHUNK ** -0.5),
        'a_b_s': 1.0 + nrm(11, (DEPTH, A_GROUPS, A_CHUNK), 0.1),
        'b_conv_w': nrm(12, (DEPTH, B_CONV, B_QKV), B_CONV ** -0.5),
        'b_a_log': jnp.log(jax.random.uniform(ks[13], (DEPTH, B_HEADS), F32, 1.0, 16.0)),
        'b_dt_bias': dt + jnp.log(-jnp.expm1(-dt)),
        'b_norm_w': 1.0 + nrm(15, (DEPTH, B_DV), 0.1),
        'c_q_norm_w': 1.0 + nrm(16, (DEPTH, Q_LORA), 0.1),
        'c_w_uq': nrm(17, (DEPTH, Q_LORA, C_HEADS, C_NOPE + C_ROPE), Q_LORA ** -0.5),
        'c_kv_norm_w': 1.0 + nrm(18, (DEPTH, KV_LORA), 0.1),
        'c_w_uk': nrm(19, (DEPTH, KV_LORA, C_HEADS, C_NOPE), KV_LORA ** -0.5),
        'c_w_uv': nrm(20, (DEPTH, KV_LORA, C_HEADS, C_VDIM), KV_LORA ** -0.5),
        'w_out': nrm(21, (DEPTH, MIX_WIDTH, D_MODEL), MIX_WIDTH ** -0.5),
        'mlp_norm_w': 1.0 + nrm(22, (DEPTH, D_MODEL), 0.1),
        'w_up': nrm(23, (DEPTH, D_MODEL, D_FF), D_MODEL ** -0.5),
        'w_down': nrm(24, (DEPTH, D_FF, D_MODEL), D_FF ** -0.5),
        'final_norm_w': 1.0 + nrm(25, (D_MODEL,), 0.1),
    }


def reference(x_prompt, x_sample, cache_mla_ckv, cache_mla_krope, state_gdn, state_gdn_conv,
              attn_norm_w, w_in, a_ln_w, a_ln_b, a_w_s, a_b_s, b_conv_w, b_a_log, b_dt_bias, b_norm_w,
              c_q_norm_w, c_w_uq, c_kv_norm_w, c_w_uk, c_w_uv, w_out, mlp_norm_w, w_up, w_down, final_norm_w):
    pos_p = jnp.arange(x_prompt.shape[1], dtype=jnp.int32)
    pos_s = PAST_LEN + jnp.arange(x_sample.shape[1], dtype=jnp.int32)
    hp, hs = x_prompt, x_sample
    sp_list, ss_list = [], []
    for l in range(DEPTH):
        lw = {'attn_norm_w': attn_norm_w[l], 'w_in': w_in[l], 'a_ln_w': a_ln_w[l], 'a_ln_b': a_ln_b[l],
              'a_w_s': a_w_s[l], 'a_b_s': a_b_s[l], 'b_conv_w': b_conv_w[l], 'b_a_log': b_a_log[l],
              'b_dt_bias': b_dt_bias[l], 'b_norm_w': b_norm_w[l], 'c_q_norm_w': c_q_norm_w[l],
              'c_w_uq': c_w_uq[l], 'c_kv_norm_w': c_kv_norm_w[l], 'c_w_uk': c_w_uk[l], 'c_w_uv': c_w_uv[l],
              'w_out': w_out[l], 'mlp_norm_w': mlp_norm_w[l], 'w_up': w_up[l], 'w_down': w_down[l]}
        hp, sp = layer(hp, pos_p, lw, None)
        hs, ss = layer(hs, pos_s, lw, (cache_mla_ckv[l], cache_mla_krope[l], state_gdn[l], state_gdn_conv[l]))
        sp_list.append(sp)
        ss_list.append(ss)
    y_prompt = rms_norm(hp, final_norm_w)
    y_sample = rms_norm(hs, final_norm_w)

    def stack(lst, i):
        return jnp.stack([s[i] for s in lst], axis=0)

    return (y_prompt, y_sample,
            stack(sp_list, 0), stack(sp_list, 1), stack(sp_list, 2), stack(sp_list, 3),
            stack(ss_list, 0), stack(ss_list, 1), stack(ss_list, 2), stack(ss_list, 3), stack(ss_list, 4))
```

```python
import functools
import math

import jax
import jax.numpy as jnp
from jax import lax
from jax.experimental import pallas as pl
from jax.experimental.pallas import tpu as pltpu

F32 = jnp.float32
BF16 = jnp.bfloat16

D_MODEL = 2048
CHUNK = 64
EPS = 1e-6
LN_EPS = 1e-5
A_GROUPS = 4
A_HEAD = 128
A_WIDTH = 512
A_CHUNK = 128
B_HEADS = 4
B_DK = 128
B_DV = 128
B_WIDTH = 512
B_QKV = 1536
B_CONV = 4
C_HEADS = 8
C_NOPE = 128
C_ROPE = 64
C_VDIM = 128
C_WIDTH = 1024
Q_LORA = 512
KV_LORA = 256
ROPE_THETA = 10000.0
C_SCALE = (C_NOPE + C_ROPE) ** -0.5
D_FF = 4 * D_MODEL

OFF_AU = 0
OFF_AV = OFF_AU + A_WIDTH
OFF_BQKV = OFF_AV + A_WIDTH
OFF_BZ = OFF_BQKV + B_QKV
OFF_BBETA = OFF_BZ + B_WIDTH
OFF_BALPHA = OFF_BBETA + B_HEADS
OFF_CQ = OFF_BALPHA + B_HEADS
OFF_CKV = OFF_CQ + Q_LORA
OFF_CKR = OFF_CKV + KV_LORA
IN_COLS = OFF_CKR + C_ROPE

P_BQKV = 0
P_AU = 1536
P_AV = 2048
P_BZ = 2560
P_CQ = 3072
P_CKV = 3584
P_G1 = 3840
P_G2 = 3968
P_COLS = 4096
LANE = 128
G1_BETA = C_ROPE
G1_ALPHA = C_ROPE + B_HEADS
KCAT = KV_LORA + 2 * C_ROPE

VMEM_LIMIT = 56 * 1024 * 1024


def _mm(a, b):
    return jnp.dot(a, b, preferred_element_type=F32)


def _mm_nt(a, b):
    return lax.dot_general(a, b, (((1,), (1,)), ((), ())), preferred_element_type=F32)


def _split2(x):
    hi = x.astype(BF16)
    lo = (x - hi.astype(F32)).astype(BF16)
    return hi, lo


def _split3(x):
    hi = x.astype(BF16)
    r = x - hi.astype(F32)
    mid = r.astype(BF16)
    lo = (r - mid.astype(F32)).astype(BF16)
    return hi, mid, lo


def _mm3(a, b):
    ah, al = _split2(a)
    bh, bl = _split2(b)
    return _mm(ah, bh) + (_mm(ah, bl) + _mm(al, bh))


def _mm_exact_lhs(a, b):
    b1, b2, b3 = _split3(b)
    return _mm(a, b1) + (_mm(a, b2) + _mm(a, b3))


def _mm_nt_exact_lhs(a, b):
    b1, b2, b3 = _split3(b)
    return _mm_nt(a, b1) + (_mm_nt(a, b2) + _mm_nt(a, b3))


def _params(sem):
    return pltpu.CompilerParams(dimension_semantics=sem, vmem_limit_bytes=VMEM_LIMIT)


def _inproj_kernel(x_ref, nw_ref, w_ref, o_ref):
    x = x_ref[...]
    ms = jnp.mean(x * x, axis=-1, keepdims=True)
    xn = (x * lax.rsqrt(ms + EPS) * nw_ref[...]).astype(BF16)
    o_ref[...] = _mm(xn, w_ref[...])


def _inproj(x, norm_w, w_pad):
    m = x.shape[0]
    tm = min(m, 512)
    tn = 1024
    return pl.pallas_call(
        _inproj_kernel,
        out_shape=jax.ShapeDtypeStruct((m, P_COLS), F32),
        grid=(P_COLS // tn, m // tm),
        in_specs=[pl.BlockSpec((tm, D_MODEL), lambda j, i: (i, 0)),
                  pl.BlockSpec((1, D_MODEL), lambda j, i: (0, 0)),
                  pl.BlockSpec((D_MODEL, tn), lambda j, i: (0, j))],
        out_specs=pl.BlockSpec((tm, tn), lambda j, i: (i, j)),
        compiler_params=_params(("arbitrary", "arbitrary")),
        name="inproj",
    )(x, norm_w.reshape(1, D_MODEL), w_pad)


def _gelu(x):
    return 0.5 * x * (1.0 + lax.erf(x * (2.0 ** -0.5)))


def _mixa_kernel(u_ref, v_ref, lnw_ref, lnb_ref, ws_ref, bs_ref, ya_ref, *va_ref, lc, nsub):
    u = _gelu(u_ref[0])
    v = _gelu(v_ref[0])
    mu = jnp.mean(v, axis=-1, keepdims=True)
    vc = v - mu
    var = jnp.mean(vc * vc, axis=-1, keepdims=True)
    vn = vc * lax.rsqrt(var + LN_EPS) * lnw_ref[...] + lnb_ref[...]
    if va_ref:
        va_ref[0][0] = vn
    for s in range(nsub):
        rows = slice(s * lc, (s + 1) * lc)
        for g in range(A_GROUPS):
            cols = slice(g * A_HEAD, (g + 1) * A_HEAD)
            f = _mm(ws_ref[g], vn[rows, cols].astype(BF16)) + bs_ref[g]
            ya_ref[0, rows, cols] = (u[rows, cols] * f).astype(BF16)


def _mixer_a(proj, ln_w, ln_b, w_s, b_s, want_va):
    bsz, L, _ = proj.shape
    lc = min(L, A_CHUNK)
    rows = min(L, 512)
    nsub = rows // lc
    idx = jnp.arange(lc)
    allowed = (idx[:, None] // CHUNK) >= (idx[None, :] // CHUNK)
    ws = jnp.where(allowed[None], w_s[:, :lc, :lc], 0.0).astype(BF16)
    bs = jnp.broadcast_to(b_s[:, :lc, None], (A_GROUPS, lc, A_HEAD)).astype(F32)
    out_shape = [jax.ShapeDtypeStruct((bsz, L, A_WIDTH), BF16)]
    out_specs = [pl.BlockSpec((1, rows, A_WIDTH), lambda b, i: (b, i, 0))]
    if want_va:
        out_shape.append(jax.ShapeDtypeStruct((bsz, L, A_WIDTH), F32))
        out_specs.append(pl.BlockSpec((1, rows, A_WIDTH), lambda b, i: (b, i, 0)))
    res = pl.pallas_call(
        functools.partial(_mixa_kernel, lc=lc, nsub=nsub),
        out_shape=out_shape,
        grid=(bsz, L // rows),
        in_specs=[pl.BlockSpec((1, rows, A_WIDTH), lambda b, i: (b, i, P_AU // A_WIDTH)),
                  pl.BlockSpec((1, rows, A_WIDTH), lambda b, i: (b, i, P_AV // A_WIDTH)),
                  pl.BlockSpec((1, A_WIDTH), lambda b, i: (0, 0)),
                  pl.BlockSpec((1, A_WIDTH), lambda b, i: (0, 0)),
                  pl.BlockSpec((A_GROUPS, lc, lc), lambda b, i: (0, 0, 0)),
                  pl.BlockSpec((A_GROUPS, lc, A_HEAD), lambda b, i: (0, 0, 0))],
        out_specs=out_specs,
        compiler_params=_params(("arbitrary", "arbitrary")),
        name="mixer_a",
    )(proj, proj, ln_w.reshape(1, A_WIDTH), ln_b.reshape(1, A_WIDTH), ws, bs)
    return res if want_va else (res[0], None)


def _softplus(x):
    return jnp.maximum(x, 0.0) + jnp.log1p(jnp.exp(-jnp.abs(x)))


def _gdn_prep_kernel(x_ref, halo_ref, buf_ref, g1_ref, cw_ref, alog_ref, dtb_ref,
                     u_ref, w_ref, qg_ref, kgt_ref, qk_ref, eg_ref, xp_scr, *, rows, chunk):
    i = pl.program_id(1)
    nsub = rows // chunk
    x = x_ref[0]
    prev = jnp.where(i == 0, buf_ref[0], halo_ref[0])
    xp_scr[0:8, :] = prev
    xp_scr[8:8 + rows, :] = x
    y = x * cw_ref[B_CONV - 1:B_CONV, :]
    for k in range(1, B_CONV):
        y = y + xp_scr[8 - k:8 - k + rows, :] * cw_ref[B_CONV - 1 - k:B_CONV - k, :]
    y = y * jax.nn.sigmoid(y)

    g1 = g1_ref[0]
    beta_t = jax.nn.sigmoid(g1)
    g_t = -jnp.exp(alog_ref[...]) * _softplus(g1 + dtb_ref[...])

    ri = lax.broadcasted_iota(jnp.int32, (rows, rows), 0)
    ci = lax.broadcasted_iota(jnp.int32, (rows, rows), 1)
    same = (ri // chunk) == (ci // chunk)
    causal = same & (ri >= ci)
    strict = same & (ri > ci)
    tri = jnp.where(causal, 1.0, 0.0).astype(BF16)
    eye_r = jnp.where(ri == ci, 1.0, 0.0).astype(F32)
    li = lax.broadcasted_iota(jnp.int32, (LANE, LANE), 0)
    lj = lax.broadcasted_iota(jnp.int32, (LANE, LANE), 1)
    eye_l = jnp.where(li == lj, 1.0, 0.0).astype(BF16)

    gc = _mm_exact_lhs(tri, g_t)
    gct = _mm_nt_exact_lhs(eye_l, gc)
    glast = jnp.concatenate(
        [jnp.broadcast_to(gc[(s + 1) * chunk - 1:(s + 1) * chunk, :], (chunk, LANE)) for s in range(nsub)], axis=0)
    eg_full = jnp.exp(glast)

    nsq = int(math.log2(chunk)) - 1
    for h in range(B_HEADS):
        hs = slice(h * B_DK, (h + 1) * B_DK)
        bcol = beta_t[:, G1_BETA + h:G1_BETA + h + 1]
        gcol = gc[:, G1_ALPHA + h:G1_ALPHA + h + 1]
        grow = gct[G1_ALPHA + h:G1_ALPHA + h + 1, :]
        glcol = glast[:, G1_ALPHA + h:G1_ALPHA + h + 1]
        qh = y[:, hs]
        qh = qh * lax.rsqrt(jnp.sum(qh * qh, axis=-1, keepdims=True) + EPS) * (B_DK ** -0.5)
        kh = y[:, B_WIDTH + h * B_DK:B_WIDTH + (h + 1) * B_DK]
        kh = kh * lax.rsqrt(jnp.sum(kh * kh, axis=-1, keepdims=True) + EPS)
        vh = y[:, 2 * B_WIDTH + h * B_DV:2 * B_WIDTH + (h + 1) * B_DV]
        kb = kh * bcol
        khb = kh.astype(BF16)
        kk = _mm_nt(kb.astype(BF16), khb)
        qkm = _mm_nt(qh.astype(BF16), khb)
        dec = jnp.where(causal, jnp.exp(jnp.minimum(gcol - grow, 0.0)), 0.0)
        p = jnp.where(strict, -(kk * dec), 0.0)
        t = eye_r + p
        mpow = p
        for _ in range(nsq):
            mpow = _mm3(mpow, mpow)
            t = t + _mm3(t, mpow)
        egc = jnp.exp(gcol)
        rhs = jnp.concatenate([vh * bcol, kb * egc], axis=1)
        sol = _mm3(t, rhs)
        u_ref[0, :, hs] = sol[:, :B_DV]
        w_ref[0, :, hs] = sol[:, B_DV:].astype(BF16)
        qg_ref[0, :, hs] = (qh * egc).astype(BF16)
        kg = (kh * jnp.exp(glcol - gcol)).astype(BF16)
        kgt_ref[0, hs, :] = _mm_nt(eye_l, kg).astype(BF16)
        qk_ref[0, :, h * rows:(h + 1) * rows] = (qkm * dec).astype(BF16)
        for s in range(nsub):
            eg_ref[0, s, h:h + 1, :] = jnp.broadcast_to(
                eg_full[s * chunk:s * chunk + 1, G1_ALPHA + h:G1_ALPHA + h + 1], (1, LANE))


def _gdn_prep(proj, buf8, conv_w, a_log, dt_bias, rows, chunk):
    bsz, L, _ = proj.shape
    nblk = L // rows
    rpb = max(rows // 8, 1)
    alog = jnp.zeros((1, LANE), F32).at[0, G1_ALPHA:G1_ALPHA + B_HEADS].set(a_log)
    dtb = jnp.zeros((1, LANE), F32).at[0, G1_ALPHA:G1_ALPHA + B_HEADS].set(dt_bias)
    return pl.pallas_call(
        functools.partial(_gdn_prep_kernel, rows=rows, chunk=chunk),
        out_shape=[jax.ShapeDtypeStruct((bsz, L, B_WIDTH), F32),
                   jax.ShapeDtypeStruct((bsz, L, B_WIDTH), BF16),
                   jax.ShapeDtypeStruct((bsz, L, B_WIDTH), BF16),
                   jax.ShapeDtypeStruct((bsz, B_WIDTH, L), BF16),
                   jax.ShapeDtypeStruct((bsz, L, B_HEADS * rows), BF16),
                   jax.ShapeDtypeStruct((bsz, L // chunk, B_HEADS, LANE), F32)],
        grid=(bsz, nblk),
        in_specs=[pl.BlockSpec((1, rows, B_QKV), lambda b, i: (b, i, 0)),
                  pl.BlockSpec((1, 8, B_QKV), lambda b, i: (b, jnp.maximum(i * rpb - 1, 0), 0)),
                  pl.BlockSpec((1, 8, B_QKV), lambda b, i: (b, 0, 0)),
                  pl.BlockSpec((1, rows, LANE), lambda b, i: (b, i, P_G1 // LANE)),
                  pl.BlockSpec((B_CONV, B_QKV), lambda b, i: (0, 0)),
                  pl.BlockSpec((1, LANE), lambda b, i: (0, 0)),
                  pl.BlockSpec((1, LANE), lambda b, i: (0, 0))],
        out_specs=[pl.BlockSpec((1, rows, B_WIDTH), lambda b, i: (b, i, 0)),
                   pl.BlockSpec((1, rows, B_WIDTH), lambda b, i: (b, i, 0)),
                   pl.BlockSpec((1, rows, B_WIDTH), lambda b, i: (b, i, 0)),
                   pl.BlockSpec((1, B_WIDTH, rows), lambda b, i: (b, 0, i)),
                   pl.BlockSpec((1, rows, B_HEADS * rows), lambda b, i: (b, i, 0)),
                   pl.BlockSpec((1, rows // chunk, B_HEADS, LANE), lambda b, i: (b, i, 0, 0))],
        scratch_shapes=[pltpu.VMEM((rows + 8, B_QKV), F32)],
        compiler_params=_params(("arbitrary", "arbitrary")),
        name="gdn_prep",
    )(proj, proj, buf8, proj, conv_w, alog, dtb)


def _gdn_scan_kernel(u_ref, w_ref, qg_ref, kgt_ref, qk_ref, eg_ref, z_ref, nw_ref, s0_ref,
                     yb_ref, s_ref, *, bsz, rows, chunk):
    @pl.when(pl.program_id(0) == 0)
    def _():
        s_ref[...] = s0_ref[...]

    nsub = rows // chunk
    for b in range(bsz):
        for h in range(B_HEADS):
            hs = slice(h * B_DV, (h + 1) * B_DV)
            st = s_ref[b, h]
            for s in range(nsub):
                rs = slice(s * chunk, (s + 1) * chunk)
                sb = st.astype(BF16)
                wq = jnp.concatenate([w_ref[b, rs, hs], qg_ref[b, rs, hs]], axis=0)
                r = _mm(wq, sb)
                v_new = u_ref[b, rs, hs] - r[:chunk]
                vb = v_new.astype(BF16)
                o = r[chunk:] + _mm(qk_ref[b, rs, h * rows + s * chunk:h * rows + (s + 1) * chunk], vb)
                st = st * eg_ref[b, s, h:h + 1, :] + _mm(kgt_ref[b, hs, rs], vb)
                z = z_ref[b, rs, hs]
                on = o * lax.rsqrt(jnp.mean(o * o, axis=-1, keepdims=True) + EPS) * nw_ref[...]
                yb_ref[b, rs, hs] = (on * (z * jax.nn.sigmoid(z))).astype(BF16)
            s_ref[b, h] = st


def _gdn_scan(prep, proj, norm_w, s0, rows, chunk):
    u, w, qg, kgt, qk, eg = prep
    bsz, L, _ = u.shape
    nblk = L // rows
    return pl.pallas_call(
        functools.partial(_gdn_scan_kernel, bsz=bsz, rows=rows, chunk=chunk),
        out_shape=[jax.ShapeDtypeStruct((bsz, L, B_WIDTH), BF16),
                   jax.ShapeDtypeStruct((bsz, B_HEADS, B_DK, B_DV), F32)],
        grid=(nblk,),
        in_specs=[pl.BlockSpec((bsz, rows, B_WIDTH), lambda i: (0, i, 0)),
                  pl.BlockSpec((bsz, rows, B_WIDTH), lambda i: (0, i, 0)),
                  pl.BlockSpec((bsz, rows, B_WIDTH), lambda i: (0, i, 0)),
                  pl.BlockSpec((bsz, B_WIDTH, rows), lambda i: (0, 0, i)),
                  pl.BlockSpec((bsz, rows, B_HEADS * rows), lambda i: (0, i, 0)),
                  pl.BlockSpec((bsz, rows // chunk, B_HEADS, LANE), lambda i: (0, i, 0, 0)),
                  pl.BlockSpec((bsz, rows, B_WIDTH), lambda i: (0, i, P_BZ // B_WIDTH)),
                  pl.BlockSpec((1, B_DV), lambda i: (0, 0)),
                  pl.BlockSpec((bsz, B_HEADS, B_DK, B_DV), lambda i: (0, 0, 0, 0))],
        out_specs=[pl.BlockSpec((bsz, rows, B_WIDTH), lambda i: (0, i, 0)),
                   pl.BlockSpec((bsz, B_HEADS, B_DK, B_DV), lambda i: (0, 0, 0, 0))],
        compiler_params=_params(("arbitrary",)),
        name="gdn_scan",
    )(u, w, qg, kgt, qk, eg, proj, norm_w.reshape(1, B_DV), s0)


def _mla_proj_kernel(cq_ref, ckv_ref, g1_ref, g2_ref, cos8_ref, sin8_ref, qnw_ref, wqn_ref, wqr_ref, wqs_ref,
                     kvnw_ref, wukt_ref, ckv_out, kr_out, kcat_out, q_out):
    cq = cq_ref[0]
    cqn = (cq * lax.rsqrt(jnp.mean(cq * cq, axis=-1, keepdims=True) + EPS) * qnw_ref[...]).astype(BF16)
    c = ckv_ref[0]
    ckv = c * lax.rsqrt(jnp.mean(c * c, axis=-1, keepdims=True) + EPS) * kvnw_ref[...]
    ckv_out[0] = ckv
    cos64 = cos8_ref[:, 0:C_ROPE]
    sin64 = sin8_ref[:, 0:C_ROPE]
    kr = g1_ref[0][:, 0:C_ROPE] * cos64 + g2_ref[0][:, 0:C_ROPE] * sin64
    kr_out[0] = kr
    kcat_out[0, :, 0:KV_LORA] = ckv.astype(BF16)
    kcat_out[0, :, KV_LORA:KV_LORA + C_ROPE] = kr.astype(BF16)
    kcat_out[0, :, KV_LORA + C_ROPE:KCAT] = kr.astype(BF16)

    qn = _mm(cqn, wqn_ref[...])
    qr = (_mm(cqn, wqr_ref[...]) * cos8_ref[...] + _mm(cqn, wqs_ref[...]) * sin8_ref[...]) * C_SCALE
    lane = lax.broadcasted_iota(jnp.int32, (1, LANE), 1)
    for h in range(C_HEADS):
        ql = _mm(qn[:, h * C_NOPE:(h + 1) * C_NOPE].astype(BF16), wukt_ref[h]) * C_SCALE
        q_out[0, h, :, 0:KV_LORA] = ql.astype(BF16)
        pair = qr[:, (h // 2) * LANE:(h // 2 + 1) * LANE]
        keep = (lane // C_ROPE) == (h % 2)
        q_out[0, h, :, KV_LORA:KCAT] = jnp.where(keep, pair, 0.0).astype(BF16)


def _mla_proj(proj, cos8, sin8, q_norm_w, wqn, wqr, wqs, kv_norm_w, wukt):
    bsz, L, _ = proj.shape
    tm = min(L, 256)
    const2 = lambda b, i: (0, 0)
    return pl.pallas_call(
        _mla_proj_kernel,
        out_shape=[jax.ShapeDtypeStruct((bsz, L, KV_LORA), F32),
                   jax.ShapeDtypeStruct((bsz, L, C_ROPE), F32),
                   jax.ShapeDtypeStruct((bsz, L, KCAT), BF16),
                   jax.ShapeDtypeStruct((bsz, C_HEADS, L, KCAT), BF16)],
        grid=(bsz, L // tm),
        in_specs=[pl.BlockSpec((1, tm, Q_LORA), lambda b, i: (b, i, P_CQ // Q_LORA)),
                  pl.BlockSpec((1, tm, KV_LORA), lambda b, i: (b, i, P_CKV // KV_LORA)),
                  pl.BlockSpec((1, tm, LANE), lambda b, i: (b, i, P_G1 // LANE)),
                  pl.BlockSpec((1, tm, LANE), lambda b, i: (b, i, P_G2 // LANE)),
                  pl.BlockSpec((tm, C_HEADS * C_ROPE), lambda b, i: (i, 0)),
                  pl.BlockSpec((tm, C_HEADS * C_ROPE), lambda b, i: (i, 0)),
                  pl.BlockSpec((1, Q_LORA), const2),
                  pl.BlockSpec((Q_LORA, C_HEADS * C_NOPE), const2),
                  pl.BlockSpec((Q_LORA, C_HEADS * C_ROPE), const2),
                  pl.BlockSpec((Q_LORA, C_HEADS * C_ROPE), const2),
                  pl.BlockSpec((1, KV_LORA), const2),
                  pl.BlockSpec((C_HEADS, C_NOPE, KV_LORA), lambda b, i: (0, 0, 0))],
        out_specs=[pl.BlockSpec((1, tm, KV_LORA), lambda b, i: (b, i, 0)),
                   pl.BlockSpec((1, tm, C_ROPE), lambda b, i: (b, i, 0)),
                   pl.BlockSpec((1, tm, KCAT), lambda b, i: (b, i, 0)),
                   pl.BlockSpec((1, C_HEADS, tm, KCAT), lambda b, i: (b, 0, i, 0))],
        compiler_params=_params(("arbitrary", "arbitrary")),
        name="mla_proj",
    )(proj, proj, proj, proj, cos8, sin8, q_norm_w.reshape(1, Q_LORA), wqn, wqr, wqs,
      kv_norm_w.reshape(1, KV_LORA), wukt)


def _attn_finish(acc, l, wuv_ref, yc_ref, tq):
    o = acc * (1.0 / l)
    for h in range(C_HEADS):
        oh = o[h * tq:(h + 1) * tq].astype(BF16)
        yc_ref[0, :, h * C_VDIM:(h + 1) * C_VDIM] = _mm(oh, wuv_ref[h]).astype(BF16)


def _attn_prompt_kernel(q_ref, k_ref, wuv_ref, yc_ref, m_scr, l_scr, acc_scr, *, tq):
    qi = pl.program_id(1)
    rows = C_HEADS * tq
    q = q_ref[0].reshape(rows, KCAT)
    m_scr[...] = jnp.full((rows, 1), -1e30, F32)
    l_scr[...] = jnp.zeros((rows, 1), F32)
    acc_scr[...] = jnp.zeros((rows, KV_LORA), F32)

    def block(kj, masked):
        k = k_ref[0, pl.ds(pl.multiple_of(kj * tq, tq), tq), :]
        s = _mm_nt(q, k)
        if masked:
            ri = lax.broadcasted_iota(jnp.int32, (rows, tq), 0)
            ci = lax.broadcasted_iota(jnp.int32, (rows, tq), 1)
            ok = ((ri % tq) // CHUNK) >= (ci // CHUNK)
            s = jnp.where(ok, s, -jnp.inf)
        m_prev = m_scr[...]
        m_new = jnp.maximum(m_prev, jnp.max(s, axis=-1, keepdims=True))
        alpha = jnp.exp(m_prev - m_new)
        p = jnp.exp(s - m_new)
        l_scr[...] = alpha * l_scr[...] + jnp.sum(p, axis=-1, keepdims=True)
        acc_scr[...] = acc_scr[...] * alpha + _mm(p.astype(BF16), k[:, 0:KV_LORA])
        m_scr[...] = m_new

    def body(kj, carry):
        block(kj, False)
        return carry

    lax.fori_loop(0, qi, body, 0)
    block(qi, True)
    _attn_finish(acc_scr[...], l_scr[...], wuv_ref, yc_ref, tq)


def _attn_prompt(q, kcat, wuv):
    bsz, _, L, _ = q.shape
    tq = 128
    rows = C_HEADS * tq
    return pl.pallas_call(
        functools.partial(_attn_prompt_kernel, tq=tq),
        out_shape=jax.ShapeDtypeStruct((bsz, L, C_WIDTH), BF16),
        grid=(bsz, L // tq),
        in_specs=[pl.BlockSpec((1, C_HEADS, tq, KCAT), lambda b, i: (b, 0, i, 0)),
                  pl.BlockSpec((1, L, KCAT), lambda b, i: (b, 0, 0)),
                  pl.BlockSpec((C_HEADS, KV_LORA, C_VDIM), lambda b, i: (0, 0, 0))],
        out_specs=pl.BlockSpec((1, tq, C_WIDTH), lambda b, i: (b, i, 0)),
        scratch_shapes=[pltpu.VMEM((rows, 1), F32), pltpu.VMEM((rows, 1), F32),
                        pltpu.VMEM((rows, KV_LORA), F32)],
        compiler_params=_params(("arbitrary", "arbitrary")),
        name="attn_prompt",
    )(q, kcat, wuv)


def _attn_full_kernel(q_ref, k_ref, wuv_ref, yc_ref, *, tq, nkeys):
    rows = C_HEADS * tq
    q = q_ref[0].reshape(rows, KCAT)
    k = k_ref[0]
    s = _mm_nt(q, k)
    ci = lax.broadcasted_iota(jnp.int32, s.shape, 1)
    s = jnp.where(ci < nkeys, s, -jnp.inf)
    m = jnp.max(s, axis=-1, keepdims=True)
    p = jnp.exp(s - m)
    l = jnp.sum(p, axis=-1, keepdims=True)
    acc = _mm(p.astype(BF16), k[:, 0:KV_LORA])
    _attn_finish(acc, l, wuv_ref, yc_ref, tq)


def _attn_full(q, kcat, wuv, nkeys):
    bsz, _, tq, _ = q.shape
    npad = kcat.shape[1]
    return pl.pallas_call(
        functools.partial(_attn_full_kernel, tq=tq, nkeys=nkeys),
        out_shape=jax.ShapeDtypeStruct((bsz, tq, C_WIDTH), BF16),
        grid=(bsz,),
        in_specs=[pl.BlockSpec((1, C_HEADS, tq, KCAT), lambda b: (b, 0, 0, 0)),
                  pl.BlockSpec((1, npad, KCAT), lambda b: (b, 0, 0)),
                  pl.BlockSpec((C_HEADS, KV_LORA, C_VDIM), lambda b: (0, 0, 0))],
        out_specs=pl.BlockSpec((1, tq, C_WIDTH), lambda b: (b, 0, 0)),
        compiler_params=_params(("arbitrary",)),
        name="attn_full",
    )(q, kcat, wuv)


def _outproj_kernel(x_ref, ya_ref, yb_ref, yc_ref, wo_ref, o_ref):
    tn = 512
    for j in range(0, D_MODEL, tn):
        acc = _mm(ya_ref[...], wo_ref[0:A_WIDTH, j:j + tn])
        acc = acc + _mm(yb_ref[...], wo_ref[A_WIDTH:A_WIDTH + B_WIDTH, j:j + tn])
        acc = acc + _mm(yc_ref[...], wo_ref[A_WIDTH + B_WIDTH:D_MODEL, j:j + tn])
        o_ref[:, j:j + tn] = x_ref[:, j:j + tn] + acc


def _outproj(x, ya, yb, yc, wo):
    m = x.shape[0]
    tm = min(m, 256)
    row = lambda i: (i, 0)
    return pl.pallas_call(
        _outproj_kernel,
        out_shape=jax.ShapeDtypeStruct((m, D_MODEL), F32),
        grid=(m // tm,),
        in_specs=[pl.BlockSpec((tm, D_MODEL), row),
                  pl.BlockSpec((tm, A_WIDTH), row),
                  pl.BlockSpec((tm, B_WIDTH), row),
                  pl.BlockSpec((tm, C_WIDTH), row),
                  pl.BlockSpec((D_MODEL, D_MODEL), lambda i: (0, 0))],
        out_specs=pl.BlockSpec((tm, D_MODEL), row),
        compiler_params=_params(("arbitrary",)),
        name="outproj",
    )(x, ya, yb, yc, wo)


def _ffn_kernel(x_ref, nw_ref, wu_ref, wd_ref, fw_ref, o_ref, xn_scr, acc_scr, *, final_norm):
    f = pl.program_id(1)

    @pl.when(f == 0)
    def _():
        x = x_ref[...]
        ms = jnp.mean(x * x, axis=-1, keepdims=True)
        xn_scr[...] = (x * lax.rsqrt(ms + EPS) * nw_ref[...]).astype(BF16)
        acc_scr[...] = jnp.zeros_like(acc_scr)

    hid = jnp.maximum(_mm(xn_scr[...], wu_ref[...]), 0.0)
    acc_scr[...] += _mm((hid * hid).astype(BF16), wd_ref[...])

    @pl.when(f == pl.num_programs(1) - 1)
    def _():
        y = x_ref[...] + acc_scr[...]
        if final_norm:
            ms = jnp.mean(y * y, axis=-1, keepdims=True)
            y = y * lax.rsqrt(ms + EPS) * fw_ref[...]
        o_ref[...] = y


def _ffn(x, norm_w, wu, wd, final_w, final_norm):
    m = x.shape[0]
    tm = min(m, 512)
    tf = 512
    return pl.pallas_call(
        functools.partial(_ffn_kernel, final_norm=final_norm),
        out_shape=jax.ShapeDtypeStruct((m, D_MODEL), F32),
        grid=(m // tm, D_FF // tf),
        in_specs=[pl.BlockSpec((tm, D_MODEL), lambda i, f: (i, 0)),
                  pl.BlockSpec((1, D_MODEL), lambda i, f: (0, 0)),
                  pl.BlockSpec((D_MODEL, tf), lambda i, f: (0, f)),
                  pl.BlockSpec((tf, D_MODEL), lambda i, f: (f, 0)),
                  pl.BlockSpec((1, D_MODEL), lambda i, f: (0, 0))],
        out_specs=pl.BlockSpec((tm, D_MODEL), lambda i, f: (i, 0)),
        scratch_shapes=[pltpu.VMEM((tm, D_MODEL), BF16), pltpu.VMEM((tm, D_MODEL), F32)],
        compiler_params=_params(("arbitrary", "arbitrary")),
        name="ffn",
    )(x, norm_w.reshape(1, D_MODEL), wu, wd, final_w.reshape(1, D_MODEL))


def _swap_halves(w):
    half = C_ROPE // 2
    return jnp.concatenate([w[..., half:], w[..., :half]], axis=-1)


def _prep_w_in(w_in):
    k = w_in.shape[0]
    ckr = w_in[:, OFF_CKR:IN_COLS]
    g1 = jnp.concatenate([ckr, w_in[:, OFF_BBETA:OFF_CQ], jnp.zeros((k, LANE - C_ROPE - 2 * B_HEADS), F32)], axis=1)
    g2 = jnp.concatenate([_swap_halves(ckr), jnp.zeros((k, LANE - C_ROPE), F32)], axis=1)
    parts = [w_in[:, OFF_BQKV:OFF_BZ], w_in[:, OFF_AU:OFF_AV], w_in[:, OFF_AV:OFF_BQKV], w_in[:, OFF_BZ:OFF_BBETA],
             w_in[:, OFF_CQ:OFF_CKV], w_in[:, OFF_CKV:OFF_CKR], g1, g2]
    return jnp.concatenate(parts, axis=1).astype(BF16)


def _rope_tables(pos):
    half = C_ROPE // 2
    inv = ROPE_THETA ** (-jnp.arange(half, dtype=F32) / half)
    ang = pos.astype(F32)[:, None] * inv[None, :]
    cos, sin = jnp.cos(ang), jnp.sin(ang)
    cos8 = jnp.tile(jnp.concatenate([cos, cos], axis=1), (1, C_HEADS))
    sin8 = jnp.tile(jnp.concatenate([-sin, sin], axis=1), (1, C_HEADS))
    return cos8, sin8


def _layer(x, bsz, L, tabs, lw, past, final_w, final_norm):
    proj = _inproj(x, lw['attn_norm_w'], lw['w_in']).reshape(bsz, L, P_COLS)

    ya, va = _mixer_a(proj, lw['a_ln_w'], lw['a_ln_b'], lw['a_w_s'], lw['a_b_s'], past is not None)

    if past is None:
        rows, chunk = 2 * CHUNK, CHUNK
        buf8 = jnp.zeros((bsz, 8, B_QKV), F32)
        s0 = jnp.zeros((bsz, B_HEADS, B_DK, B_DV), F32)
    else:
        rows, chunk = L, L
        ckv_past, kr_past, s0, conv_buf = past
        buf8 = jnp.concatenate([jnp.zeros((bsz, 8 - (B_CONV - 1), B_QKV), F32), conv_buf], axis=1)
    prep = _gdn_prep(proj, buf8, lw['b_conv_w'], lw['b_a_log'], lw['b_dt_bias'], rows, chunk)
    yb, s_new = _gdn_scan(prep, proj, lw['b_norm_w'], s0, rows, chunk)
    buf_new = proj[:, L - (B_CONV - 1):, P_BQKV:P_BQKV + B_QKV]

    cos8, sin8 = tabs
    ckv, kr, kcat, q = _mla_proj(proj, cos8, sin8, lw['c_q_norm_w'], lw['wqn'], lw['wqr'], lw['wqs'],
                                 lw['c_kv_norm_w'], lw['wukt'])
    if past is None:
        yc = _attn_prompt(q, kcat, lw['wuv'])
    else:
        past_len = ckv_past.shape[1]
        nkeys = past_len + L
        npad = -(-nkeys // LANE) * LANE
        kpast = jnp.concatenate([ckv_past, kr_past, kr_past], axis=-1).astype(BF16)
        kall = jnp.concatenate([kpast, kcat, jnp.zeros((bsz, npad - nkeys, KCAT), BF16)], axis=1)
        yc = _attn_full(q, kall, lw['wuv'], nkeys)

    m = bsz * L
    x = _outproj(x, ya.reshape(m, A_WIDTH), yb.reshape(m, B_WIDTH), yc.reshape(m, C_WIDTH), lw['w_out'])
    x = _ffn(x, lw['mlp_norm_w'], lw['w_up'], lw['w_down'], final_w, final_norm)
    return x, (ckv, kr, s_new, buf_new, va)


def kernel(x_prompt, x_sample, cache_mla_ckv, cache_mla_krope, state_gdn, state_gdn_conv,
           attn_norm_w, w_in, a_ln_w, a_ln_b, a_w_s, a_b_s, b_conv_w, b_a_log, b_dt_bias, b_norm_w,
           c_q_norm_w, c_w_uq, c_kv_norm_w, c_w_uk, c_w_uv, w_out, mlp_norm_w, w_up, w_down, final_norm_w):
    depth = w_in.shape[0]
    bp, lp, _ = x_prompt.shape
    bs, ls, _ = x_sample.shape
    past_len = cache_mla_ckv.shape[2]
    tabs_p = _rope_tables(jnp.arange(lp, dtype=jnp.int32))
    tabs_s = _rope_tables(past_len + jnp.arange(ls, dtype=jnp.int32))
    hp = x_prompt.reshape(bp * lp, D_MODEL)
    hs = x_sample.reshape(bs * ls, D_MODEL)
    sp_list, ss_list = [], []
    for l in range(depth):
        wq_rope = c_w_uq[l][:, :, C_NOPE:]
        lw = {
            'attn_norm_w': attn_norm_w[l], 'w_in': _prep_w_in(w_in[l]),
            'a_ln_w': a_ln_w[l], 'a_ln_b': a_ln_b[l], 'a_w_s': a_w_s[l], 'a_b_s': a_b_s[l],
            'b_conv_w': b_conv_w[l], 'b_a_log': b_a_log[l], 'b_dt_bias': b_dt_bias[l], 'b_norm_w': b_norm_w[l],
            'c_q_norm_w': c_q_norm_w[l], 'c_kv_norm_w': c_kv_norm_w[l],
            'wqn': c_w_uq[l][:, :, :C_NOPE].reshape(Q_LORA, C_HEADS * C_NOPE).astype(BF16),
            'wqr': wq_rope.reshape(Q_LORA, C_HEADS * C_ROPE).astype(BF16),
            'wqs': _swap_halves(wq_rope).reshape(Q_LORA, C_HEADS * C_ROPE).astype(BF16),
            'wukt': jnp.transpose(c_w_uk[l], (1, 2, 0)).astype(BF16),
            'wuv': jnp.transpose(c_w_uv[l], (1, 0, 2)).astype(BF16),
            'w_out': w_out[l].astype(BF16), 'mlp_norm_w': mlp_norm_w[l],
            'w_up': w_up[l].astype(BF16), 'w_down': w_down[l].astype(BF16),
        }
        last = l == depth - 1
        hp, sp = _layer(hp, bp, lp, tabs_p, lw, None, final_norm_w, last)
        hs, ss = _layer(hs, bs, ls, tabs_s, lw,
                        (cache_mla_ckv[l], cache_mla_krope[l], state_gdn[l], state_gdn_conv[l]), final_norm_w, last)
        sp_list.append(sp)
        ss_list.append(ss)

    def stack(lst, i):
        return jnp.stack([s[i] for s in lst], axis=0)

    return (hp.reshape(bp, lp, D_MODEL), hs.reshape(bs, ls, D_MODEL),
            stack(sp_list, 0), stack(sp_list, 1), stack(sp_list, 2), stack(sp_list, 3),
            stack(ss_list, 0), stack(ss_list, 1), stack(ss_list, 2), stack(ss_list, 3), stack(ss_list, 4))
```

```python
import functools
import math

import jax
import jax.numpy as jnp
from jax import lax
from jax.experimental import pallas as pl
from jax.experimental.pallas import tpu as pltpu

F32 = jnp.float32
BF16 = jnp.bfloat16

D_MODEL = 2048
CHUNK = 64
EPS = 1e-6
LN_EPS = 1e-5
A_GROUPS = 4
A_HEAD = 128
A_WIDTH = 512
A_CHUNK = 128
B_HEADS = 4
B_DK = 128
B_DV = 128
B_WIDTH = 512
B_QKV = 1536
B_CONV = 4
C_HEADS = 8
C_NOPE = 128
C_ROPE = 64
C_VDIM = 128
C_WIDTH = 1024
Q_LORA = 512
KV_LORA = 256
ROPE_THETA = 10000.0
C_SCALE = (C_NOPE + C_ROPE) ** -0.5
Q_SCALE = C_SCALE * math.log2(math.e)
D_FF = 4 * D_MODEL

OFF_AU = 0
OFF_AV = OFF_AU + A_WIDTH
OFF_BQKV = OFF_AV + A_WIDTH
OFF_BZ = OFF_BQKV + B_QKV
OFF_BBETA = OFF_BZ + B_WIDTH
OFF_BALPHA = OFF_BBETA + B_HEADS
OFF_CQ = OFF_BALPHA + B_HEADS
OFF_CKV = OFF_CQ + Q_LORA
OFF_CKR = OFF_CKV + KV_LORA
IN_COLS = OFF_CKR + C_ROPE

P_BQKV = 0
P_AU = 1536
P_AV = 2048
P_BZ = 2560
P_CQ = 3072
P_CKV = 3584
P_G1 = 3840
P_G2 = 3968
P_COLS = 4096
LANE = 128
G1_BETA = C_ROPE
G1_ALPHA = C_ROPE + B_HEADS
KCAT = KV_LORA + 2 * C_ROPE

VMEM_LIMIT = 56 * 1024 * 1024


def _mm(a, b):
    return jnp.dot(a, b, preferred_element_type=F32)


def _mm_nt(a, b):
    return lax.dot_general(a, b, (((1,), (1,)), ((), ())), preferred_element_type=F32)


def _split2(x):
    hi = x.astype(BF16)
    lo = (x - hi.astype(F32)).astype(BF16)
    return hi, lo


def _split3(x):
    hi = x.astype(BF16)
    r = x - hi.astype(F32)
    mid = r.astype(BF16)
    lo = (r - mid.astype(F32)).astype(BF16)
    return hi, mid, lo


def _mm3(a, b):
    ah, al = _split2(a)
    bh, bl = _split2(b)
    return _mm(ah, bh) + (_mm(ah, bl) + _mm(al, bh))


def _mm_exact_lhs(a, b):
    b1, b2, b3 = _split3(b)
    return _mm(a, b1) + (_mm(a, b2) + _mm(a, b3))


def _mm_nt_exact_lhs(a, b):
    b1, b2, b3 = _split3(b)
    return _mm_nt(a, b1) + (_mm_nt(a, b2) + _mm_nt(a, b3))


def _params(sem):
    return pltpu.CompilerParams(dimension_semantics=sem, vmem_limit_bytes=VMEM_LIMIT)


def _inproj_kernel(x_ref, nw_ref, w_ref, o_ref):
    x = x_ref[...]
    ms = jnp.mean(x * x, axis=-1, keepdims=True)
    xn = (x * lax.rsqrt(ms + EPS) * nw_ref[...]).astype(BF16)
    o_ref[...] = _mm(xn, w_ref[...])


def _inproj(x, norm_w, w_pad):
    m = x.shape[0]
    tm = min(m, 512)
    tn = 1024
    return pl.pallas_call(
        _inproj_kernel,
        out_shape=jax.ShapeDtypeStruct((m, P_COLS), F32),
        grid=(P_COLS // tn, m // tm),
        in_specs=[pl.BlockSpec((tm, D_MODEL), lambda j, i: (i, 0)),
                  pl.BlockSpec((1, D_MODEL), lambda j, i: (0, 0)),
                  pl.BlockSpec((D_MODEL, tn), lambda j, i: (0, j))],
        out_specs=pl.BlockSpec((tm, tn), lambda j, i: (i, j)),
        compiler_params=_params(("arbitrary", "arbitrary")),
        name="inproj",
    )(x, norm_w.reshape(1, D_MODEL), w_pad)


def _gelu(x):
    return 0.5 * x * (1.0 + lax.erf(x * (2.0 ** -0.5)))


def _mixa_kernel(u_ref, v_ref, lnw_ref, lnb_ref, ws_ref, bs_ref, ya_ref, *va_ref, lc, nsub):
    u = _gelu(u_ref[0])
    v = _gelu(v_ref[0])
    mu = jnp.mean(v, axis=-1, keepdims=True)
    vc = v - mu
    var = jnp.mean(vc * vc, axis=-1, keepdims=True)
    vn = vc * lax.rsqrt(var + LN_EPS) * lnw_ref[...] + lnb_ref[...]
    if va_ref:
        va_ref[0][0] = vn
    for s in range(nsub):
        rows = slice(s * lc, (s + 1) * lc)
        for g in range(A_GROUPS):
            cols = slice(g * A_HEAD, (g + 1) * A_HEAD)
            f = _mm(ws_ref[g], vn[rows, cols].astype(BF16)) + bs_ref[g]
            ya_ref[0, rows, cols] = (u[rows, cols] * f).astype(BF16)


def _mixer_a(proj, ln_w, ln_b, w_s, b_s, want_va):
    bsz, L, _ = proj.shape
    lc = min(L, A_CHUNK)
    rows = min(L, 512)
    nsub = rows // lc
    idx = jnp.arange(lc)
    allowed = (idx[:, None] // CHUNK) >= (idx[None, :] // CHUNK)
    ws = jnp.where(allowed[None], w_s[:, :lc, :lc], 0.0).astype(BF16)
    bs = jnp.broadcast_to(b_s[:, :lc, None], (A_GROUPS, lc, A_HEAD)).astype(F32)
    out_shape = [jax.ShapeDtypeStruct((bsz, L, A_WIDTH), BF16)]
    out_specs = [pl.BlockSpec((1, rows, A_WIDTH), lambda b, i: (b, i, 0))]
    if want_va:
        out_shape.append(jax.ShapeDtypeStruct((bsz, L, A_WIDTH), F32))
        out_specs.append(pl.BlockSpec((1, rows, A_WIDTH), lambda b, i: (b, i, 0)))
    res = pl.pallas_call(
        functools.partial(_mixa_kernel, lc=lc, nsub=nsub),
        out_shape=out_shape,
        grid=(bsz, L // rows),
        in_specs=[pl.BlockSpec((1, rows, A_WIDTH), lambda b, i: (b, i, P_AU // A_WIDTH)),
                  pl.BlockSpec((1, rows, A_WIDTH), lambda b, i: (b, i, P_AV // A_WIDTH)),
                  pl.BlockSpec((1, A_WIDTH), lambda b, i: (0, 0)),
                  pl.BlockSpec((1, A_WIDTH), lambda b, i: (0, 0)),
                  pl.BlockSpec((A_GROUPS, lc, lc), lambda b, i: (0, 0, 0)),
                  pl.BlockSpec((A_GROUPS, lc, A_HEAD), lambda b, i: (0, 0, 0))],
        out_specs=out_specs,
        compiler_params=_params(("arbitrary", "arbitrary")),
        name="mixer_a",
    )(proj, proj, ln_w.reshape(1, A_WIDTH), ln_b.reshape(1, A_WIDTH), ws, bs)
    return res if want_va else (res[0], None)


def _softplus(x):
    return jnp.maximum(x, 0.0) + jnp.log1p(jnp.exp(-jnp.abs(x)))


def _gdn_prep_kernel(x_ref, halo_ref, buf_ref, g1_ref, cw_ref, alog_ref, dtb_ref,
                     u_ref, w_ref, qg_ref, kgt_ref, qk_ref, eg_ref, xp_scr, *, rows, chunk):
    i = pl.program_id(1)
    nsub = rows // chunk
    x = x_ref[0]
    prev = jnp.where(i == 0, buf_ref[0], halo_ref[0])
    xp_scr[0:8, :] = prev
    xp_scr[8:8 + rows, :] = x
    y = x * cw_ref[B_CONV - 1:B_CONV, :]
    for k in range(1, B_CONV):
        y = y + xp_scr[8 - k:8 - k + rows, :] * cw_ref[B_CONV - 1 - k:B_CONV - k, :]
    y = y * jax.nn.sigmoid(y)

    g1 = g1_ref[0]
    beta_t = jax.nn.sigmoid(g1)
    g_t = -jnp.exp(alog_ref[...]) * _softplus(g1 + dtb_ref[...])

    ri = lax.broadcasted_iota(jnp.int32, (rows, rows), 0)
    ci = lax.broadcasted_iota(jnp.int32, (rows, rows), 1)
    same = (ri // chunk) == (ci // chunk)
    causal = same & (ri >= ci)
    strict = same & (ri > ci)
    tri = jnp.where(causal, 1.0, 0.0).astype(BF16)
    eye_r = jnp.where(ri == ci, 1.0, 0.0).astype(F32)
    li = lax.broadcasted_iota(jnp.int32, (LANE, LANE), 0)
    lj = lax.broadcasted_iota(jnp.int32, (LANE, LANE), 1)
    eye_l = jnp.where(li == lj, 1.0, 0.0).astype(BF16)

    gc = _mm_exact_lhs(tri, g_t)
    gct = _mm_nt_exact_lhs(eye_l, gc)
    glast = jnp.concatenate(
        [jnp.broadcast_to(gc[(s + 1) * chunk - 1:(s + 1) * chunk, :], (chunk, LANE)) for s in range(nsub)], axis=0)
    eg_full = jnp.exp(glast)

    nsq = int(math.log2(chunk)) - 1
    for h in range(B_HEADS):
        hs = slice(h * B_DK, (h + 1) * B_DK)
        bcol = beta_t[:, G1_BETA + h:G1_BETA + h + 1]
        gcol = gc[:, G1_ALPHA + h:G1_ALPHA + h + 1]
        grow = gct[G1_ALPHA + h:G1_ALPHA + h + 1, :]
        glcol = glast[:, G1_ALPHA + h:G1_ALPHA + h + 1]
        qh = y[:, hs]
        qh = qh * lax.rsqrt(jnp.sum(qh * qh, axis=-1, keepdims=True) + EPS) * (B_DK ** -0.5)
        kh = y[:, B_WIDTH + h * B_DK:B_WIDTH + (h + 1) * B_DK]
        kh = kh * lax.rsqrt(jnp.sum(kh * kh, axis=-1, keepdims=True) + EPS)
        vh = y[:, 2 * B_WIDTH + h * B_DV:2 * B_WIDTH + (h + 1) * B_DV]
        kb = kh * bcol
        khb = kh.astype(BF16)
        kk = _mm_nt(kb.astype(BF16), khb)
        qkm = _mm_nt(qh.astype(BF16), khb)
        dec = jnp.where(causal, jnp.exp(jnp.minimum(gcol - grow, 0.0)), 0.0)
        p = jnp.where(strict, -(kk * dec), 0.0)
        t = eye_r + p
        mpow = p
        for _ in range(nsq):
            mpow = _mm3(mpow, mpow)
            t = t + _mm3(t, mpow)
        egc = jnp.exp(gcol)
        rhs = jnp.concatenate([vh * bcol, kb * egc], axis=1)
        sol = _mm3(t, rhs)
        u_ref[0, :, hs] = sol[:, :B_DV]
        w_ref[0, :, hs] = sol[:, B_DV:].astype(BF16)
        qg_ref[0, :, hs] = (qh * egc).astype(BF16)
        kg = (kh * jnp.exp(glcol - gcol)).astype(BF16)
        kgt_ref[0, hs, :] = _mm_nt(eye_l, kg).astype(BF16)
        qk_ref[0, :, h * rows:(h + 1) * rows] = (qkm * dec).astype(BF16)
        for s in range(nsub):
            eg_ref[0, s, h:h + 1, :] = jnp.broadcast_to(
                eg_full[s * chunk:s * chunk + 1, G1_ALPHA + h:G1_ALPHA + h + 1], (1, LANE))


def _gdn_prep(proj, buf8, conv_w, a_log, dt_bias, rows, chunk):
    bsz, L, _ = proj.shape
    nblk = L // rows
    rpb = max(rows // 8, 1)
    alog = jnp.zeros((1, LANE), F32).at[0, G1_ALPHA:G1_ALPHA + B_HEADS].set(a_log)
    dtb = jnp.zeros((1, LANE), F32).at[0, G1_ALPHA:G1_ALPHA + B_HEADS].set(dt_bias)
    return pl.pallas_call(
        functools.partial(_gdn_prep_kernel, rows=rows, chunk=chunk),
        out_shape=[jax.ShapeDtypeStruct((bsz, L, B_WIDTH), F32),
                   jax.ShapeDtypeStruct((bsz, L, B_WIDTH), BF16),
                   jax.ShapeDtypeStruct((bsz, L, B_WIDTH), BF16),
                   jax.ShapeDtypeStruct((bsz, B_WIDTH, L), BF16),
                   jax.ShapeDtypeStruct((bsz, L, B_HEADS * rows), BF16),
                   jax.ShapeDtypeStruct((bsz, L // chunk, B_HEADS, LANE), F32)],
        grid=(bsz, nblk),
        in_specs=[pl.BlockSpec((1, rows, B_QKV), lambda b, i: (b, i, 0)),
                  pl.BlockSpec((1, 8, B_QKV), lambda b, i: (b, jnp.maximum(i * rpb - 1, 0), 0)),
                  pl.BlockSpec((1, 8, B_QKV), lambda b, i: (b, 0, 0)),
                  pl.BlockSpec((1, rows, LANE), lambda b, i: (b, i, P_G1 // LANE)),
                  pl.BlockSpec((B_CONV, B_QKV), lambda b, i: (0, 0)),
                  pl.BlockSpec((1, LANE), lambda b, i: (0, 0)),
                  pl.BlockSpec((1, LANE), lambda b, i: (0, 0))],
        out_specs=[pl.BlockSpec((1, rows, B_WIDTH), lambda b, i: (b, i, 0)),
                   pl.BlockSpec((1, rows, B_WIDTH), lambda b, i: (b, i, 0)),
                   pl.BlockSpec((1, rows, B_WIDTH), lambda b, i: (b, i, 0)),
                   pl.BlockSpec((1, B_WIDTH, rows), lambda b, i: (b, 0, i)),
                   pl.BlockSpec((1, rows, B_HEADS * rows), lambda b, i: (b, i, 0)),
                   pl.BlockSpec((1, rows // chunk, B_HEADS, LANE), lambda b, i: (b, i, 0, 0))],
        scratch_shapes=[pltpu.VMEM((rows + 8, B_QKV), F32)],
        compiler_params=_params(("arbitrary", "arbitrary")),
        name="gdn_prep",
    )(proj, proj, buf8, proj, conv_w, alog, dtb)


def _gdn_scan_kernel(u_ref, w_ref, qg_ref, kgt_ref, qk_ref, eg_ref, z_ref, nw_ref, s0_ref,
                     yb_ref, s_ref, *, bsz, rows, chunk):
    @pl.when(pl.program_id(0) == 0)
    def _():
        s_ref[...] = s0_ref[...]

    nsub = rows // chunk
    for b in range(bsz):
        for h in range(B_HEADS):
            hs = slice(h * B_DV, (h + 1) * B_DV)
            st = s_ref[b, h]
            for s in range(nsub):
                rs = slice(s * chunk, (s + 1) * chunk)
                sb = st.astype(BF16)
                wq = jnp.concatenate([w_ref[b, rs, hs], qg_ref[b, rs, hs]], axis=0)
                r = _mm(wq, sb)
                v_new = u_ref[b, rs, hs] - r[:chunk]
                vb = v_new.astype(BF16)
                o = r[chunk:] + _mm(qk_ref[b, rs, h * rows + s * chunk:h * rows + (s + 1) * chunk], vb)
                st = st * eg_ref[b, s, h:h + 1, :] + _mm(kgt_ref[b, hs, rs], vb)
                z = z_ref[b, rs, hs]
                on = o * lax.rsqrt(jnp.mean(o * o, axis=-1, keepdims=True) + EPS) * nw_ref[...]
                yb_ref[b, rs, hs] = (on * (z * jax.nn.sigmoid(z))).astype(BF16)
            s_ref[b, h] = st


def _gdn_scan(prep, proj, norm_w, s0, rows, chunk):
    u, w, qg, kgt, qk, eg = prep
    bsz, L, _ = u.shape
    nblk = L // rows
    return pl.pallas_call(
        functools.partial(_gdn_scan_kernel, bsz=bsz, rows=rows, chunk=chunk),
        out_shape=[jax.ShapeDtypeStruct((bsz, L, B_WIDTH), BF16),
                   jax.ShapeDtypeStruct((bsz, B_HEADS, B_DK, B_DV), F32)],
        grid=(nblk,),
        in_specs=[pl.BlockSpec((bsz, rows, B_WIDTH), lambda i: (0, i, 0)),
                  pl.BlockSpec((bsz, rows, B_WIDTH), lambda i: (0, i, 0)),
                  pl.BlockSpec((bsz, rows, B_WIDTH), lambda i: (0, i, 0)),
                  pl.BlockSpec((bsz, B_WIDTH, rows), lambda i: (0, 0, i)),
                  pl.BlockSpec((bsz, rows, B_HEADS * rows), lambda i: (0, i, 0)),
                  pl.BlockSpec((bsz, rows // chunk, B_HEADS, LANE), lambda i: (0, i, 0, 0)),
                  pl.BlockSpec((bsz, rows, B_WIDTH), lambda i: (0, i, P_BZ // B_WIDTH)),
                  pl.BlockSpec((1, B_DV), lambda i: (0, 0)),
                  pl.BlockSpec((bsz, B_HEADS, B_DK, B_DV), lambda i: (0, 0, 0, 0))],
        out_specs=[pl.BlockSpec((bsz, rows, B_WIDTH), lambda i: (0, i, 0)),
                   pl.BlockSpec((bsz, B_HEADS, B_DK, B_DV), lambda i: (0, 0, 0, 0))],
        compiler_params=_params(("arbitrary",)),
        name="gdn_scan",
    )(u, w, qg, kgt, qk, eg, proj, norm_w.reshape(1, B_DV), s0)


def _mla_proj_kernel(cq_ref, ckv_ref, g1_ref, g2_ref, cos8_ref, sin8_ref, qnw_ref, wqn_ref, wqr_ref, wqs_ref,
                     kvnw_ref, wukt_ref, ckv_out, kr_out, kcat_out, q_out):
    cq = cq_ref[0]
    cqn = (cq * lax.rsqrt(jnp.mean(cq * cq, axis=-1, keepdims=True) + EPS) * qnw_ref[...]).astype(BF16)
    c = ckv_ref[0]
    ckv = c * lax.rsqrt(jnp.mean(c * c, axis=-1, keepdims=True) + EPS) * kvnw_ref[...]
    ckv_out[0] = ckv
    cos64 = cos8_ref[:, 0:C_ROPE]
    sin64 = sin8_ref[:, 0:C_ROPE]
    kr = g1_ref[0][:, 0:C_ROPE] * cos64 + g2_ref[0][:, 0:C_ROPE] * sin64
    kr_out[0] = kr
    kcat_out[0, :, 0:KV_LORA] = ckv.astype(BF16)
    kcat_out[0, :, KV_LORA:KV_LORA + C_ROPE] = kr.astype(BF16)
    kcat_out[0, :, KV_LORA + C_ROPE:KCAT] = kr.astype(BF16)

    qn = _mm(cqn, wqn_ref[...])
    qr = (_mm(cqn, wqr_ref[...]) * cos8_ref[...] + _mm(cqn, wqs_ref[...]) * sin8_ref[...]) * Q_SCALE
    lane = lax.broadcasted_iota(jnp.int32, (1, LANE), 1)
    for h in range(C_HEADS):
        ql = _mm(qn[:, h * C_NOPE:(h + 1) * C_NOPE].astype(BF16), wukt_ref[h]) * Q_SCALE
        q_out[0, h, :, 0:KV_LORA] = ql.astype(BF16)
        pair = qr[:, (h // 2) * LANE:(h // 2 + 1) * LANE]
        keep = (lane // C_ROPE) == (h % 2)
        q_out[0, h, :, KV_LORA:KCAT] = jnp.where(keep, pair, 0.0).astype(BF16)


def _mla_proj(proj, cos8, sin8, q_norm_w, wqn, wqr, wqs, kv_norm_w, wukt):
    bsz, L, _ = proj.shape
    tm = min(L, 256)
    const2 = lambda b, i: (0, 0)
    return pl.pallas_call(
        _mla_proj_kernel,
        out_shape=[jax.ShapeDtypeStruct((bsz, L, KV_LORA), F32),
                   jax.ShapeDtypeStruct((bsz, L, C_ROPE), F32),
                   jax.ShapeDtypeStruct((bsz, L, KCAT), BF16),
                   jax.ShapeDtypeStruct((bsz, C_HEADS, L, KCAT), BF16)],
        grid=(bsz, L // tm),
        in_specs=[pl.BlockSpec((1, tm, Q_LORA), lambda b, i: (b, i, P_CQ // Q_LORA)),
                  pl.BlockSpec((1, tm, KV_LORA), lambda b, i: (b, i, P_CKV // KV_LORA)),
                  pl.BlockSpec((1, tm, LANE), lambda b, i: (b, i, P_G1 // LANE)),
                  pl.BlockSpec((1, tm, LANE), lambda b, i: (b, i, P_G2 // LANE)),
                  pl.BlockSpec((tm, C_HEADS * C_ROPE), lambda b, i: (i, 0)),
                  pl.BlockSpec((tm, C_HEADS * C_ROPE), lambda b, i: (i, 0)),
                  pl.BlockSpec((1, Q_LORA), const2),
                  pl.BlockSpec((Q_LORA, C_HEADS * C_NOPE), const2),
                  pl.BlockSpec((Q_LORA, C_HEADS * C_ROPE), const2),
                  pl.BlockSpec((Q_LORA, C_HEADS * C_ROPE), const2),
                  pl.BlockSpec((1, KV_LORA), const2),
                  pl.BlockSpec((C_HEADS, C_NOPE, KV_LORA), lambda b, i: (0, 0, 0))],
        out_specs=[pl.BlockSpec((1, tm, KV_LORA), lambda b, i: (b, i, 0)),
                   pl.BlockSpec((1, tm, C_ROPE), lambda b, i: (b, i, 0)),
                   pl.BlockSpec((1, tm, KCAT), lambda b, i: (b, i, 0)),
                   pl.BlockSpec((1, C_HEADS, tm, KCAT), lambda b, i: (b, 0, i, 0))],
        compiler_params=_params(("arbitrary", "arbitrary")),
        name="mla_proj",
    )(proj, proj, proj, proj, cos8, sin8, q_norm_w.reshape(1, Q_LORA), wqn, wqr, wqs,
      kv_norm_w.reshape(1, KV_LORA), wukt)


def _attn_finish(acc, l, wuv_ref, yc_ref, tq):
    o = acc * (1.0 / l)
    for h in range(C_HEADS):
        oh = o[h * tq:(h + 1) * tq].astype(BF16)
        yc_ref[0, :, h * C_VDIM:(h + 1) * C_VDIM] = _mm(oh, wuv_ref[h]).astype(BF16)


def _attn_prompt_kernel(q_ref, k_ref, wuv_ref, yc_ref, m_scr, l_scr, acc_scr, *, tq, tk):
    qi = pl.program_id(1)
    rows = C_HEADS * tq
    q = q_ref[0].reshape(rows, KCAT)
    m_scr[...] = jnp.full((rows, LANE), -1e30, F32)
    l_scr[...] = jnp.zeros((rows, LANE), F32)
    acc_scr[...] = jnp.zeros((rows, KV_LORA), F32)

    def block(kj, masked):
        start = pl.multiple_of(kj * tk, tk)
        k = k_ref[0, pl.ds(start, tk), :]
        s = _mm_nt(q, k)
        if masked:
            ri = lax.broadcasted_iota(jnp.int32, (rows, tk), 0)
            ci = lax.broadcasted_iota(jnp.int32, (rows, tk), 1)
            ok = ((qi * tq + ri % tq) // CHUNK) >= ((start + ci) // CHUNK)
            s = jnp.where(ok, s, -jnp.inf)
        m_prev = m_scr[...]
        m_new = jnp.maximum(m_prev, jnp.max(s, axis=-1, keepdims=True))
        alpha = jnp.exp2(m_prev - m_new)
        p = jnp.exp2(s - pltpu.repeat(m_new, tk // LANE, axis=1))
        l_scr[...] = alpha * l_scr[...] + jnp.sum(p, axis=-1, keepdims=True)
        acc_scr[...] = (acc_scr[...] * pltpu.repeat(alpha, KV_LORA // LANE, axis=1)
                        + _mm(p.astype(BF16), k[:, 0:KV_LORA]))
        m_scr[...] = m_new

    def body(kj, carry):
        block(kj, False)
        return carry

    nfull = (qi * tq) // tk
    lax.fori_loop(0, nfull, body, 0)
    block(nfull, True)
    _attn_finish(acc_scr[...], pltpu.repeat(l_scr[...], KV_LORA // LANE, axis=1), wuv_ref, yc_ref, tq)


def _attn_prompt(q, kcat, wuv):
    bsz, _, L, _ = q.shape
    tq = 128
    tk = min(L, 512)
    rows = C_HEADS * tq
    return pl.pallas_call(
        functools.partial(_attn_prompt_kernel, tq=tq, tk=tk),
        out_shape=jax.ShapeDtypeStruct((bsz, L, C_WIDTH), BF16),
        grid=(bsz, L // tq),
        in_specs=[pl.BlockSpec((1, C_HEADS, tq, KCAT), lambda b, i: (b, 0, i, 0)),
                  pl.BlockSpec((1, L, KCAT), lambda b, i: (b, 0, 0)),
                  pl.BlockSpec((C_HEADS, KV_LORA, C_VDIM), lambda b, i: (0, 0, 0))],
        out_specs=pl.BlockSpec((1, tq, C_WIDTH), lambda b, i: (b, i, 0)),
        scratch_shapes=[pltpu.VMEM((rows, LANE), F32), pltpu.VMEM((rows, LANE), F32),
                        pltpu.VMEM((rows, KV_LORA), F32)],
        compiler_params=_params(("arbitrary", "arbitrary")),
        name="attn_prompt",
    )(q, kcat, wuv)


def _attn_full_kernel(q_ref, k_ref, wuv_ref, yc_ref, *, tq, nkeys):
    rows = C_HEADS * tq
    q = q_ref[0].reshape(rows, KCAT)
    k = k_ref[0]
    s = _mm_nt(q, k)
    ci = lax.broadcasted_iota(jnp.int32, s.shape, 1)
    s = jnp.where(ci < nkeys, s, -jnp.inf)
    m = jnp.max(s, axis=-1, keepdims=True)
    p = jnp.exp2(s - m)
    l = jnp.sum(p, axis=-1, keepdims=True)
    acc = _mm(p.astype(BF16), k[:, 0:KV_LORA])
    _attn_finish(acc, l, wuv_ref, yc_ref, tq)


def _attn_full(q, kcat, wuv, nkeys):
    bsz, _, tq, _ = q.shape
    npad = kcat.shape[1]
    return pl.pallas_call(
        functools.partial(_attn_full_kernel, tq=tq, nkeys=nkeys),
        out_shape=jax.ShapeDtypeStruct((bsz, tq, C_WIDTH), BF16),
        grid=(bsz,),
        in_specs=[pl.BlockSpec((1, C_HEADS, tq, KCAT), lambda b: (b, 0, 0, 0)),
                  pl.BlockSpec((1, npad, KCAT), lambda b: (b, 0, 0)),
                  pl.BlockSpec((C_HEADS, KV_LORA, C_VDIM), lambda b: (0, 0, 0))],
        out_specs=pl.BlockSpec((1, tq, C_WIDTH), lambda b: (b, 0, 0)),
        compiler_params=_params(("arbitrary",)),
        name="attn_full",
    )(q, kcat, wuv)


def _outproj_kernel(x_ref, ya_ref, yb_ref, yc_ref, wo_ref, o_ref):
    tn = 512
    for j in range(0, D_MODEL, tn):
        acc = _mm(ya_ref[...], wo_ref[0:A_WIDTH, j:j + tn])
        acc = acc + _mm(yb_ref[...], wo_ref[A_WIDTH:A_WIDTH + B_WIDTH, j:j + tn])
        acc = acc + _mm(yc_ref[...], wo_ref[A_WIDTH + B_WIDTH:D_MODEL, j:j + tn])
        o_ref[:, j:j + tn] = x_ref[:, j:j + tn] + acc


def _outproj(x, ya, yb, yc, wo):
    m = x.shape[0]
    tm = min(m, 256)
    row = lambda i: (i, 0)
    return pl.pallas_call(
        _outproj_kernel,
        out_shape=jax.ShapeDtypeStruct((m, D_MODEL), F32),
        grid=(m // tm,),
        in_specs=[pl.BlockSpec((tm, D_MODEL), row),
                  pl.BlockSpec((tm, A_WIDTH), row),
                  pl.BlockSpec((tm, B_WIDTH), row),
                  pl.BlockSpec((tm, C_WIDTH), row),
                  pl.BlockSpec((D_MODEL, D_MODEL), lambda i: (0, 0))],
        out_specs=pl.BlockSpec((tm, D_MODEL), row),
        compiler_params=_params(("arbitrary",)),
        name="outproj",
    )(x, ya, yb, yc, wo)


def _ffn_kernel(x_ref, nw_ref, wu_ref, wd_ref, fw_ref, o_ref, xn_scr, acc_scr, *, final_norm):
    f = pl.program_id(1)

    @pl.when(f == 0)
    def _():
        x = x_ref[...]
        ms = jnp.mean(x * x, axis=-1, keepdims=True)
        xn_scr[...] = (x * lax.rsqrt(ms + EPS) * nw_ref[...]).astype(BF16)
        acc_scr[...] = jnp.zeros_like(acc_scr)

    hid = jnp.maximum(_mm(xn_scr[...], wu_ref[...]), 0.0)
    acc_scr[...] += _mm((hid * hid).astype(BF16), wd_ref[...])

    @pl.when(f == pl.num_programs(1) - 1)
    def _():
        y = x_ref[...] + acc_scr[...]
        if final_norm:
            ms = jnp.mean(y * y, axis=-1, keepdims=True)
            y = y * lax.rsqrt(ms + EPS) * fw_ref[...]
        o_ref[...] = y


def _ffn(x, norm_w, wu, wd, final_w, final_norm):
    m = x.shape[0]
    tm = min(m, 512)
    tf = 512
    return pl.pallas_call(
        functools.partial(_ffn_kernel, final_norm=final_norm),
        out_shape=jax.ShapeDtypeStruct((m, D_MODEL), F32),
        grid=(m // tm, D_FF // tf),
        in_specs=[pl.BlockSpec((tm, D_MODEL), lambda i, f: (i, 0)),
                  pl.BlockSpec((1, D_MODEL), lambda i, f: (0, 0)),
                  pl.BlockSpec((D_MODEL, tf), lambda i, f: (0, f)),
                  pl.BlockSpec((tf, D_MODEL), lambda i, f: (f, 0)),
                  pl.BlockSpec((1, D_MODEL), lambda i, f: (0, 0))],
        out_specs=pl.BlockSpec((tm, D_MODEL), lambda i, f: (i, 0)),
        scratch_shapes=[pltpu.VMEM((tm, D_MODEL), BF16), pltpu.VMEM((tm, D_MODEL), F32)],
        compiler_params=_params(("arbitrary", "arbitrary")),
        name="ffn",
    )(x, norm_w.reshape(1, D_MODEL), wu, wd, final_w.reshape(1, D_MODEL))


def _swap_halves(w):
    half = C_ROPE // 2
    return jnp.concatenate([w[..., half:], w[..., :half]], axis=-1)


def _prep_w_in(w_in):
    k = w_in.shape[0]
    ckr = w_in[:, OFF_CKR:IN_COLS]
    g1 = jnp.concatenate([ckr, w_in[:, OFF_BBETA:OFF_CQ], jnp.zeros((k, LANE - C_ROPE - 2 * B_HEADS), F32)], axis=1)
    g2 = jnp.concatenate([_swap_halves(ckr), jnp.zeros((k, LANE - C_ROPE), F32)], axis=1)
    parts = [w_in[:, OFF_BQKV:OFF_BZ], w_in[:, OFF_AU:OFF_AV], w_in[:, OFF_AV:OFF_BQKV], w_in[:, OFF_BZ:OFF_BBETA],
             w_in[:, OFF_CQ:OFF_CKV], w_in[:, OFF_CKV:OFF_CKR], g1, g2]
    return jnp.concatenate(parts, axis=1).astype(BF16)


def _rope_tables(pos):
    half = C_ROPE // 2
    inv = ROPE_THETA ** (-jnp.arange(half, dtype=F32) / half)
    ang = pos.astype(F32)[:, None] * inv[None, :]
    cos, sin = jnp.cos(ang), jnp.sin(ang)
    cos8 = jnp.tile(jnp.concatenate([cos, cos], axis=1), (1, C_HEADS))
    sin8 = jnp.tile(jnp.concatenate([-sin, sin], axis=1), (1, C_HEADS))
    return cos8, sin8


def _layer(x, bsz, L, tabs, lw, past, final_w, final_norm):
    proj = _inproj(x, lw['attn_norm_w'], lw['w_in']).reshape(bsz, L, P_COLS)

    ya, va = _mixer_a(proj, lw['a_ln_w'], lw['a_ln_b'], lw['a_w_s'], lw['a_b_s'], past is not None)

    if past is None:
        rows, chunk = 2 * CHUNK, CHUNK
        buf8 = jnp.zeros((bsz, 8, B_QKV), F32)
        s0 = jnp.zeros((bsz, B_HEADS, B_DK, B_DV), F32)
    else:
        rows, chunk = L, L
        ckv_past, kr_past, s0, conv_buf = past
        buf8 = jnp.concatenate([jnp.zeros((bsz, 8 - (B_CONV - 1), B_QKV), F32), conv_buf], axis=1)
    prep = _gdn_prep(proj, buf8, lw['b_conv_w'], lw['b_a_log'], lw['b_dt_bias'], rows, chunk)
    yb, s_new = _gdn_scan(prep, proj, lw['b_norm_w'], s0, rows, chunk)
    buf_new = proj[:, L - (B_CONV - 1):, P_BQKV:P_BQKV + B_QKV]

    cos8, sin8 = tabs
    ckv, kr, kcat, q = _mla_proj(proj, cos8, sin8, lw['c_q_norm_w'], lw['wqn'], lw['wqr'], lw['wqs'],
                                 lw['c_kv_norm_w'], lw['wukt'])
    if past is None:
        yc = _attn_prompt(q, kcat, lw['wuv'])
    else:
        past_len = ckv_past.shape[1]
        nkeys = past_len + L
        npad = -(-nkeys // LANE) * LANE
        kpast = jnp.concatenate([ckv_past, kr_past, kr_past], axis=-1).astype(BF16)
        kall = jnp.concatenate([kpast, kcat, jnp.zeros((bsz, npad - nkeys, KCAT), BF16)], axis=1)
        yc = _attn_full(q, kall, lw['wuv'], nkeys)

    m = bsz * L
    x = _outproj(x, ya.reshape(m, A_WIDTH), yb.reshape(m, B_WIDTH), yc.reshape(m, C_WIDTH), lw['w_out'])
    x = _ffn(x, lw['mlp_norm_w'], lw['w_up'], lw['w_down'], final_w, final_norm)
    return x, (ckv, kr, s_new, buf_new, va)


def kernel(x_prompt, x_sample, cache_mla_ckv, cache_mla_krope, state_gdn, state_gdn_conv,
           attn_norm_w, w_in, a_ln_w, a_ln_b, a_w_s, a_b_s, b_conv_w, b_a_log, b_dt_bias, b_norm_w,
           c_q_norm_w, c_w_uq, c_kv_norm_w, c_w_uk, c_w_uv, w_out, mlp_norm_w, w_up, w_down, final_norm_w):
    depth = w_in.shape[0]
    bp, lp, _ = x_prompt.shape
    bs, ls, _ = x_sample.shape
    past_len = cache_mla_ckv.shape[2]
    tabs_p = _rope_tables(jnp.arange(lp, dtype=jnp.int32))
    tabs_s = _rope_tables(past_len + jnp.arange(ls, dtype=jnp.int32))
    hp = x_prompt.reshape(bp * lp, D_MODEL)
    hs = x_sample.reshape(bs * ls, D_MODEL)
    sp_list, ss_list = [], []
    for l in range(depth):
        wq_rope = c_w_uq[l][:, :, C_NOPE:]
        lw = {
            'attn_norm_w': attn_norm_w[l], 'w_in': _prep_w_in(w_in[l]),
            'a_ln_w': a_ln_w[l], 'a_ln_b': a_ln_b[l], 'a_w_s': a_w_s[l], 'a_b_s': a_b_s[l],
            'b_conv_w': b_conv_w[l], 'b_a_log': b_a_log[l], 'b_dt_bias': b_dt_bias[l], 'b_norm_w': b_norm_w[l],
            'c_q_norm_w': c_q_norm_w[l], 'c_kv_norm_w': c_kv_norm_w[l],
            'wqn': c_w_uq[l][:, :, :C_NOPE].reshape(Q_LORA, C_HEADS * C_NOPE).astype(BF16),
            'wqr': wq_rope.reshape(Q_LORA, C_HEADS * C_ROPE).astype(BF16),
            'wqs': _swap_halves(wq_rope).reshape(Q_LORA, C_HEADS * C_ROPE).astype(BF16),
            'wukt': jnp.transpose(c_w_uk[l], (1, 2, 0)).astype(BF16),
            'wuv': jnp.transpose(c_w_uv[l], (1, 0, 2)).astype(BF16),
            'w_out': w_out[l].astype(BF16), 'mlp_norm_w': mlp_norm_w[l],
            'w_up': w_up[l].astype(BF16), 'w_down': w_down[l].astype(BF16),
        }
        last = l == depth - 1
        hp, sp = _layer(hp, bp, lp, tabs_p, lw, None, final_norm_w, last)
        hs, ss = _layer(hs, bs, ls, tabs_s, lw,
                        (cache_mla_ckv[l], cache_mla_krope[l], state_gdn[l], state_gdn_conv[l]), final_norm_w, last)
        sp_list.append(sp)
        ss_list.append(ss)

    def stack(lst, i):
        return jnp.stack([s[i] for s in lst], axis=0)

    return (hp.reshape(bp, lp, D_MODEL), hs.reshape(bs, ls, D_MODEL),
            stack(sp_list, 0), stack(sp_list, 1), stack(sp_list, 2), stack(sp_list, 3),
            stack(ss_list, 0), stack(ss_list, 1), stack(ss_list, 2), stack(ss_list, 3), stack(ss_list, 4))
```

```python
import functools
import math

import jax
import jax.numpy as jnp
from jax import lax
from jax.experimental import pallas as pl
from jax.experimental.pallas import tpu as pltpu

F32 = jnp.float32
BF16 = jnp.bfloat16

D_MODEL = 2048
CHUNK = 64
EPS = 1e-6
LN_EPS = 1e-5
A_GROUPS = 4
A_HEAD = 128
A_WIDTH = 512
A_CHUNK = 128
B_HEADS = 4
B_DK = 128
B_DV = 128
B_WIDTH = 512
B_QKV = 1536
B_CONV = 4
C_HEADS = 8
C_NOPE = 128
C_ROPE = 64
C_VDIM = 128
C_WIDTH = 1024
Q_LORA = 512
KV_LORA = 256
ROPE_THETA = 10000.0
C_SCALE = (C_NOPE + C_ROPE) ** -0.5
Q_SCALE = C_SCALE * math.log2(math.e)
D_FF = 4 * D_MODEL

OFF_AU = 0
OFF_AV = OFF_AU + A_WIDTH
OFF_BQKV = OFF_AV + A_WIDTH
OFF_BZ = OFF_BQKV + B_QKV
OFF_BBETA = OFF_BZ + B_WIDTH
OFF_BALPHA = OFF_BBETA + B_HEADS
OFF_CQ = OFF_BALPHA + B_HEADS
OFF_CKV = OFF_CQ + Q_LORA
OFF_CKR = OFF_CKV + KV_LORA
IN_COLS = OFF_CKR + C_ROPE

P_BQKV = 0
P_AU = 1536
P_AV = 2048
P_BZ = 2560
P_CQ = 3072
P_CKV = 3584
P_G1 = 3840
P_G2 = 3968
P_COLS = 4096
LANE = 128
G1_BETA = C_ROPE
G1_ALPHA = C_ROPE + B_HEADS
KCAT = KV_LORA + 2 * C_ROPE

VMEM_LIMIT = 56 * 1024 * 1024


def _mm(a, b):
    return jnp.dot(a, b, preferred_element_type=F32)


def _mm_nt(a, b):
    return lax.dot_general(a, b, (((1,), (1,)), ((), ())), preferred_element_type=F32)


def _split2(x):
    hi = x.astype(BF16)
    lo = (x - hi.astype(F32)).astype(BF16)
    return hi, lo


def _split3(x):
    hi = x.astype(BF16)
    r = x - hi.astype(F32)
    mid = r.astype(BF16)
    lo = (r - mid.astype(F32)).astype(BF16)
    return hi, mid, lo


def _cat3_lhs(a):
    ah, al = _split2(a)
    return jnp.concatenate([ah, ah, al], axis=1)


def _cat3_rhs(b):
    bh, bl = _split2(b)
    return jnp.concatenate([bh, bl, bh], axis=0)


def _mm3(a, b):
    return _mm(_cat3_lhs(a), _cat3_rhs(b))


def _mm_exact_lhs(a, b):
    return _mm(jnp.concatenate([a, a, a], axis=1), jnp.concatenate(_split3(b), axis=0))


def _mm_nt_exact_lhs(a, b):
    return _mm_nt(jnp.concatenate([a, a, a], axis=1), jnp.concatenate(_split3(b), axis=1))


def _lane_tile(x, n):
    return x if n == 1 else jnp.concatenate([x] * n, axis=1)


def _params(sem):
    return pltpu.CompilerParams(dimension_semantics=sem, vmem_limit_bytes=VMEM_LIMIT)


def _inproj_kernel(x_ref, nw_ref, w_ref, o_ref):
    x = x_ref[...]
    ms = jnp.mean(x * x, axis=-1, keepdims=True)
    xn = (x * lax.rsqrt(ms + EPS) * nw_ref[...]).astype(BF16)
    o_ref[...] = _mm(xn, w_ref[...])


def _inproj(x, norm_w, w_pad):
    m = x.shape[0]
    tm = min(m, 512)
    tn = 1024
    return pl.pallas_call(
        _inproj_kernel,
        out_shape=jax.ShapeDtypeStruct((m, P_COLS), F32),
        grid=(P_COLS // tn, m // tm),
        in_specs=[pl.BlockSpec((tm, D_MODEL), lambda j, i: (i, 0)),
                  pl.BlockSpec((1, D_MODEL), lambda j, i: (0, 0)),
                  pl.BlockSpec((D_MODEL, tn), lambda j, i: (0, j))],
        out_specs=pl.BlockSpec((tm, tn), lambda j, i: (i, j)),
        compiler_params=_params(("arbitrary", "arbitrary")),
        name="inproj",
    )(x, norm_w.reshape(1, D_MODEL), w_pad)


def _gelu(x):
    return 0.5 * x * (1.0 + lax.erf(x * (2.0 ** -0.5)))


def _mixa_kernel(u_ref, v_ref, lnw_ref, lnb_ref, ws_ref, bs_ref, ya_ref, *va_ref, lc, nsub):
    u = _gelu(u_ref[0])
    v = _gelu(v_ref[0])
    mu = jnp.mean(v, axis=-1, keepdims=True)
    vc = v - mu
    var = jnp.mean(vc * vc, axis=-1, keepdims=True)
    vn = vc * lax.rsqrt(var + LN_EPS) * lnw_ref[...] + lnb_ref[...]
    if va_ref:
        va_ref[0][0] = vn
    for s in range(nsub):
        rows = slice(s * lc, (s + 1) * lc)
        for g in range(A_GROUPS):
            cols = slice(g * A_HEAD, (g + 1) * A_HEAD)
            f = _mm(ws_ref[g], vn[rows, cols].astype(BF16)) + bs_ref[g]
            ya_ref[0, rows, cols] = (u[rows, cols] * f).astype(BF16)


def _mixer_a(proj, ln_w, ln_b, w_s, b_s, want_va):
    bsz, L, _ = proj.shape
    lc = min(L, A_CHUNK)
    rows = min(L, 512)
    nsub = rows // lc
    idx = jnp.arange(lc)
    allowed = (idx[:, None] // CHUNK) >= (idx[None, :] // CHUNK)
    ws = jnp.where(allowed[None], w_s[:, :lc, :lc], 0.0).astype(BF16)
    bs = jnp.broadcast_to(b_s[:, :lc, None], (A_GROUPS, lc, A_HEAD)).astype(F32)
    out_shape = [jax.ShapeDtypeStruct((bsz, L, A_WIDTH), BF16)]
    out_specs = [pl.BlockSpec((1, rows, A_WIDTH), lambda b, i: (b, i, 0))]
    if want_va:
        out_shape.append(jax.ShapeDtypeStruct((bsz, L, A_WIDTH), F32))
        out_specs.append(pl.BlockSpec((1, rows, A_WIDTH), lambda b, i: (b, i, 0)))
    res = pl.pallas_call(
        functools.partial(_mixa_kernel, lc=lc, nsub=nsub),
        out_shape=out_shape,
        grid=(bsz, L // rows),
        in_specs=[pl.BlockSpec((1, rows, A_WIDTH), lambda b, i: (b, i, P_AU // A_WIDTH)),
                  pl.BlockSpec((1, rows, A_WIDTH), lambda b, i: (b, i, P_AV // A_WIDTH)),
                  pl.BlockSpec((1, A_WIDTH), lambda b, i: (0, 0)),
                  pl.BlockSpec((1, A_WIDTH), lambda b, i: (0, 0)),
                  pl.BlockSpec((A_GROUPS, lc, lc), lambda b, i: (0, 0, 0)),
                  pl.BlockSpec((A_GROUPS, lc, A_HEAD), lambda b, i: (0, 0, 0))],
        out_specs=out_specs,
        compiler_params=_params(("arbitrary", "arbitrary")),
        name="mixer_a",
    )(proj, proj, ln_w.reshape(1, A_WIDTH), ln_b.reshape(1, A_WIDTH), ws, bs)
    return res if want_va else (res[0], None)


def _softplus(x):
    return jnp.maximum(x, 0.0) + jnp.log1p(jnp.exp(-jnp.abs(x)))


def _gdn_prep_kernel(x_ref, halo_ref, buf_ref, g1_ref, cw_ref, alog_ref, dtb_ref,
                     u_ref, w_ref, qg_ref, kgt_ref, qk_ref, eg_ref, xp_scr, *, nb, rows, chunk):
    i = pl.program_id(1)
    nsub = rows // chunk
    ri = lax.broadcasted_iota(jnp.int32, (rows, rows), 0)
    ci = lax.broadcasted_iota(jnp.int32, (rows, rows), 1)
    same = (ri // chunk) == (ci // chunk)
    causal = same & (ri >= ci)
    strict = same & (ri > ci)
    tri = jnp.where(causal, 1.0, 0.0).astype(BF16)
    eye_r = jnp.where(ri == ci, 1.0, 0.0).astype(F32)
    li = lax.broadcasted_iota(jnp.int32, (LANE, LANE), 0)
    lj = lax.broadcasted_iota(jnp.int32, (LANE, LANE), 1)
    eye_l = jnp.where(li == lj, 1.0, 0.0).astype(BF16)

    ys, betas, gts = [], [], []
    for bb in range(nb):
        x = x_ref[bb]
        xp_scr[bb, 0:8, :] = jnp.where(i == 0, buf_ref[bb], halo_ref[bb])
        xp_scr[bb, 8:8 + rows, :] = x
        y = x * cw_ref[B_CONV - 1:B_CONV, :]
        for k in range(1, B_CONV):
            y = y + xp_scr[bb, 8 - k:8 - k + rows, :] * cw_ref[B_CONV - 1 - k:B_CONV - k, :]
        ys.append(y * jax.nn.sigmoid(y))
        g1 = g1_ref[bb]
        betas.append(jax.nn.sigmoid(g1))
        gts.append(-jnp.exp(alog_ref[...]) * _softplus(g1 + dtb_ref[...]))

    gcs = [_mm_exact_lhs(tri, g) for g in gts]
    gcts = [_mm_nt_exact_lhs(eye_l, gc) for gc in gcs]
    glasts = [jnp.concatenate(
        [jnp.broadcast_to(gc[(s + 1) * chunk - 1:(s + 1) * chunk, :], (chunk, LANE)) for s in range(nsub)], axis=0)
        for gc in gcs]

    pairs = [(bb, h) for bb in range(nb) for h in range(B_HEADS)]
    qs, ks, kbs, rhss, decs, egcs = [], [], [], [], [], []
    for bb, h in pairs:
        y, gc = ys[bb], gcs[bb]
        bcol = betas[bb][:, G1_BETA + h:G1_BETA + h + 1]
        gcol = gc[:, G1_ALPHA + h:G1_ALPHA + h + 1]
        grow = gcts[bb][G1_ALPHA + h:G1_ALPHA + h + 1, :]
        qh = y[:, h * B_DK:(h + 1) * B_DK]
        qh = qh * lax.rsqrt(jnp.sum(qh * qh, axis=-1, keepdims=True) + EPS) * (B_DK ** -0.5)
        kh = y[:, B_WIDTH + h * B_DK:B_WIDTH + (h + 1) * B_DK]
        kh = kh * lax.rsqrt(jnp.sum(kh * kh, axis=-1, keepdims=True) + EPS)
        vh = y[:, 2 * B_WIDTH + h * B_DV:2 * B_WIDTH + (h + 1) * B_DV]
        kb = kh * bcol
        egc = jnp.exp(gcol)
        qs.append(qh)
        ks.append(kh)
        kbs.append(kb)
        egcs.append(egc)
        rhss.append(jnp.concatenate([vh * bcol, kb * egc], axis=1))
        decs.append(jnp.where(causal, jnp.exp(jnp.minimum(gcol - grow, 0.0)), 0.0))

    khb = [k.astype(BF16) for k in ks]
    kk = [_mm_nt(kb.astype(BF16), kx) for kb, kx in zip(kbs, khb)]
    qkm = [_mm_nt(q.astype(BF16), kx) for q, kx in zip(qs, khb)]

    ps = [jnp.where(strict, -(a * d), 0.0) for a, d in zip(kk, decs)]
    ts = [eye_r + p for p in ps]
    ms = [_mm(_cat3_lhs(p), _cat3_rhs(p)) for p in ps]
    nsq = int(math.log2(chunk)) - 1
    for step in range(nsq):
        mr = [_cat3_rhs(m) for m in ms]
        upd = [_mm(_cat3_lhs(t), b) for t, b in zip(ts, mr)]
        if step + 1 < nsq:
            ms = [_mm(_cat3_lhs(m), b) for m, b in zip(ms, mr)]
        ts = [t + d for t, d in zip(ts, upd)]
    sols = [_mm3(t, r) for t, r in zip(ts, rhss)]

    for n, (bb, h) in enumerate(pairs):
        hs = slice(h * B_DK, (h + 1) * B_DK)
        gcol = gcs[bb][:, G1_ALPHA + h:G1_ALPHA + h + 1]
        glcol = glasts[bb][:, G1_ALPHA + h:G1_ALPHA + h + 1]
        u_ref[bb, :, hs] = sols[n][:, :B_DV]
        w_ref[bb, :, hs] = sols[n][:, B_DV:].astype(BF16)
        qg_ref[bb, :, hs] = (qs[n] * egcs[n]).astype(BF16)
        kg = (ks[n] * jnp.exp(glcol - gcol)).astype(BF16)
        kgt_ref[bb, hs, :] = _mm_nt(eye_l, kg).astype(BF16)
        qk_ref[bb, :, h * rows:(h + 1) * rows] = (qkm[n] * decs[n]).astype(BF16)
        for s in range(nsub):
            eg_ref[bb, s, h:h + 1, :] = jnp.broadcast_to(
                jnp.exp(glasts[bb][s * chunk:s * chunk + 1, G1_ALPHA + h:G1_ALPHA + h + 1]), (1, LANE))


def _gdn_prep(proj, buf8, conv_w, a_log, dt_bias, rows, chunk):
    bsz, L, _ = proj.shape
    nb = 2
    nblk = L // rows
    rpb = max(rows // 8, 1)
    alog = jnp.zeros((1, LANE), F32).at[0, G1_ALPHA:G1_ALPHA + B_HEADS].set(a_log)
    dtb = jnp.zeros((1, LANE), F32).at[0, G1_ALPHA:G1_ALPHA + B_HEADS].set(dt_bias)
    return pl.pallas_call(
        functools.partial(_gdn_prep_kernel, nb=nb, rows=rows, chunk=chunk),
        out_shape=[jax.ShapeDtypeStruct((bsz, L, B_WIDTH), F32),
                   jax.ShapeDtypeStruct((bsz, L, B_WIDTH), BF16),
                   jax.ShapeDtypeStruct((bsz, L, B_WIDTH), BF16),
                   jax.ShapeDtypeStruct((bsz, B_WIDTH, L), BF16),
                   jax.ShapeDtypeStruct((bsz, L, B_HEADS * rows), BF16),
                   jax.ShapeDtypeStruct((bsz, L // chunk, B_HEADS, LANE), F32)],
        grid=(bsz // nb, nblk),
        in_specs=[pl.BlockSpec((nb, rows, B_QKV), lambda b, i: (b, i, 0)),
                  pl.BlockSpec((nb, 8, B_QKV), lambda b, i: (b, jnp.maximum(i * rpb - 1, 0), 0)),
                  pl.BlockSpec((nb, 8, B_QKV), lambda b, i: (b, 0, 0)),
                  pl.BlockSpec((nb, rows, LANE), lambda b, i: (b, i, P_G1 // LANE)),
                  pl.BlockSpec((B_CONV, B_QKV), lambda b, i: (0, 0)),
                  pl.BlockSpec((1, LANE), lambda b, i: (0, 0)),
                  pl.BlockSpec((1, LANE), lambda b, i: (0, 0))],
        out_specs=[pl.BlockSpec((nb, rows, B_WIDTH), lambda b, i: (b, i, 0)),
                   pl.BlockSpec((nb, rows, B_WIDTH), lambda b, i: (b, i, 0)),
                   pl.BlockSpec((nb, rows, B_WIDTH), lambda b, i: (b, i, 0)),
                   pl.BlockSpec((nb, B_WIDTH, rows), lambda b, i: (b, 0, i)),
                   pl.BlockSpec((nb, rows, B_HEADS * rows), lambda b, i: (b, i, 0)),
                   pl.BlockSpec((nb, rows // chunk, B_HEADS, LANE), lambda b, i: (b, i, 0, 0))],
        scratch_shapes=[pltpu.VMEM((nb, rows + 8, B_QKV), F32)],
        compiler_params=_params(("arbitrary", "arbitrary")),
        name="gdn_prep",
    )(proj, proj, buf8, proj, conv_w, alog, dtb)


def _gdn_scan_kernel(u_ref, w_ref, qg_ref, kgt_ref, qk_ref, eg_ref, z_ref, nw_ref, s0_ref,
                     yb_ref, s_ref, *, bsz, rows, chunk):
    @pl.when(pl.program_id(0) == 0)
    def _():
        s_ref[...] = s0_ref[...]

    nsub = rows // chunk
    pairs = [(b, h) for b in range(bsz) for h in range(B_HEADS)]
    st = [s_ref[b, h] for b, h in pairs]
    for s in range(nsub):
        rs = slice(s * chunk, (s + 1) * chunk)
        r1 = [_mm(jnp.concatenate([w_ref[b, rs, h * B_DV:(h + 1) * B_DV], qg_ref[b, rs, h * B_DV:(h + 1) * B_DV]],
                                  axis=0), x.astype(BF16))
              for (b, h), x in zip(pairs, st)]
        vb = [(u_ref[b, rs, h * B_DV:(h + 1) * B_DV] - r[:chunk]).astype(BF16) for (b, h), r in zip(pairs, r1)]
        r2 = [_mm(jnp.concatenate([qk_ref[b, rs, h * rows + s * chunk:h * rows + (s + 1) * chunk],
                                   kgt_ref[b, h * B_DK:(h + 1) * B_DK, rs]], axis=0), v)
              for (b, h), v in zip(pairs, vb)]
        for n, (b, h) in enumerate(pairs):
            hs = slice(h * B_DV, (h + 1) * B_DV)
            o = r1[n][chunk:] + r2[n][:chunk]
            st[n] = st[n] * eg_ref[b, s, h:h + 1, :] + r2[n][chunk:]
            z = z_ref[b, rs, hs]
            on = o * lax.rsqrt(jnp.mean(o * o, axis=-1, keepdims=True) + EPS) * nw_ref[...]
            yb_ref[b, rs, hs] = (on * (z * jax.nn.sigmoid(z))).astype(BF16)
    for n, (b, h) in enumerate(pairs):
        s_ref[b, h] = st[n]


def _gdn_scan(prep, proj, norm_w, s0, rows, chunk):
    u, w, qg, kgt, qk, eg = prep
    bsz, L, _ = u.shape
    nblk = L // rows
    return pl.pallas_call(
        functools.partial(_gdn_scan_kernel, bsz=bsz, rows=rows, chunk=chunk),
        out_shape=[jax.ShapeDtypeStruct((bsz, L, B_WIDTH), BF16),
                   jax.ShapeDtypeStruct((bsz, B_HEADS, B_DK, B_DV), F32)],
        grid=(nblk,),
        in_specs=[pl.BlockSpec((bsz, rows, B_WIDTH), lambda i: (0, i, 0)),
                  pl.BlockSpec((bsz, rows, B_WIDTH), lambda i: (0, i, 0)),
                  pl.BlockSpec((bsz, rows, B_WIDTH), lambda i: (0, i, 0)),
                  pl.BlockSpec((bsz, B_WIDTH, rows), lambda i: (0, 0, i)),
                  pl.BlockSpec((bsz, rows, B_HEADS * rows), lambda i: (0, i, 0)),
                  pl.BlockSpec((bsz, rows // chunk, B_HEADS, LANE), lambda i: (0, i, 0, 0)),
                  pl.BlockSpec((bsz, rows, B_WIDTH), lambda i: (0, i, P_BZ // B_WIDTH)),
                  pl.BlockSpec((1, B_DV), lambda i: (0, 0)),
                  pl.BlockSpec((bsz, B_HEADS, B_DK, B_DV), lambda i: (0, 0, 0, 0))],
        out_specs=[pl.BlockSpec((bsz, rows, B_WIDTH), lambda i: (0, i, 0)),
                   pl.BlockSpec((bsz, B_HEADS, B_DK, B_DV), lambda i: (0, 0, 0, 0))],
        compiler_params=_params(("arbitrary",)),
        name="gdn_scan",
    )(u, w, qg, kgt, qk, eg, proj, norm_w.reshape(1, B_DV), s0)


def _mla_proj_kernel(cq_ref, ckv_ref, g1_ref, g2_ref, cos8_ref, sin8_ref, qnw_ref, wqn_ref, wqr_ref, wqs_ref,
                     kvnw_ref, wukt_ref, ckv_out, kr_out, kcat_out, q_out):
    cq = cq_ref[0]
    cqn = (cq * lax.rsqrt(jnp.mean(cq * cq, axis=-1, keepdims=True) + EPS) * qnw_ref[...]).astype(BF16)
    c = ckv_ref[0]
    ckv = c * lax.rsqrt(jnp.mean(c * c, axis=-1, keepdims=True) + EPS) * kvnw_ref[...]
    ckv_out[0] = ckv
    cos64 = cos8_ref[:, 0:C_ROPE]
    sin64 = sin8_ref[:, 0:C_ROPE]
    kr = g1_ref[0][:, 0:C_ROPE] * cos64 + g2_ref[0][:, 0:C_ROPE] * sin64
    kr_out[0] = kr
    kcat_out[0, :, 0:KV_LORA] = ckv.astype(BF16)
    kcat_out[0, :, KV_LORA:KV_LORA + C_ROPE] = kr.astype(BF16)
    kcat_out[0, :, KV_LORA + C_ROPE:KCAT] = kr.astype(BF16)

    qn = _mm(cqn, wqn_ref[...])
    qr = (_mm(cqn, wqr_ref[...]) * cos8_ref[...] + _mm(cqn, wqs_ref[...]) * sin8_ref[...]) * Q_SCALE
    lane = lax.broadcasted_iota(jnp.int32, (1, LANE), 1)
    for h in range(C_HEADS):
        ql = _mm(qn[:, h * C_NOPE:(h + 1) * C_NOPE].astype(BF16), wukt_ref[h]) * Q_SCALE
        q_out[0, h, :, 0:KV_LORA] = ql.astype(BF16)
        pair = qr[:, (h // 2) * LANE:(h // 2 + 1) * LANE]
        keep = (lane // C_ROPE) == (h % 2)
        q_out[0, h, :, KV_LORA:KCAT] = jnp.where(keep, pair, 0.0).astype(BF16)


def _mla_proj(proj, cos8, sin8, q_norm_w, wqn, wqr, wqs, kv_norm_w, wukt):
    bsz, L, _ = proj.shape
    tm = min(L, 256)
    const2 = lambda b, i: (0, 0)
    return pl.pallas_call(
        _mla_proj_kernel,
        out_shape=[jax.ShapeDtypeStruct((bsz, L, KV_LORA), F32),
                   jax.ShapeDtypeStruct((bsz, L, C_ROPE), F32),
                   jax.ShapeDtypeStruct((bsz, L, KCAT), BF16),
                   jax.ShapeDtypeStruct((bsz, C_HEADS, L, KCAT), BF16)],
        grid=(bsz, L // tm),
        in_specs=[pl.BlockSpec((1, tm, Q_LORA), lambda b, i: (b, i, P_CQ // Q_LORA)),
                  pl.BlockSpec((1, tm, KV_LORA), lambda b, i: (b, i, P_CKV // KV_LORA)),
                  pl.BlockSpec((1, tm, LANE), lambda b, i: (b, i, P_G1 // LANE)),
                  pl.BlockSpec((1, tm, LANE), lambda b, i: (b, i, P_G2 // LANE)),
                  pl.BlockSpec((tm, C_HEADS * C_ROPE), lambda b, i: (i, 0)),
                  pl.BlockSpec((tm, C_HEADS * C_ROPE), lambda b, i: (i, 0)),
                  pl.BlockSpec((1, Q_LORA), const2),
                  pl.BlockSpec((Q_LORA, C_HEADS * C_NOPE), const2),
                  pl.BlockSpec((Q_LORA, C_HEADS * C_ROPE), const2),
                  pl.BlockSpec((Q_LORA, C_HEADS * C_ROPE), const2),
                  pl.BlockSpec((1, KV_LORA), const2),
                  pl.BlockSpec((C_HEADS, C_NOPE, KV_LORA), lambda b, i: (0, 0, 0))],
        out_specs=[pl.BlockSpec((1, tm, KV_LORA), lambda b, i: (b, i, 0)),
                   pl.BlockSpec((1, tm, C_ROPE), lambda b, i: (b, i, 0)),
                   pl.BlockSpec((1, tm, KCAT), lambda b, i: (b, i, 0)),
                   pl.BlockSpec((1, C_HEADS, tm, KCAT), lambda b, i: (b, 0, i, 0))],
        compiler_params=_params(("arbitrary", "arbitrary")),
        name="mla_proj",
    )(proj, proj, proj, proj, cos8, sin8, q_norm_w.reshape(1, Q_LORA), wqn, wqr, wqs,
      kv_norm_w.reshape(1, KV_LORA), wukt)


def _attn_finish(acc, l, wuv_ref, yc_ref, tq):
    o = acc * (1.0 / l)
    for h in range(C_HEADS):
        oh = o[h * tq:(h + 1) * tq].astype(BF16)
        yc_ref[0, :, h * C_VDIM:(h + 1) * C_VDIM] = _mm(oh, wuv_ref[h]).astype(BF16)


def _attn_prompt_kernel(q_ref, k_ref, wuv_ref, yc_ref, m_scr, l_scr, acc_scr, *, tq, tk, hpg):
    qi = pl.program_id(1)
    rows = C_HEADS * tq
    ngrp = C_HEADS // hpg
    grows = hpg * tq
    m_scr[...] = jnp.full((rows, LANE), -1e30, F32)
    l_scr[...] = jnp.zeros((rows, LANE), F32)
    acc_scr[...] = jnp.zeros((rows, KV_LORA), F32)

    def block(kj, masked):
        start = pl.multiple_of(kj * tk, tk)
        k = k_ref[0, pl.ds(start, tk), :]
        v = k[:, 0:KV_LORA]

        def scores(g):
            return _mm_nt(q_ref[0, g * hpg:(g + 1) * hpg].reshape(grows, KCAT), k)

        s_next = scores(0)
        for g in range(ngrp):
            rs = slice(g * grows, (g + 1) * grows)
            s = s_next
            if g + 1 < ngrp:
                s_next = scores(g + 1)
            if masked:
                ri = lax.broadcasted_iota(jnp.int32, (grows, tk), 0)
                ci = lax.broadcasted_iota(jnp.int32, (grows, tk), 1)
                ok = ((qi * tq + ri % tq) // CHUNK) >= ((start + ci) // CHUNK)
                s = jnp.where(ok, s, -jnp.inf)
            m_prev = m_scr[rs]
            m_new = jnp.maximum(m_prev, jnp.max(s, axis=-1, keepdims=True))
            alpha = jnp.exp2(m_prev - m_new)
            p = jnp.exp2(s - _lane_tile(m_new, tk // LANE))
            l_scr[rs] = alpha * l_scr[rs] + jnp.sum(p, axis=-1, keepdims=True)
            acc_scr[rs] = acc_scr[rs] * _lane_tile(alpha, KV_LORA // LANE) + _mm(p.astype(BF16), v)
            m_scr[rs] = m_new

    def body(kj, carry):
        block(kj, False)
        return carry

    nfull = (qi * tq) // tk
    lax.fori_loop(0, nfull, body, 0)
    block(nfull, True)
    _attn_finish(acc_scr[...], _lane_tile(l_scr[...], KV_LORA // LANE), wuv_ref, yc_ref, tq)


def _attn_prompt(q, kcat, wuv):
    bsz, _, L, _ = q.shape
    tq = min(L, 256)
    tk = min(L, 512)
    rows = C_HEADS * tq
    return pl.pallas_call(
        functools.partial(_attn_prompt_kernel, tq=tq, tk=tk, hpg=2),
        out_shape=jax.ShapeDtypeStruct((bsz, L, C_WIDTH), BF16),
        grid=(bsz, L // tq),
        in_specs=[pl.BlockSpec((1, C_HEADS, tq, KCAT), lambda b, i: (b, 0, i, 0)),
                  pl.BlockSpec((1, L, KCAT), lambda b, i: (b, 0, 0)),
                  pl.BlockSpec((C_HEADS, KV_LORA, C_VDIM), lambda b, i: (0, 0, 0))],
        out_specs=pl.BlockSpec((1, tq, C_WIDTH), lambda b, i: (b, i, 0)),
        scratch_shapes=[pltpu.VMEM((rows, LANE), F32), pltpu.VMEM((rows, LANE), F32),
                        pltpu.VMEM((rows, KV_LORA), F32)],
        compiler_params=_params(("arbitrary", "arbitrary")),
        name="attn_prompt",
    )(q, kcat, wuv)


def _attn_full_kernel(q_ref, k_ref, wuv_ref, yc_ref, *, tq, nkeys):
    rows = C_HEADS * tq
    q = q_ref[0].reshape(rows, KCAT)
    k = k_ref[0]
    s = _mm_nt(q, k)
    ci = lax.broadcasted_iota(jnp.int32, s.shape, 1)
    s = jnp.where(ci < nkeys, s, -jnp.inf)
    m = jnp.max(s, axis=-1, keepdims=True)
    p = jnp.exp2(s - m)
    l = jnp.sum(p, axis=-1, keepdims=True)
    acc = _mm(p.astype(BF16), k[:, 0:KV_LORA])
    _attn_finish(acc, l, wuv_ref, yc_ref, tq)


def _attn_full(q, kcat, wuv, nkeys):
    bsz, _, tq, _ = q.shape
    npad = kcat.shape[1]
    return pl.pallas_call(
        functools.partial(_attn_full_kernel, tq=tq, nkeys=nkeys),
        out_shape=jax.ShapeDtypeStruct((bsz, tq, C_WIDTH), BF16),
        grid=(bsz,),
        in_specs=[pl.BlockSpec((1, C_HEADS, tq, KCAT), lambda b: (b, 0, 0, 0)),
                  pl.BlockSpec((1, npad, KCAT), lambda b: (b, 0, 0)),
                  pl.BlockSpec((C_HEADS, KV_LORA, C_VDIM), lambda b: (0, 0, 0))],
        out_specs=pl.BlockSpec((1, tq, C_WIDTH), lambda b: (b, 0, 0)),
        compiler_params=_params(("arbitrary",)),
        name="attn_full",
    )(q, kcat, wuv)


def _outproj_kernel(x_ref, ya_ref, yb_ref, yc_ref, wo_ref, o_ref):
    tn = 512
    for j in range(0, D_MODEL, tn):
        acc = _mm(ya_ref[...], wo_ref[0:A_WIDTH, j:j + tn])
        acc = acc + _mm(yb_ref[...], wo_ref[A_WIDTH:A_WIDTH + B_WIDTH, j:j + tn])
        acc = acc + _mm(yc_ref[...], wo_ref[A_WIDTH + B_WIDTH:D_MODEL, j:j + tn])
        o_ref[:, j:j + tn] = x_ref[:, j:j + tn] + acc


def _outproj(x, ya, yb, yc, wo):
    m = x.shape[0]
    tm = min(m, 256)
    row = lambda i: (i, 0)
    return pl.pallas_call(
        _outproj_kernel,
        out_shape=jax.ShapeDtypeStruct((m, D_MODEL), F32),
        grid=(m // tm,),
        in_specs=[pl.BlockSpec((tm, D_MODEL), row),
                  pl.BlockSpec((tm, A_WIDTH), row),
                  pl.BlockSpec((tm, B_WIDTH), row),
                  pl.BlockSpec((tm, C_WIDTH), row),
                  pl.BlockSpec((D_MODEL, D_MODEL), lambda i: (0, 0))],
        out_specs=pl.BlockSpec((tm, D_MODEL), row),
        compiler_params=_params(("arbitrary",)),
        name="outproj",
    )(x, ya, yb, yc, wo)


def _ffn_kernel(x_ref, nw_ref, wu_ref, wd_ref, fw_ref, o_ref, xn_scr, acc_scr, *, final_norm):
    f = pl.program_id(1)

    @pl.when(f == 0)
    def _():
        x = x_ref[...]
        ms = jnp.mean(x * x, axis=-1, keepdims=True)
        xn_scr[...] = (x * lax.rsqrt(ms + EPS) * nw_ref[...]).astype(BF16)
        acc_scr[...] = jnp.zeros_like(acc_scr)

    hid = jnp.maximum(_mm(xn_scr[...], wu_ref[...]), 0.0)
    acc_scr[...] += _mm((hid * hid).astype(BF16), wd_ref[...])

    @pl.when(f == pl.num_programs(1) - 1)
    def _():
        y = x_ref[...] + acc_scr[...]
        if final_norm:
            ms = jnp.mean(y * y, axis=-1, keepdims=True)
            y = y * lax.rsqrt(ms + EPS) * fw_ref[...]
        o_ref[...] = y


def _ffn(x, norm_w, wu, wd, final_w, final_norm):
    m = x.shape[0]
    tm = min(m, 512)
    tf = 512
    return pl.pallas_call(
        functools.partial(_ffn_kernel, final_norm=final_norm),
        out_shape=jax.ShapeDtypeStruct((m, D_MODEL), F32),
        grid=(m // tm, D_FF // tf),
        in_specs=[pl.BlockSpec((tm, D_MODEL), lambda i, f: (i, 0)),
                  pl.BlockSpec((1, D_MODEL), lambda i, f: (0, 0)),
                  pl.BlockSpec((D_MODEL, tf), lambda i, f: (0, f)),
                  pl.BlockSpec((tf, D_MODEL), lambda i, f: (f, 0)),
                  pl.BlockSpec((1, D_MODEL), lambda i, f: (0, 0))],
        out_specs=pl.BlockSpec((tm, D_MODEL), lambda i, f: (i, 0)),
        scratch_shapes=[pltpu.VMEM((tm, D_MODEL), BF16), pltpu.VMEM((tm, D_MODEL), F32)],
        compiler_params=_params(("arbitrary", "arbitrary")),
        name="ffn",
    )(x, norm_w.reshape(1, D_MODEL), wu, wd, final_w.reshape(1, D_MODEL))


def _swap_halves(w):
    half = C_ROPE // 2
    return jnp.concatenate([w[..., half:], w[..., :half]], axis=-1)


def _prep_w_in(w_in):
    k = w_in.shape[0]
    ckr = w_in[:, OFF_CKR:IN_COLS]
    g1 = jnp.concatenate([ckr, w_in[:, OFF_BBETA:OFF_CQ], jnp.zeros((k, LANE - C_ROPE - 2 * B_HEADS), F32)], axis=1)
    g2 = jnp.concatenate([_swap_halves(ckr), jnp.zeros((k, LANE - C_ROPE), F32)], axis=1)
    parts = [w_in[:, OFF_BQKV:OFF_BZ], w_in[:, OFF_AU:OFF_AV], w_in[:, OFF_AV:OFF_BQKV], w_in[:, OFF_BZ:OFF_BBETA],
             w_in[:, OFF_CQ:OFF_CKV], w_in[:, OFF_CKV:OFF_CKR], g1, g2]
    return jnp.concatenate(parts, axis=1).astype(BF16)


def _rope_tables(pos):
    half = C_ROPE // 2
    inv = ROPE_THETA ** (-jnp.arange(half, dtype=F32) / half)
    ang = pos.astype(F32)[:, None] * inv[None, :]
    cos, sin = jnp.cos(ang), jnp.sin(ang)
    cos8 = jnp.tile(jnp.concatenate([cos, cos], axis=1), (1, C_HEADS))
    sin8 = jnp.tile(jnp.concatenate([-sin, sin], axis=1), (1, C_HEADS))
    return cos8, sin8


def _layer(x, bsz, L, tabs, lw, past, final_w, final_norm):
    proj = _inproj(x, lw['attn_norm_w'], lw['w_in']).reshape(bsz, L, P_COLS)

    ya, va = _mixer_a(proj, lw['a_ln_w'], lw['a_ln_b'], lw['a_w_s'], lw['a_b_s'], past is not None)

    if past is None:
        rows, chunk = 2 * CHUNK, CHUNK
        buf8 = jnp.zeros((bsz, 8, B_QKV), F32)
        s0 = jnp.zeros((bsz, B_HEADS, B_DK, B_DV), F32)
    else:
        rows, chunk = L, L
        ckv_past, kr_past, s0, conv_buf = past
        buf8 = jnp.concatenate([jnp.zeros((bsz, 8 - (B_CONV - 1), B_QKV), F32), conv_buf], axis=1)
    prep = _gdn_prep(proj, buf8, lw['b_conv_w'], lw['b_a_log'], lw['b_dt_bias'], rows, chunk)
    yb, s_new = _gdn_scan(prep, proj, lw['b_norm_w'], s0, rows, chunk)
    buf_new = proj[:, L - (B_CONV - 1):, P_BQKV:P_BQKV + B_QKV]

    cos8, sin8 = tabs
    ckv, kr, kcat, q = _mla_proj(proj, cos8, sin8, lw['c_q_norm_w'], lw['wqn'], lw['wqr'], lw['wqs'],
                                 lw['c_kv_norm_w'], lw['wukt'])
    if past is None:
        yc = _attn_prompt(q, kcat, lw['wuv'])
    else:
        past_len = ckv_past.shape[1]
        nkeys = past_len + L
        npad = -(-nkeys // LANE) * LANE
        kpast = jnp.concatenate([ckv_past, kr_past, kr_past], axis=-1).astype(BF16)
        kall = jnp.concatenate([kpast, kcat, jnp.zeros((bsz, npad - nkeys, KCAT), BF16)], axis=1)
        yc = _attn_full(q, kall, lw['wuv'], nkeys)

    m = bsz * L
    x = _outproj(x, ya.reshape(m, A_WIDTH), yb.reshape(m, B_WIDTH), yc.reshape(m, C_WIDTH), lw['w_out'])
    x = _ffn(x, lw['mlp_norm_w'], lw['w_up'], lw['w_down'], final_w, final_norm)
    return x, (ckv, kr, s_new, buf_new, va)


def kernel(x_prompt, x_sample, cache_mla_ckv, cache_mla_krope, state_gdn, state_gdn_conv,
           attn_norm_w, w_in, a_ln_w, a_ln_b, a_w_s, a_b_s, b_conv_w, b_a_log, b_dt_bias, b_norm_w,
           c_q_norm_w, c_w_uq, c_kv_norm_w, c_w_uk, c_w_uv, w_out, mlp_norm_w, w_up, w_down, final_norm_w):
    depth = w_in.shape[0]
    bp, lp, _ = x_prompt.shape
    bs, ls, _ = x_sample.shape
    past_len = cache_mla_ckv.shape[2]
    tabs_p = _rope_tables(jnp.arange(lp, dtype=jnp.int32))
    tabs_s = _rope_tables(past_len + jnp.arange(ls, dtype=jnp.int32))
    hp = x_prompt.reshape(bp * lp, D_MODEL)
    hs = x_sample.reshape(bs * ls, D_MODEL)
    sp_list, ss_list = [], []
    for l in range(depth):
        wq_rope = c_w_uq[l][:, :, C_NOPE:]
        lw = {
            'attn_norm_w': attn_norm_w[l], 'w_in': _prep_w_in(w_in[l]),
            'a_ln_w': a_ln_w[l], 'a_ln_b': a_ln_b[l], 'a_w_s': a_w_s[l], 'a_b_s': a_b_s[l],
            'b_conv_w': b_conv_w[l], 'b_a_log': b_a_log[l], 'b_dt_bias': b_dt_bias[l], 'b_norm_w': b_norm_w[l],
            'c_q_norm_w': c_q_norm_w[l], 'c_kv_norm_w': c_kv_norm_w[l],
            'wqn': c_w_uq[l][:, :, :C_NOPE].reshape(Q_LORA, C_HEADS * C_NOPE).astype(BF16),
            'wqr': wq_rope.reshape(Q_LORA, C_HEADS * C_ROPE).astype(BF16),
            'wqs': _swap_halves(wq_rope).reshape(Q_LORA, C_HEADS * C_ROPE).astype(BF16),
            'wukt': jnp.transpose(c_w_uk[l], (1, 2, 0)).astype(BF16),
            'wuv': jnp.transpose(c_w_uv[l], (1, 0, 2)).astype(BF16),
            'w_out': w_out[l].astype(BF16), 'mlp_norm_w': mlp_norm_w[l],
            'w_up': w_up[l].astype(BF16), 'w_down': w_down[l].astype(BF16),
        }
        last = l == depth - 1
        hp, sp = _layer(hp, bp, lp, tabs_p, lw, None, final_norm_w, last)
        hs, ss = _layer(hs, bs, ls, tabs_s, lw,
                        (cache_mla_ckv[l], cache_mla_krope[l], state_gdn[l], state_gdn_conv[l]), final_norm_w, last)
        sp_list.append(sp)
        ss_list.append(ss)

    def stack(lst, i):
        return jnp.stack([s[i] for s in lst], axis=0)

    return (hp.reshape(bp, lp, D_MODEL), hs.reshape(bs, ls, D_MODEL),
            stack(sp_list, 0), stack(sp_list, 1), stack(sp_list, 2), stack(sp_list, 3),
            stack(ss_list, 0), stack(ss_list, 1), stack(ss_list, 2), stack(ss_list, 3), stack(ss_list, 4))
```

```python
import functools
import math

import jax
import jax.numpy as jnp
from jax import lax
from jax.experimental import pallas as pl
from jax.experimental.pallas import tpu as pltpu

F32 = jnp.float32
BF16 = jnp.bfloat16

D_MODEL = 2048
CHUNK = 64
EPS = 1e-6
LN_EPS = 1e-5
A_GROUPS = 4
A_HEAD = 128
A_WIDTH = 512
A_CHUNK = 128
B_HEADS = 4
B_DK = 128
B_DV = 128
B_WIDTH = 512
B_QKV = 1536
B_CONV = 4
C_HEADS = 8
C_NOPE = 128
C_ROPE = 64
C_VDIM = 128
C_WIDTH = 1024
Q_LORA = 512
KV_LORA = 256
ROPE_THETA = 10000.0
C_SCALE = (C_NOPE + C_ROPE) ** -0.5
Q_SCALE = C_SCALE * math.log2(math.e)
D_FF = 4 * D_MODEL

OFF_AU = 0
OFF_AV = OFF_AU + A_WIDTH
OFF_BQKV = OFF_AV + A_WIDTH
OFF_BZ = OFF_BQKV + B_QKV
OFF_BBETA = OFF_BZ + B_WIDTH
OFF_BALPHA = OFF_BBETA + B_HEADS
OFF_CQ = OFF_BALPHA + B_HEADS
OFF_CKV = OFF_CQ + Q_LORA
OFF_CKR = OFF_CKV + KV_LORA
IN_COLS = OFF_CKR + C_ROPE

P_BQKV = 0
P_AU = 1536
P_AV = 2048
P_BZ = 2560
P_CQ = 3072
P_CKV = 3584
P_G1 = 3840
P_G2 = 3968
P_COLS = 4096
LANE = 128
G1_BETA = C_ROPE
G1_ALPHA = C_ROPE + B_HEADS
KCAT = KV_LORA + 2 * C_ROPE
QK_DIM = C_NOPE + 2 * C_ROPE

VMEM_LIMIT = 56 * 1024 * 1024


def _mm(a, b):
    return jnp.dot(a, b, preferred_element_type=F32)


def _mm_nt(a, b):
    return lax.dot_general(a, b, (((1,), (1,)), ((), ())), preferred_element_type=F32)


def _split2(x):
    hi = x.astype(BF16)
    lo = (x - hi.astype(F32)).astype(BF16)
    return hi, lo


def _split3(x):
    hi = x.astype(BF16)
    r = x - hi.astype(F32)
    mid = r.astype(BF16)
    lo = (r - mid.astype(F32)).astype(BF16)
    return hi, mid, lo


def _cat3_lhs(a):
    ah, al = _split2(a)
    return jnp.concatenate([ah, ah, al], axis=1)


def _cat3_rhs(b):
    bh, bl = _split2(b)
    return jnp.concatenate([bh, bl, bh], axis=0)


def _mm3(a, b):
    return _mm(_cat3_lhs(a), _cat3_rhs(b))


def _mm_exact_lhs(a, b):
    return _mm(jnp.concatenate([a, a, a], axis=1), jnp.concatenate(_split3(b), axis=0))


def _mm_nt_exact_lhs(a, b):
    return _mm_nt(jnp.concatenate([a, a, a], axis=1), jnp.concatenate(_split3(b), axis=1))


def _lane_tile(x, n):
    return x if n == 1 else jnp.concatenate([x] * n, axis=1)


def _params(sem):
    return pltpu.CompilerParams(dimension_semantics=sem, vmem_limit_bytes=VMEM_LIMIT)


def _inproj_kernel(x_ref, nw_ref, w_ref, o_ref):
    x = x_ref[...]
    ms = jnp.mean(x * x, axis=-1, keepdims=True)
    xn = (x * lax.rsqrt(ms + EPS) * nw_ref[...]).astype(BF16)
    o_ref[...] = _mm(xn, w_ref[...])


def _inproj(x, norm_w, w_pad):
    m = x.shape[0]
    tm = min(m, 512)
    tn = 1024
    return pl.pallas_call(
        _inproj_kernel,
        out_shape=jax.ShapeDtypeStruct((m, P_COLS), F32),
        grid=(P_COLS // tn, m // tm),
        in_specs=[pl.BlockSpec((tm, D_MODEL), lambda j, i: (i, 0)),
                  pl.BlockSpec((1, D_MODEL), lambda j, i: (0, 0)),
                  pl.BlockSpec((D_MODEL, tn), lambda j, i: (0, j))],
        out_specs=pl.BlockSpec((tm, tn), lambda j, i: (i, j)),
        compiler_params=_params(("arbitrary", "arbitrary")),
        name="inproj",
    )(x, norm_w.reshape(1, D_MODEL), w_pad)


def _gelu(x):
    return 0.5 * x * (1.0 + lax.erf(x * (2.0 ** -0.5)))


def _mixa_kernel(u_ref, v_ref, lnw_ref, lnb_ref, ws_ref, bs_ref, ya_ref, *va_ref, lc, nsub):
    u = _gelu(u_ref[0])
    v = _gelu(v_ref[0])
    mu = jnp.mean(v, axis=-1, keepdims=True)
    vc = v - mu
    var = jnp.mean(vc * vc, axis=-1, keepdims=True)
    vn = vc * lax.rsqrt(var + LN_EPS) * lnw_ref[...] + lnb_ref[...]
    if va_ref:
        va_ref[0][0] = vn
    for s in range(nsub):
        rows = slice(s * lc, (s + 1) * lc)
        for g in range(A_GROUPS):
            cols = slice(g * A_HEAD, (g + 1) * A_HEAD)
            f = _mm(ws_ref[g], vn[rows, cols].astype(BF16)) + bs_ref[g]
            ya_ref[0, rows, cols] = (u[rows, cols] * f).astype(BF16)


def _mixer_a(proj, ln_w, ln_b, w_s, b_s, want_va):
    bsz, L, _ = proj.shape
    lc = min(L, A_CHUNK)
    rows = min(L, 512)
    nsub = rows // lc
    idx = jnp.arange(lc)
    allowed = (idx[:, None] // CHUNK) >= (idx[None, :] // CHUNK)
    ws = jnp.where(allowed[None], w_s[:, :lc, :lc], 0.0).astype(BF16)
    bs = jnp.broadcast_to(b_s[:, :lc, None], (A_GROUPS, lc, A_HEAD)).astype(F32)
    out_shape = [jax.ShapeDtypeStruct((bsz, L, A_WIDTH), BF16)]
    out_specs = [pl.BlockSpec((1, rows, A_WIDTH), lambda b, i: (b, i, 0))]
    if want_va:
        out_shape.append(jax.ShapeDtypeStruct((bsz, L, A_WIDTH), F32))
        out_specs.append(pl.BlockSpec((1, rows, A_WIDTH), lambda b, i: (b, i, 0)))
    res = pl.pallas_call(
        functools.partial(_mixa_kernel, lc=lc, nsub=nsub),
        out_shape=out_shape,
        grid=(bsz, L // rows),
        in_specs=[pl.BlockSpec((1, rows, A_WIDTH), lambda b, i: (b, i, P_AU // A_WIDTH)),
                  pl.BlockSpec((1, rows, A_WIDTH), lambda b, i: (b, i, P_AV // A_WIDTH)),
                  pl.BlockSpec((1, A_WIDTH), lambda b, i: (0, 0)),
                  pl.BlockSpec((1, A_WIDTH), lambda b, i: (0, 0)),
                  pl.BlockSpec((A_GROUPS, lc, lc), lambda b, i: (0, 0, 0)),
                  pl.BlockSpec((A_GROUPS, lc, A_HEAD), lambda b, i: (0, 0, 0))],
        out_specs=out_specs,
        compiler_params=_params(("arbitrary", "arbitrary")),
        name="mixer_a",
    )(proj, proj, ln_w.reshape(1, A_WIDTH), ln_b.reshape(1, A_WIDTH), ws, bs)
    return res if want_va else (res[0], None)


def _softplus(x):
    return jnp.maximum(x, 0.0) + jnp.log1p(jnp.exp(-jnp.abs(x)))


def _gdn_prep_kernel(x_ref, halo_ref, buf_ref, g1_ref, cw_ref, alog_ref, dtb_ref,
                     u_ref, w_ref, qg_ref, kgt_ref, qk_ref, eg_ref, xp_scr, *, nb, rows, chunk):
    i = pl.program_id(1)
    nsub = rows // chunk
    ri = lax.broadcasted_iota(jnp.int32, (rows, rows), 0)
    ci = lax.broadcasted_iota(jnp.int32, (rows, rows), 1)
    same = (ri // chunk) == (ci // chunk)
    causal = same & (ri >= ci)
    strict = same & (ri > ci)
    tri = jnp.where(causal, 1.0, 0.0).astype(BF16)
    eye_r = jnp.where(ri == ci, 1.0, 0.0).astype(F32)
    li = lax.broadcasted_iota(jnp.int32, (LANE, LANE), 0)
    lj = lax.broadcasted_iota(jnp.int32, (LANE, LANE), 1)
    eye_l = jnp.where(li == lj, 1.0, 0.0).astype(BF16)

    ys, betas, gts = [], [], []
    for bb in range(nb):
        x = x_ref[bb]
        xp_scr[bb, 0:8, :] = jnp.where(i == 0, buf_ref[bb], halo_ref[bb])
        xp_scr[bb, 8:8 + rows, :] = x
        y = x * cw_ref[B_CONV - 1:B_CONV, :]
        for k in range(1, B_CONV):
            y = y + xp_scr[bb, 8 - k:8 - k + rows, :] * cw_ref[B_CONV - 1 - k:B_CONV - k, :]
        ys.append(y * jax.nn.sigmoid(y))
        g1 = g1_ref[bb]
        betas.append(jax.nn.sigmoid(g1))
        gts.append(-jnp.exp(alog_ref[...]) * _softplus(g1 + dtb_ref[...]))

    gcs = [_mm_exact_lhs(tri, g) for g in gts]
    gcts = [_mm_nt_exact_lhs(eye_l, gc) for gc in gcs]
    glasts = [jnp.concatenate(
        [jnp.broadcast_to(gc[(s + 1) * chunk - 1:(s + 1) * chunk, :], (chunk, LANE)) for s in range(nsub)], axis=0)
        for gc in gcs]

    pairs = [(bb, h) for bb in range(nb) for h in range(B_HEADS)]
    qs, ks, kbs, rhss, decs, egcs = [], [], [], [], [], []
    for bb, h in pairs:
        y, gc = ys[bb], gcs[bb]
        bcol = betas[bb][:, G1_BETA + h:G1_BETA + h + 1]
        gcol = gc[:, G1_ALPHA + h:G1_ALPHA + h + 1]
        grow = gcts[bb][G1_ALPHA + h:G1_ALPHA + h + 1, :]
        qh = y[:, h * B_DK:(h + 1) * B_DK]
        qh = qh * lax.rsqrt(jnp.sum(qh * qh, axis=-1, keepdims=True) + EPS) * (B_DK ** -0.5)
        kh = y[:, B_WIDTH + h * B_DK:B_WIDTH + (h + 1) * B_DK]
        kh = kh * lax.rsqrt(jnp.sum(kh * kh, axis=-1, keepdims=True) + EPS)
        vh = y[:, 2 * B_WIDTH + h * B_DV:2 * B_WIDTH + (h + 1) * B_DV]
        kb = kh * bcol
        egc = jnp.exp(gcol)
        qs.append(qh)
        ks.append(kh)
        kbs.append(kb)
        egcs.append(egc)
        rhss.append(jnp.concatenate([vh * bcol, kb * egc], axis=1))
        decs.append(jnp.where(causal, jnp.exp(jnp.minimum(gcol - grow, 0.0)), 0.0))

    khb = [k.astype(BF16) for k in ks]
    kk = [_mm_nt(kb.astype(BF16), kx) for kb, kx in zip(kbs, khb)]
    qkm = [_mm_nt(q.astype(BF16), kx) for q, kx in zip(qs, khb)]

    ps = [jnp.where(strict, -(a * d), 0.0) for a, d in zip(kk, decs)]
    ts = [eye_r + p for p in ps]
    ms = [_mm(_cat3_lhs(p), _cat3_rhs(p)) for p in ps]
    nsq = int(math.log2(chunk)) - 1
    for step in range(nsq):
        mr = [_cat3_rhs(m) for m in ms]
        upd = [_mm(_cat3_lhs(t), b) for t, b in zip(ts, mr)]
        if step + 1 < nsq:
            ms = [_mm(_cat3_lhs(m), b) for m, b in zip(ms, mr)]
        ts = [t + d for t, d in zip(ts, upd)]
    sols = [_mm3(t, r) for t, r in zip(ts, rhss)]

    for n, (bb, h) in enumerate(pairs):
        hs = slice(h * B_DK, (h + 1) * B_DK)
        gcol = gcs[bb][:, G1_ALPHA + h:G1_ALPHA + h + 1]
        glcol = glasts[bb][:, G1_ALPHA + h:G1_ALPHA + h + 1]
        u_ref[bb, :, hs] = sols[n][:, :B_DV]
        w_ref[bb, :, hs] = sols[n][:, B_DV:].astype(BF16)
        qg_ref[bb, :, hs] = (qs[n] * egcs[n]).astype(BF16)
        kg = (ks[n] * jnp.exp(glcol - gcol)).astype(BF16)
        kgt_ref[bb, hs, :] = _mm_nt(eye_l, kg).astype(BF16)
        qk_ref[bb, :, h * rows:(h + 1) * rows] = (qkm[n] * decs[n]).astype(BF16)
        for s in range(nsub):
            eg_ref[bb, s, h:h + 1, :] = jnp.broadcast_to(
                jnp.exp(glasts[bb][s * chunk:s * chunk + 1, G1_ALPHA + h:G1_ALPHA + h + 1]), (1, LANE))


def _gdn_prep(proj, buf8, conv_w, a_log, dt_bias, rows, chunk):
    bsz, L, _ = proj.shape
    nb = 2
    nblk = L // rows
    rpb = max(rows // 8, 1)
    alog = jnp.zeros((1, LANE), F32).at[0, G1_ALPHA:G1_ALPHA + B_HEADS].set(a_log)
    dtb = jnp.zeros((1, LANE), F32).at[0, G1_ALPHA:G1_ALPHA + B_HEADS].set(dt_bias)
    return pl.pallas_call(
        functools.partial(_gdn_prep_kernel, nb=nb, rows=rows, chunk=chunk),
        out_shape=[jax.ShapeDtypeStruct((bsz, L, B_WIDTH), F32),
                   jax.ShapeDtypeStruct((bsz, L, B_WIDTH), BF16),
                   jax.ShapeDtypeStruct((bsz, L, B_WIDTH), BF16),
                   jax.ShapeDtypeStruct((bsz, B_WIDTH, L), BF16),
                   jax.ShapeDtypeStruct((bsz, L, B_HEADS * rows), BF16),
                   jax.ShapeDtypeStruct((bsz, L // chunk, B_HEADS, LANE), F32)],
        grid=(bsz // nb, nblk),
        in_specs=[pl.BlockSpec((nb, rows, B_QKV), lambda b, i: (b, i, 0)),
                  pl.BlockSpec((nb, 8, B_QKV), lambda b, i: (b, jnp.maximum(i * rpb - 1, 0), 0)),
                  pl.BlockSpec((nb, 8, B_QKV), lambda b, i: (b, 0, 0)),
                  pl.BlockSpec((nb, rows, LANE), lambda b, i: (b, i, P_G1 // LANE)),
                  pl.BlockSpec((B_CONV, B_QKV), lambda b, i: (0, 0)),
                  pl.BlockSpec((1, LANE), lambda b, i: (0, 0)),
                  pl.BlockSpec((1, LANE), lambda b, i: (0, 0))],
        out_specs=[pl.BlockSpec((nb, rows, B_WIDTH), lambda b, i: (b, i, 0)),
                   pl.BlockSpec((nb, rows, B_WIDTH), lambda b, i: (b, i, 0)),
                   pl.BlockSpec((nb, rows, B_WIDTH), lambda b, i: (b, i, 0)),
                   pl.BlockSpec((nb, B_WIDTH, rows), lambda b, i: (b, 0, i)),
                   pl.BlockSpec((nb, rows, B_HEADS * rows), lambda b, i: (b, i, 0)),
                   pl.BlockSpec((nb, rows // chunk, B_HEADS, LANE), lambda b, i: (b, i, 0, 0))],
        scratch_shapes=[pltpu.VMEM((nb, rows + 8, B_QKV), F32)],
        compiler_params=_params(("arbitrary", "arbitrary")),
        name="gdn_prep",
    )(proj, proj, buf8, proj, conv_w, alog, dtb)


def _gdn_scan_kernel(u_ref, w_ref, qg_ref, kgt_ref, qk_ref, eg_ref, z_ref, nw_ref, s0_ref,
                     yb_ref, s_ref, *, bsz, rows, chunk):
    @pl.when(pl.program_id(0) == 0)
    def _():
        s_ref[...] = s0_ref[...]

    nsub = rows // chunk
    pairs = [(b, h) for b in range(bsz) for h in range(B_HEADS)]
    st = [s_ref[b, h] for b, h in pairs]
    for s in range(nsub):
        rs = slice(s * chunk, (s + 1) * chunk)
        r1 = [_mm(jnp.concatenate([w_ref[b, rs, h * B_DV:(h + 1) * B_DV], qg_ref[b, rs, h * B_DV:(h + 1) * B_DV]],
                                  axis=0), x.astype(BF16))
              for (b, h), x in zip(pairs, st)]
        vb = [(u_ref[b, rs, h * B_DV:(h + 1) * B_DV] - r[:chunk]).astype(BF16) for (b, h), r in zip(pairs, r1)]
        r2 = [_mm(jnp.concatenate([qk_ref[b, rs, h * rows + s * chunk:h * rows + (s + 1) * chunk],
                                   kgt_ref[b, h * B_DK:(h + 1) * B_DK, rs]], axis=0), v)
              for (b, h), v in zip(pairs, vb)]
        for n, (b, h) in enumerate(pairs):
            hs = slice(h * B_DV, (h + 1) * B_DV)
            o = r1[n][chunk:] + r2[n][:chunk]
            st[n] = st[n] * eg_ref[b, s, h:h + 1, :] + r2[n][chunk:]
            z = z_ref[b, rs, hs]
            on = o * lax.rsqrt(jnp.mean(o * o, axis=-1, keepdims=True) + EPS) * nw_ref[...]
            yb_ref[b, rs, hs] = (on * (z * jax.nn.sigmoid(z))).astype(BF16)
    for n, (b, h) in enumerate(pairs):
        s_ref[b, h] = st[n]


def _gdn_scan(prep, proj, norm_w, s0, rows, chunk):
    u, w, qg, kgt, qk, eg = prep
    bsz, L, _ = u.shape
    nblk = L // rows
    return pl.pallas_call(
        functools.partial(_gdn_scan_kernel, bsz=bsz, rows=rows, chunk=chunk),
        out_shape=[jax.ShapeDtypeStruct((bsz, L, B_WIDTH), BF16),
                   jax.ShapeDtypeStruct((bsz, B_HEADS, B_DK, B_DV), F32)],
        grid=(nblk,),
        in_specs=[pl.BlockSpec((bsz, rows, B_WIDTH), lambda i: (0, i, 0)),
                  pl.BlockSpec((bsz, rows, B_WIDTH), lambda i: (0, i, 0)),
                  pl.BlockSpec((bsz, rows, B_WIDTH), lambda i: (0, i, 0)),
                  pl.BlockSpec((bsz, B_WIDTH, rows), lambda i: (0, 0, i)),
                  pl.BlockSpec((bsz, rows, B_HEADS * rows), lambda i: (0, i, 0)),
                  pl.BlockSpec((bsz, rows // chunk, B_HEADS, LANE), lambda i: (0, i, 0, 0)),
                  pl.BlockSpec((bsz, rows, B_WIDTH), lambda i: (0, i, P_BZ // B_WIDTH)),
                  pl.BlockSpec((1, B_DV), lambda i: (0, 0)),
                  pl.BlockSpec((bsz, B_HEADS, B_DK, B_DV), lambda i: (0, 0, 0, 0))],
        out_specs=[pl.BlockSpec((bsz, rows, B_WIDTH), lambda i: (0, i, 0)),
                   pl.BlockSpec((bsz, B_HEADS, B_DK, B_DV), lambda i: (0, 0, 0, 0))],
        compiler_params=_params(("arbitrary",)),
        name="gdn_scan",
    )(u, w, qg, kgt, qk, eg, proj, norm_w.reshape(1, B_DV), s0)


def _mla_common(cq_ref, ckv_ref, g1_ref, g2_ref, cos8_ref, sin8_ref, qnw_ref, wqn_ref, wqr_ref, wqs_ref, kvnw_ref):
    cq = cq_ref[0]
    cqn = (cq * lax.rsqrt(jnp.mean(cq * cq, axis=-1, keepdims=True) + EPS) * qnw_ref[...]).astype(BF16)
    c = ckv_ref[0]
    ckv = c * lax.rsqrt(jnp.mean(c * c, axis=-1, keepdims=True) + EPS) * kvnw_ref[...]
    kr = g1_ref[0][:, 0:C_ROPE] * cos8_ref[:, 0:C_ROPE] + g2_ref[0][:, 0:C_ROPE] * sin8_ref[:, 0:C_ROPE]
    qn = _mm(cqn, wqn_ref[...])
    qr = (_mm(cqn, wqr_ref[...]) * cos8_ref[...] + _mm(cqn, wqs_ref[...]) * sin8_ref[...]) * Q_SCALE
    return ckv, kr, qn, qr


def _rope_slot(qr, h):
    lane = lax.broadcasted_iota(jnp.int32, (1, LANE), 1)
    pair = qr[:, (h // 2) * LANE:(h // 2 + 1) * LANE]
    return jnp.where((lane // C_ROPE) == (h % 2), pair, 0.0).astype(BF16)


def _mla_proj_kernel(cq_ref, ckv_ref, g1_ref, g2_ref, cos8_ref, sin8_ref, qnw_ref, wqn_ref, wqr_ref, wqs_ref,
                     kvnw_ref, wukt_ref, ckv_out, kr_out, kcat_out, q_out):
    ckv, kr, qn, qr = _mla_common(cq_ref, ckv_ref, g1_ref, g2_ref, cos8_ref, sin8_ref, qnw_ref, wqn_ref, wqr_ref,
                                  wqs_ref, kvnw_ref)
    ckv_out[0] = ckv
    kr_out[0] = kr
    kcat_out[0, :, 0:KV_LORA] = ckv.astype(BF16)
    kcat_out[0, :, KV_LORA:KV_LORA + C_ROPE] = kr.astype(BF16)
    kcat_out[0, :, KV_LORA + C_ROPE:KCAT] = kr.astype(BF16)
    for h in range(C_HEADS):
        ql = _mm(qn[:, h * C_NOPE:(h + 1) * C_NOPE].astype(BF16), wukt_ref[h]) * Q_SCALE
        q_out[0, h, :, 0:KV_LORA] = ql.astype(BF16)
        q_out[0, h, :, KV_LORA:KCAT] = _rope_slot(qr, h)


def _mla_heads_kernel(cq_ref, ckv_ref, g1_ref, g2_ref, cos8_ref, sin8_ref, qnw_ref, wqn_ref, wqr_ref, wqs_ref,
                      kvnw_ref, wk_ref, wv_ref, ckv_out, kr_out, q_out, k_out, v_out):
    ckv, kr, qn, qr = _mla_common(cq_ref, ckv_ref, g1_ref, g2_ref, cos8_ref, sin8_ref, qnw_ref, wqn_ref, wqr_ref,
                                  wqs_ref, kvnw_ref)
    ckv_out[0] = ckv
    kr_out[0] = kr
    ckvb = ckv.astype(BF16)
    kn = _mm(ckvb, wk_ref[...])
    vv = _mm(ckvb, wv_ref[...])
    krb = kr.astype(BF16)
    kr2 = jnp.concatenate([krb, krb], axis=1)
    for h in range(C_HEADS):
        hs = slice(h * C_NOPE, (h + 1) * C_NOPE)
        q_out[0, h, :, 0:C_NOPE] = (qn[:, hs] * Q_SCALE).astype(BF16)
        q_out[0, h, :, C_NOPE:QK_DIM] = _rope_slot(qr, h)
        k_out[0, h, :, 0:C_NOPE] = kn[:, hs].astype(BF16)
        k_out[0, h, :, C_NOPE:QK_DIM] = kr2
        v_out[0, h] = vv[:, h * C_VDIM:(h + 1) * C_VDIM].astype(BF16)


def _mla_proj(proj, cos8, sin8, q_norm_w, wqn, wqr, wqs, kv_norm_w, head_w, absorbed):
    bsz, L, _ = proj.shape
    tm = min(L, 256)
    const2 = lambda b, i: (0, 0)
    head_out = lambda width: (jax.ShapeDtypeStruct((bsz, C_HEADS, L, width), BF16),
                              pl.BlockSpec((1, C_HEADS, tm, width), lambda b, i: (b, 0, i, 0)))
    outs = [(jax.ShapeDtypeStruct((bsz, L, KV_LORA), F32), pl.BlockSpec((1, tm, KV_LORA), lambda b, i: (b, i, 0))),
            (jax.ShapeDtypeStruct((bsz, L, C_ROPE), F32), pl.BlockSpec((1, tm, C_ROPE), lambda b, i: (b, i, 0)))]
    if absorbed:
        body = _mla_proj_kernel
        w_specs = [pl.BlockSpec((C_HEADS, C_NOPE, KV_LORA), lambda b, i: (0, 0, 0))]
        outs += [(jax.ShapeDtypeStruct((bsz, L, KCAT), BF16), pl.BlockSpec((1, tm, KCAT), lambda b, i: (b, i, 0))),
                 head_out(KCAT)]
    else:
        body = _mla_heads_kernel
        w_specs = [pl.BlockSpec((KV_LORA, C_HEADS * C_NOPE), const2), pl.BlockSpec((KV_LORA, C_HEADS * C_VDIM), const2)]
        outs += [head_out(QK_DIM), head_out(QK_DIM), head_out(C_VDIM)]
    return pl.pallas_call(
        body,
        out_shape=[o[0] for o in outs],
        grid=(bsz, L // tm),
        in_specs=[pl.BlockSpec((1, tm, Q_LORA), lambda b, i: (b, i, P_CQ // Q_LORA)),
                  pl.BlockSpec((1, tm, KV_LORA), lambda b, i: (b, i, P_CKV // KV_LORA)),
                  pl.BlockSpec((1, tm, LANE), lambda b, i: (b, i, P_G1 // LANE)),
                  pl.BlockSpec((1, tm, LANE), lambda b, i: (b, i, P_G2 // LANE)),
                  pl.BlockSpec((tm, C_HEADS * C_ROPE), lambda b, i: (i, 0)),
                  pl.BlockSpec((tm, C_HEADS * C_ROPE), lambda b, i: (i, 0)),
                  pl.BlockSpec((1, Q_LORA), const2),
                  pl.BlockSpec((Q_LORA, C_HEADS * C_NOPE), const2),
                  pl.BlockSpec((Q_LORA, C_HEADS * C_ROPE), const2),
                  pl.BlockSpec((Q_LORA, C_HEADS * C_ROPE), const2),
                  pl.BlockSpec((1, KV_LORA), const2)] + w_specs,
        out_specs=[o[1] for o in outs],
        compiler_params=_params(("arbitrary", "arbitrary")),
        name="mla_proj" if absorbed else "mla_heads",
    )(proj, proj, proj, proj, cos8, sin8, q_norm_w.reshape(1, Q_LORA), wqn, wqr, wqs,
      kv_norm_w.reshape(1, KV_LORA), *head_w)


def _attn_heads_kernel(q_ref, k_ref, v_ref, yc_ref, m_scr, l_scr, acc_scr, *, tq, tk, hps):
    qi = pl.program_id(2)
    rows = hps * tq
    m_scr[...] = jnp.full((rows, LANE), -1e30, F32)
    l_scr[...] = jnp.zeros((rows, LANE), F32)
    acc_scr[...] = jnp.zeros((rows, C_VDIM), F32)

    def block(kj, masked):
        start = pl.multiple_of(kj * tk, tk)

        def scores(g):
            return _mm_nt(q_ref[0, g], k_ref[0, g, pl.ds(start, tk), :])

        s_next = scores(0)
        for g in range(hps):
            rs = slice(g * tq, (g + 1) * tq)
            s = s_next
            if g + 1 < hps:
                s_next = scores(g + 1)
            if masked:
                ri = lax.broadcasted_iota(jnp.int32, (tq, tk), 0)
                ci = lax.broadcasted_iota(jnp.int32, (tq, tk), 1)
                ok = ((qi * tq + ri) // CHUNK) >= ((start + ci) // CHUNK)
                s = jnp.where(ok, s, -jnp.inf)
            m_prev = m_scr[rs]
            m_new = jnp.maximum(m_prev, jnp.max(s, axis=-1, keepdims=True))
            alpha = jnp.exp2(m_prev - m_new)
            p = jnp.exp2(s - _lane_tile(m_new, tk // LANE))
            l_scr[rs] = alpha * l_scr[rs] + jnp.sum(p, axis=-1, keepdims=True)
            acc_scr[rs] = acc_scr[rs] * alpha + _mm(p.astype(BF16), v_ref[0, g, pl.ds(start, tk), :])
            m_scr[rs] = m_new

    def body(kj, carry):
        block(kj, False)
        return carry

    nfull = (qi * tq) // tk
    lax.fori_loop(0, nfull, body, 0)
    block(nfull, True)
    for g in range(hps):
        rs = slice(g * tq, (g + 1) * tq)
        yc_ref[0, :, g * C_VDIM:(g + 1) * C_VDIM] = (acc_scr[rs] * (1.0 / l_scr[rs])).astype(BF16)


def _attn_heads(q, k, v):
    bsz, _, L, _ = q.shape
    tq = min(L, 512)
    tk = min(L, 512)
    hps = 4
    rows = hps * tq
    return pl.pallas_call(
        functools.partial(_attn_heads_kernel, tq=tq, tk=tk, hps=hps),
        out_shape=jax.ShapeDtypeStruct((bsz, L, C_WIDTH), BF16),
        grid=(bsz, C_HEADS // hps, L // tq),
        in_specs=[pl.BlockSpec((1, hps, tq, QK_DIM), lambda b, g, i: (b, g, i, 0)),
                  pl.BlockSpec((1, hps, L, QK_DIM), lambda b, g, i: (b, g, 0, 0)),
                  pl.BlockSpec((1, hps, L, C_VDIM), lambda b, g, i: (b, g, 0, 0))],
        out_specs=pl.BlockSpec((1, tq, hps * C_VDIM), lambda b, g, i: (b, i, g)),
        scratch_shapes=[pltpu.VMEM((rows, LANE), F32), pltpu.VMEM((rows, LANE), F32),
                        pltpu.VMEM((rows, C_VDIM), F32)],
        compiler_params=_params(("arbitrary", "arbitrary", "arbitrary")),
        name="attn_heads",
    )(q, k, v)


def _attn_finish(acc, l, wuv_ref, yc_ref, tq):
    o = acc * (1.0 / l)
    for h in range(C_HEADS):
        oh = o[h * tq:(h + 1) * tq].astype(BF16)
        yc_ref[0, :, h * C_VDIM:(h + 1) * C_VDIM] = _mm(oh, wuv_ref[h]).astype(BF16)


def _attn_full_kernel(q_ref, k_ref, wuv_ref, yc_ref, *, tq, nkeys):
    rows = C_HEADS * tq
    q = q_ref[0].reshape(rows, KCAT)
    k = k_ref[0]
    s = _mm_nt(q, k)
    ci = lax.broadcasted_iota(jnp.int32, s.shape, 1)
    s = jnp.where(ci < nkeys, s, -jnp.inf)
    m = jnp.max(s, axis=-1, keepdims=True)
    p = jnp.exp2(s - m)
    l = jnp.sum(p, axis=-1, keepdims=True)
    acc = _mm(p.astype(BF16), k[:, 0:KV_LORA])
    _attn_finish(acc, l, wuv_ref, yc_ref, tq)


def _attn_full(q, kcat, wuv, nkeys):
    bsz, _, tq, _ = q.shape
    npad = kcat.shape[1]
    return pl.pallas_call(
        functools.partial(_attn_full_kernel, tq=tq, nkeys=nkeys),
        out_shape=jax.ShapeDtypeStruct((bsz, tq, C_WIDTH), BF16),
        grid=(bsz,),
        in_specs=[pl.BlockSpec((1, C_HEADS, tq, KCAT), lambda b: (b, 0, 0, 0)),
                  pl.BlockSpec((1, npad, KCAT), lambda b: (b, 0, 0)),
                  pl.BlockSpec((C_HEADS, KV_LORA, C_VDIM), lambda b: (0, 0, 0))],
        out_specs=pl.BlockSpec((1, tq, C_WIDTH), lambda b: (b, 0, 0)),
        compiler_params=_params(("arbitrary",)),
        name="attn_full",
    )(q, kcat, wuv)


def _outproj_kernel(x_ref, ya_ref, yb_ref, yc_ref, wo_ref, o_ref):
    tn = 512
    for j in range(0, D_MODEL, tn):
        acc = _mm(ya_ref[...], wo_ref[0:A_WIDTH, j:j + tn])
        acc = acc + _mm(yb_ref[...], wo_ref[A_WIDTH:A_WIDTH + B_WIDTH, j:j + tn])
        acc = acc + _mm(yc_ref[...], wo_ref[A_WIDTH + B_WIDTH:D_MODEL, j:j + tn])
        o_ref[:, j:j + tn] = x_ref[:, j:j + tn] + acc


def _outproj(x, ya, yb, yc, wo):
    m = x.shape[0]
    tm = min(m, 256)
    row = lambda i: (i, 0)
    return pl.pallas_call(
        _outproj_kernel,
        out_shape=jax.ShapeDtypeStruct((m, D_MODEL), F32),
        grid=(m // tm,),
        in_specs=[pl.BlockSpec((tm, D_MODEL), row),
                  pl.BlockSpec((tm, A_WIDTH), row),
                  pl.BlockSpec((tm, B_WIDTH), row),
                  pl.BlockSpec((tm, C_WIDTH), row),
                  pl.BlockSpec((D_MODEL, D_MODEL), lambda i: (0, 0))],
        out_specs=pl.BlockSpec((tm, D_MODEL), row),
        compiler_params=_params(("arbitrary",)),
        name="outproj",
    )(x, ya, yb, yc, wo)


def _ffn_kernel(x_ref, nw_ref, wu_ref, wd_ref, fw_ref, o_ref, xn_scr, acc_scr, *, final_norm):
    f = pl.program_id(1)

    @pl.when(f == 0)
    def _():
        x = x_ref[...]
        ms = jnp.mean(x * x, axis=-1, keepdims=True)
        xn_scr[...] = (x * lax.rsqrt(ms + EPS) * nw_ref[...]).astype(BF16)
        acc_scr[...] = jnp.zeros_like(acc_scr)

    hid = jnp.maximum(_mm(xn_scr[...], wu_ref[...]), 0.0)
    acc_scr[...] += _mm((hid * hid).astype(BF16), wd_ref[...])

    @pl.when(f == pl.num_programs(1) - 1)
    def _():
        y = x_ref[...] + acc_scr[...]
        if final_norm:
            ms = jnp.mean(y * y, axis=-1, keepdims=True)
            y = y * lax.rsqrt(ms + EPS) * fw_ref[...]
        o_ref[...] = y


def _ffn(x, norm_w, wu, wd, final_w, final_norm):
    m = x.shape[0]
    tm = min(m, 512)
    tf = 1024
    return pl.pallas_call(
        functools.partial(_ffn_kernel, final_norm=final_norm),
        out_shape=jax.ShapeDtypeStruct((m, D_MODEL), F32),
        grid=(m // tm, D_FF // tf),
        in_specs=[pl.BlockSpec((tm, D_MODEL), lambda i, f: (i, 0)),
                  pl.BlockSpec((1, D_MODEL), lambda i, f: (0, 0)),
                  pl.BlockSpec((D_MODEL, tf), lambda i, f: (0, f)),
                  pl.BlockSpec((tf, D_MODEL), lambda i, f: (f, 0)),
                  pl.BlockSpec((1, D_MODEL), lambda i, f: (0, 0))],
        out_specs=pl.BlockSpec((tm, D_MODEL), lambda i, f: (i, 0)),
        scratch_shapes=[pltpu.VMEM((tm, D_MODEL), BF16), pltpu.VMEM((tm, D_MODEL), F32)],
        compiler_params=_params(("arbitrary", "arbitrary")),
        name="ffn",
    )(x, norm_w.reshape(1, D_MODEL), wu, wd, final_w.reshape(1, D_MODEL))


def _swap_halves(w):
    half = C_ROPE // 2
    return jnp.concatenate([w[..., half:], w[..., :half]], axis=-1)


def _prep_w_in(w_in):
    k = w_in.shape[0]
    ckr = w_in[:, OFF_CKR:IN_COLS]
    g1 = jnp.concatenate([ckr, w_in[:, OFF_BBETA:OFF_CQ], jnp.zeros((k, LANE - C_ROPE - 2 * B_HEADS), F32)], axis=1)
    g2 = jnp.concatenate([_swap_halves(ckr), jnp.zeros((k, LANE - C_ROPE), F32)], axis=1)
    parts = [w_in[:, OFF_BQKV:OFF_BZ], w_in[:, OFF_AU:OFF_AV], w_in[:, OFF_AV:OFF_BQKV], w_in[:, OFF_BZ:OFF_BBETA],
             w_in[:, OFF_CQ:OFF_CKV], w_in[:, OFF_CKV:OFF_CKR], g1, g2]
    return jnp.concatenate(parts, axis=1).astype(BF16)


def _rope_tables(pos):
    half = C_ROPE // 2
    inv = ROPE_THETA ** (-jnp.arange(half, dtype=F32) / half)
    ang = pos.astype(F32)[:, None] * inv[None, :]
    cos, sin = jnp.cos(ang), jnp.sin(ang)
    cos8 = jnp.tile(jnp.concatenate([cos, cos], axis=1), (1, C_HEADS))
    sin8 = jnp.tile(jnp.concatenate([-sin, sin], axis=1), (1, C_HEADS))
    return cos8, sin8


def _layer(x, bsz, L, tabs, lw, past, final_w, final_norm):
    proj = _inproj(x, lw['attn_norm_w'], lw['w_in']).reshape(bsz, L, P_COLS)

    ya, va = _mixer_a(proj, lw['a_ln_w'], lw['a_ln_b'], lw['a_w_s'], lw['a_b_s'], past is not None)

    if past is None:
        rows, chunk = 2 * CHUNK, CHUNK
        buf8 = jnp.zeros((bsz, 8, B_QKV), F32)
        s0 = jnp.zeros((bsz, B_HEADS, B_DK, B_DV), F32)
    else:
        rows, chunk = L, L
        ckv_past, kr_past, s0, conv_buf = past
        buf8 = jnp.concatenate([jnp.zeros((bsz, 8 - (B_CONV - 1), B_QKV), F32), conv_buf], axis=1)
    prep = _gdn_prep(proj, buf8, lw['b_conv_w'], lw['b_a_log'], lw['b_dt_bias'], rows, chunk)
    yb, s_new = _gdn_scan(prep, proj, lw['b_norm_w'], s0, rows, chunk)
    buf_new = proj[:, L - (B_CONV - 1):, P_BQKV:P_BQKV + B_QKV]

    cos8, sin8 = tabs
    mla_args = (proj, cos8, sin8, lw['c_q_norm_w'], lw['wqn'], lw['wqr'], lw['wqs'], lw['c_kv_norm_w'])
    if past is None:
        ckv, kr, q, k, v = _mla_proj(*mla_args, (lw['wk'], lw['wv']), absorbed=False)
        yc = _attn_heads(q, k, v)
    else:
        ckv, kr, kcat, q = _mla_proj(*mla_args, (lw['wukt'],), absorbed=True)
        past_len = ckv_past.shape[1]
        nkeys = past_len + L
        npad = -(-nkeys // LANE) * LANE
        kpast = jnp.concatenate([ckv_past, kr_past, kr_past], axis=-1).astype(BF16)
        kall = jnp.concatenate([kpast, kcat, jnp.zeros((bsz, npad - nkeys, KCAT), BF16)], axis=1)
        yc = _attn_full(q, kall, lw['wuv'], nkeys)

    m = bsz * L
    x = _outproj(x, ya.reshape(m, A_WIDTH), yb.reshape(m, B_WIDTH), yc.reshape(m, C_WIDTH), lw['w_out'])
    x = _ffn(x, lw['mlp_norm_w'], lw['w_up'], lw['w_down'], final_w, final_norm)
    return x, (ckv, kr, s_new, buf_new, va)


def kernel(x_prompt, x_sample, cache_mla_ckv, cache_mla_krope, state_gdn, state_gdn_conv,
           attn_norm_w, w_in, a_ln_w, a_ln_b, a_w_s, a_b_s, b_conv_w, b_a_log, b_dt_bias, b_norm_w,
           c_q_norm_w, c_w_uq, c_kv_norm_w, c_w_uk, c_w_uv, w_out, mlp_norm_w, w_up, w_down, final_norm_w):
    depth = w_in.shape[0]
    bp, lp, _ = x_prompt.shape
    bs, ls, _ = x_sample.shape
    past_len = cache_mla_ckv.shape[2]
    tabs_p = _rope_tables(jnp.arange(lp, dtype=jnp.int32))
    tabs_s = _rope_tables(past_len + jnp.arange(ls, dtype=jnp.int32))
    hp = x_prompt.reshape(bp * lp, D_MODEL)
    hs = x_sample.reshape(bs * ls, D_MODEL)
    sp_list, ss_list = [], []
    for l in range(depth):
        wq_rope = c_w_uq[l][:, :, C_NOPE:]
        lw = {
            'attn_norm_w': attn_norm_w[l], 'w_in': _prep_w_in(w_in[l]),
            'a_ln_w': a_ln_w[l], 'a_ln_b': a_ln_b[l], 'a_w_s': a_w_s[l], 'a_b_s': a_b_s[l],
            'b_conv_w': b_conv_w[l], 'b_a_log': b_a_log[l], 'b_dt_bias': b_dt_bias[l], 'b_norm_w': b_norm_w[l],
            'c_q_norm_w': c_q_norm_w[l], 'c_kv_norm_w': c_kv_norm_w[l],
            'wqn': c_w_uq[l][:, :, :C_NOPE].reshape(Q_LORA, C_HEADS * C_NOPE).astype(BF16),
            'wqr': wq_rope.reshape(Q_LORA, C_HEADS * C_ROPE).astype(BF16),
            'wqs': _swap_halves(wq_rope).reshape(Q_LORA, C_HEADS * C_ROPE).astype(BF16),
            'wk': c_w_uk[l].reshape(KV_LORA, C_HEADS * C_NOPE).astype(BF16),
            'wv': c_w_uv[l].reshape(KV_LORA, C_HEADS * C_VDIM).astype(BF16),
            'wukt': jnp.transpose(c_w_uk[l], (1, 2, 0)).astype(BF16),
            'wuv': jnp.transpose(c_w_uv[l], (1, 0, 2)).astype(BF16),
            'w_out': w_out[l].astype(BF16), 'mlp_norm_w': mlp_norm_w[l],
            'w_up': w_up[l].astype(BF16), 'w_down': w_down[l].astype(BF16),
        }
        last = l == depth - 1
        hp, sp = _layer(hp, bp, lp, tabs_p, lw, None, final_norm_w, last)
        hs, ss = _layer(hs, bs, ls, tabs_s, lw,
                        (cache_mla_ckv[l], cache_mla_krope[l], state_gdn[l], state_gdn_conv[l]), final_norm_w, last)
        sp_list.append(sp)
        ss_list.append(ss)

    def stack(lst, i):
        return jnp.stack([s[i] for s in lst], axis=0)

    return (hp.reshape(bp, lp, D_MODEL), hs.reshape(bs, ls, D_MODEL),
            stack(sp_list, 0), stack(sp_list, 1), stack(sp_list, 2), stack(sp_list, 3),
            stack(ss_list, 0), stack(ss_list, 1), stack(ss_list, 2), stack(ss_list, 3), stack(ss_list, 4))
```

```python
import functools
import math

import jax
import jax.numpy as jnp
from jax import lax
from jax.experimental import pallas as pl
from jax.experimental.pallas import tpu as pltpu

F32 = jnp.float32
BF16 = jnp.bfloat16

D_MODEL = 2048
CHUNK = 64
EPS = 1e-6
LN_EPS = 1e-5
A_GROUPS = 4
A_HEAD = 128
A_WIDTH = 512
A_CHUNK = 128
B_HEADS = 4
B_DK = 128
B_DV = 128
B_WIDTH = 512
B_QKV = 1536
B_CONV = 4
C_HEADS = 8
C_NOPE = 128
C_ROPE = 64
C_VDIM = 128
C_WIDTH = 1024
Q_LORA = 512
KV_LORA = 256
ROPE_THETA = 10000.0
C_SCALE = (C_NOPE + C_ROPE) ** -0.5
Q_SCALE = C_SCALE * math.log2(math.e)
D_FF = 4 * D_MODEL

OFF_AU = 0
OFF_AV = OFF_AU + A_WIDTH
OFF_BQKV = OFF_AV + A_WIDTH
OFF_BZ = OFF_BQKV + B_QKV
OFF_BBETA = OFF_BZ + B_WIDTH
OFF_BALPHA = OFF_BBETA + B_HEADS
OFF_CQ = OFF_BALPHA + B_HEADS
OFF_CKV = OFF_CQ + Q_LORA
OFF_CKR = OFF_CKV + KV_LORA
IN_COLS = OFF_CKR + C_ROPE

P_BQKV = 0
P_AU = 1536
P_AV = 2048
P_BZ = 2560
P_CQ = 3072
P_CKV = 3584
P_G1 = 3840
P_G2 = 3968
P_COLS = 4096
LANE = 128
G1_BETA = C_ROPE
G1_ALPHA = C_ROPE + B_HEADS
KCAT = KV_LORA + 2 * C_ROPE
QK_DIM = C_NOPE + 2 * C_ROPE

VMEM_LIMIT = 56 * 1024 * 1024


def _mm(a, b):
    return jnp.dot(a, b, preferred_element_type=F32)


def _mm_nt(a, b):
    return lax.dot_general(a, b, (((1,), (1,)), ((), ())), preferred_element_type=F32)


def _split2(x):
    hi = x.astype(BF16)
    lo = (x - hi.astype(F32)).astype(BF16)
    return hi, lo


def _split3(x):
    hi = x.astype(BF16)
    r = x - hi.astype(F32)
    mid = r.astype(BF16)
    lo = (r - mid.astype(F32)).astype(BF16)
    return hi, mid, lo


def _cat3_lhs(a):
    ah, al = _split2(a)
    return jnp.concatenate([ah, ah, al], axis=1)


def _cat3_rhs(b):
    bh, bl = _split2(b)
    return jnp.concatenate([bh, bl, bh], axis=0)


def _mm3(a, b):
    return _mm(_cat3_lhs(a), _cat3_rhs(b))


def _mm_exact_lhs(a, b):
    return _mm(jnp.concatenate([a, a, a], axis=1), jnp.concatenate(_split3(b), axis=0))


def _mm_nt_exact_lhs(a, b):
    return _mm_nt(jnp.concatenate([a, a, a], axis=1), jnp.concatenate(_split3(b), axis=1))


def _lane_tile(x, n):
    return x if n == 1 else jnp.concatenate([x] * n, axis=1)


def _params(sem):
    return pltpu.CompilerParams(dimension_semantics=sem, vmem_limit_bytes=VMEM_LIMIT)


def _inproj_kernel(x_ref, nw_ref, w_ref, o_ref, xn_scr):
    @pl.when(pl.program_id(1) == 0)
    def _():
        x = x_ref[...]
        ms = jnp.mean(x * x, axis=-1, keepdims=True)
        xn_scr[...] = (x * lax.rsqrt(ms + EPS) * nw_ref[...]).astype(BF16)

    o_ref[...] = _mm(xn_scr[...], w_ref[...])


def _inproj(x, norm_w, w_pad):
    m = x.shape[0]
    tm = min(m, 512)
    tn = 1024
    return pl.pallas_call(
        _inproj_kernel,
        out_shape=jax.ShapeDtypeStruct((m, P_COLS), F32),
        grid=(m // tm, P_COLS // tn),
        in_specs=[pl.BlockSpec((tm, D_MODEL), lambda i, j: (i, 0)),
                  pl.BlockSpec((1, D_MODEL), lambda i, j: (0, 0)),
                  pl.BlockSpec((D_MODEL, tn), lambda i, j: (0, j))],
        out_specs=pl.BlockSpec((tm, tn), lambda i, j: (i, j)),
        scratch_shapes=[pltpu.VMEM((tm, D_MODEL), BF16)],
        compiler_params=_params(("arbitrary", "arbitrary")),
        name="inproj",
    )(x, norm_w.reshape(1, D_MODEL), w_pad)


def _gelu(x):
    return 0.5 * x * (1.0 + lax.erf(x * (2.0 ** -0.5)))


def _mixa_kernel(u_ref, v_ref, lnw_ref, lnb_ref, ws_ref, bs_ref, ya_ref, *va_ref, lc, nsub):
    u = _gelu(u_ref[0])
    v = _gelu(v_ref[0])
    mu = jnp.mean(v, axis=-1, keepdims=True)
    vc = v - mu
    var = jnp.mean(vc * vc, axis=-1, keepdims=True)
    vn = vc * lax.rsqrt(var + LN_EPS) * lnw_ref[...] + lnb_ref[...]
    if va_ref:
        va_ref[0][0] = vn
    for s in range(nsub):
        rows = slice(s * lc, (s + 1) * lc)
        for g in range(A_GROUPS):
            cols = slice(g * A_HEAD, (g + 1) * A_HEAD)
            f = _mm(ws_ref[g], vn[rows, cols].astype(BF16)) + bs_ref[g]
            ya_ref[0, rows, cols] = (u[rows, cols] * f).astype(BF16)


def _mixer_a(proj, ln_w, ln_b, w_s, b_s, want_va):
    bsz, L, _ = proj.shape
    lc = min(L, A_CHUNK)
    rows = min(L, 512)
    nsub = rows // lc
    idx = jnp.arange(lc)
    allowed = (idx[:, None] // CHUNK) >= (idx[None, :] // CHUNK)
    ws = jnp.where(allowed[None], w_s[:, :lc, :lc], 0.0).astype(BF16)
    bs = jnp.broadcast_to(b_s[:, :lc, None], (A_GROUPS, lc, A_HEAD)).astype(F32)
    out_shape = [jax.ShapeDtypeStruct((bsz, L, A_WIDTH), BF16)]
    out_specs = [pl.BlockSpec((1, rows, A_WIDTH), lambda b, i: (b, i, 0))]
    if want_va:
        out_shape.append(jax.ShapeDtypeStruct((bsz, L, A_WIDTH), F32))
        out_specs.append(pl.BlockSpec((1, rows, A_WIDTH), lambda b, i: (b, i, 0)))
    res = pl.pallas_call(
        functools.partial(_mixa_kernel, lc=lc, nsub=nsub),
        out_shape=out_shape,
        grid=(bsz, L // rows),
        in_specs=[pl.BlockSpec((1, rows, A_WIDTH), lambda b, i: (b, i, P_AU // A_WIDTH)),
                  pl.BlockSpec((1, rows, A_WIDTH), lambda b, i: (b, i, P_AV // A_WIDTH)),
                  pl.BlockSpec((1, A_WIDTH), lambda b, i: (0, 0)),
                  pl.BlockSpec((1, A_WIDTH), lambda b, i: (0, 0)),
                  pl.BlockSpec((A_GROUPS, lc, lc), lambda b, i: (0, 0, 0)),
                  pl.BlockSpec((A_GROUPS, lc, A_HEAD), lambda b, i: (0, 0, 0))],
        out_specs=out_specs,
        compiler_params=_params(("arbitrary", "arbitrary")),
        name="mixer_a",
    )(proj, proj, ln_w.reshape(1, A_WIDTH), ln_b.reshape(1, A_WIDTH), ws, bs)
    return res if want_va else (res[0], None)


def _softplus(x):
    return jnp.maximum(x, 0.0) + jnp.log1p(jnp.exp(-jnp.abs(x)))


def _gdn_prep_kernel(x_ref, halo_ref, buf_ref, g1_ref, cw_ref, alog_ref, dtb_ref,
                     u_ref, w_ref, qg_ref, kgt_ref, qk_ref, eg_ref, xp_scr, *, nb, rows, chunk):
    i = pl.program_id(1)
    nsub = rows // chunk
    ri = lax.broadcasted_iota(jnp.int32, (rows, rows), 0)
    ci = lax.broadcasted_iota(jnp.int32, (rows, rows), 1)
    same = (ri // chunk) == (ci // chunk)
    causal = same & (ri >= ci)
    strict = same & (ri > ci)
    tri = jnp.where(causal, 1.0, 0.0).astype(BF16)
    eye_r = jnp.where(ri == ci, 1.0, 0.0).astype(F32)
    li = lax.broadcasted_iota(jnp.int32, (LANE, LANE), 0)
    lj = lax.broadcasted_iota(jnp.int32, (LANE, LANE), 1)
    eye_l = jnp.where(li == lj, 1.0, 0.0).astype(BF16)

    ys, betas, gts = [], [], []
    for bb in range(nb):
        x = x_ref[bb]
        xp_scr[bb, 0:8, :] = jnp.where(i == 0, buf_ref[bb], halo_ref[bb])
        xp_scr[bb, 8:8 + rows, :] = x
        y = x * cw_ref[B_CONV - 1:B_CONV, :]
        for k in range(1, B_CONV):
            y = y + xp_scr[bb, 8 - k:8 - k + rows, :] * cw_ref[B_CONV - 1 - k:B_CONV - k, :]
        ys.append(y * jax.nn.sigmoid(y))
        g1 = g1_ref[bb]
        betas.append(jax.nn.sigmoid(g1))
        gts.append(-jnp.exp(alog_ref[...]) * _softplus(g1 + dtb_ref[...]))

    gcs = [_mm_exact_lhs(tri, g) for g in gts]
    gcts = [_mm_nt_exact_lhs(eye_l, gc) for gc in gcs]
    glasts = [jnp.concatenate(
        [jnp.broadcast_to(gc[(s + 1) * chunk - 1:(s + 1) * chunk, :], (chunk, LANE)) for s in range(nsub)], axis=0)
        for gc in gcs]

    pairs = [(bb, h) for bb in range(nb) for h in range(B_HEADS)]
    qs, ks, kbs, rhss, decs, egcs = [], [], [], [], [], []
    for bb, h in pairs:
        y, gc = ys[bb], gcs[bb]
        bcol = betas[bb][:, G1_BETA + h:G1_BETA + h + 1]
        gcol = gc[:, G1_ALPHA + h:G1_ALPHA + h + 1]
        grow = gcts[bb][G1_ALPHA + h:G1_ALPHA + h + 1, :]
        qh = y[:, h * B_DK:(h + 1) * B_DK]
        qh = qh * lax.rsqrt(jnp.sum(qh * qh, axis=-1, keepdims=True) + EPS) * (B_DK ** -0.5)
        kh = y[:, B_WIDTH + h * B_DK:B_WIDTH + (h + 1) * B_DK]
        kh = kh * lax.rsqrt(jnp.sum(kh * kh, axis=-1, keepdims=True) + EPS)
        vh = y[:, 2 * B_WIDTH + h * B_DV:2 * B_WIDTH + (h + 1) * B_DV]
        kb = kh * bcol
        egc = jnp.exp(gcol)
        qs.append(qh)
        ks.append(kh)
        kbs.append(kb)
        egcs.append(egc)
        rhss.append(jnp.concatenate([vh * bcol, kb * egc], axis=1))
        decs.append(jnp.where(causal, jnp.exp(jnp.minimum(gcol - grow, 0.0)), 0.0))

    khb = [k.astype(BF16) for k in ks]
    kk = [_mm_nt(kb.astype(BF16), kx) for kb, kx in zip(kbs, khb)]
    qkm = [_mm_nt(q.astype(BF16), kx) for q, kx in zip(qs, khb)]

    ps = [jnp.where(strict, -(a * d), 0.0) for a, d in zip(kk, decs)]
    ts = [eye_r + p for p in ps]
    ms = [_mm(_cat3_lhs(p), _cat3_rhs(p)) for p in ps]
    nsq = int(math.log2(chunk)) - 1
    for step in range(nsq):
        mr = [_cat3_rhs(m) for m in ms]
        upd = [_mm(_cat3_lhs(t), b) for t, b in zip(ts, mr)]
        if step + 1 < nsq:
            ms = [_mm(_cat3_lhs(m), b) for m, b in zip(ms, mr)]
        ts = [t + d for t, d in zip(ts, upd)]
    sols = [_mm3(t, r) for t, r in zip(ts, rhss)]

    for n, (bb, h) in enumerate(pairs):
        hs = slice(h * B_DK, (h + 1) * B_DK)
        gcol = gcs[bb][:, G1_ALPHA + h:G1_ALPHA + h + 1]
        glcol = glasts[bb][:, G1_ALPHA + h:G1_ALPHA + h + 1]
        u_ref[bb, :, hs] = sols[n][:, :B_DV]
        w_ref[bb, :, hs] = sols[n][:, B_DV:].astype(BF16)
        qg_ref[bb, :, hs] = (qs[n] * egcs[n]).astype(BF16)
        kg = (ks[n] * jnp.exp(glcol - gcol)).astype(BF16)
        kgt_ref[bb, hs, :] = _mm_nt(eye_l, kg).astype(BF16)
        qk_ref[bb, :, h * rows:(h + 1) * rows] = (qkm[n] * decs[n]).astype(BF16)
        for s in range(nsub):
            eg_ref[bb, s, h:h + 1, :] = jnp.broadcast_to(
                jnp.exp(glasts[bb][s * chunk:s * chunk + 1, G1_ALPHA + h:G1_ALPHA + h + 1]), (1, LANE))


def _gdn_prep(proj, buf8, conv_w, a_log, dt_bias, rows, chunk):
    bsz, L, _ = proj.shape
    nb = 2
    nblk = L // rows
    rpb = max(rows // 8, 1)
    alog = jnp.zeros((1, LANE), F32).at[0, G1_ALPHA:G1_ALPHA + B_HEADS].set(a_log)
    dtb = jnp.zeros((1, LANE), F32).at[0, G1_ALPHA:G1_ALPHA + B_HEADS].set(dt_bias)
    return pl.pallas_call(
        functools.partial(_gdn_prep_kernel, nb=nb, rows=rows, chunk=chunk),
        out_shape=[jax.ShapeDtypeStruct((bsz, L, B_WIDTH), F32),
                   jax.ShapeDtypeStruct((bsz, L, B_WIDTH), BF16),
                   jax.ShapeDtypeStruct((bsz, L, B_WIDTH), BF16),
                   jax.ShapeDtypeStruct((bsz, B_WIDTH, L), BF16),
                   jax.ShapeDtypeStruct((bsz, L, B_HEADS * rows), BF16),
                   jax.ShapeDtypeStruct((bsz, L // chunk, B_HEADS, LANE), F32)],
        grid=(bsz // nb, nblk),
        in_specs=[pl.BlockSpec((nb, rows, B_QKV), lambda b, i: (b, i, 0)),
                  pl.BlockSpec((nb, 8, B_QKV), lambda b, i: (b, jnp.maximum(i * rpb - 1, 0), 0)),
                  pl.BlockSpec((nb, 8, B_QKV), lambda b, i: (b, 0, 0)),
                  pl.BlockSpec((nb, rows, LANE), lambda b, i: (b, i, P_G1 // LANE)),
                  pl.BlockSpec((B_CONV, B_QKV), lambda b, i: (0, 0)),
                  pl.BlockSpec((1, LANE), lambda b, i: (0, 0)),
                  pl.BlockSpec((1, LANE), lambda b, i: (0, 0))],
        out_specs=[pl.BlockSpec((nb, rows, B_WIDTH), lambda b, i: (b, i, 0)),
                   pl.BlockSpec((nb, rows, B_WIDTH), lambda b, i: (b, i, 0)),
                   pl.BlockSpec((nb, rows, B_WIDTH), lambda b, i: (b, i, 0)),
                   pl.BlockSpec((nb, B_WIDTH, rows), lambda b, i: (b, 0, i)),
                   pl.BlockSpec((nb, rows, B_HEADS * rows), lambda b, i: (b, i, 0)),
                   pl.BlockSpec((nb, rows // chunk, B_HEADS, LANE), lambda b, i: (b, i, 0, 0))],
        scratch_shapes=[pltpu.VMEM((nb, rows + 8, B_QKV), F32)],
        compiler_params=_params(("arbitrary", "arbitrary")),
        name="gdn_prep",
    )(proj, proj, buf8, proj, conv_w, alog, dtb)


def _gdn_scan_kernel(u_ref, w_ref, qg_ref, kgt_ref, qk_ref, eg_ref, z_ref, nw_ref, s0_ref,
                     yb_ref, s_ref, *, bsz, rows, chunk):
    @pl.when(pl.program_id(0) == 0)
    def _():
        s_ref[...] = s0_ref[...]

    nsub = rows // chunk
    pairs = [(b, h) for b in range(bsz) for h in range(B_HEADS)]
    st = [s_ref[b, h] for b, h in pairs]
    for s in range(nsub):
        rs = slice(s * chunk, (s + 1) * chunk)
        r1 = [_mm(jnp.concatenate([w_ref[b, rs, h * B_DV:(h + 1) * B_DV], qg_ref[b, rs, h * B_DV:(h + 1) * B_DV]],
                                  axis=0), x.astype(BF16))
              for (b, h), x in zip(pairs, st)]
        vb = [(u_ref[b, rs, h * B_DV:(h + 1) * B_DV] - r[:chunk]).astype(BF16) for (b, h), r in zip(pairs, r1)]
        r2 = [_mm(jnp.concatenate([qk_ref[b, rs, h * rows + s * chunk:h * rows + (s + 1) * chunk],
                                   kgt_ref[b, h * B_DK:(h + 1) * B_DK, rs]], axis=0), v)
              for (b, h), v in zip(pairs, vb)]
        for n, (b, h) in enumerate(pairs):
            hs = slice(h * B_DV, (h + 1) * B_DV)
            o = r1[n][chunk:] + r2[n][:chunk]
            st[n] = st[n] * eg_ref[b, s, h:h + 1, :] + r2[n][chunk:]
            z = z_ref[b, rs, hs]
            on = o * lax.rsqrt(jnp.mean(o * o, axis=-1, keepdims=True) + EPS) * nw_ref[...]
            yb_ref[b, rs, hs] = (on * (z * jax.nn.sigmoid(z))).astype(BF16)
    for n, (b, h) in enumerate(pairs):
        s_ref[b, h] = st[n]


def _gdn_scan(prep, proj, norm_w, s0, rows, chunk):
    u, w, qg, kgt, qk, eg = prep
    bsz, L, _ = u.shape
    nblk = L // rows
    return pl.pallas_call(
        functools.partial(_gdn_scan_kernel, bsz=bsz, rows=rows, chunk=chunk),
        out_shape=[jax.ShapeDtypeStruct((bsz, L, B_WIDTH), BF16),
                   jax.ShapeDtypeStruct((bsz, B_HEADS, B_DK, B_DV), F32)],
        grid=(nblk,),
        in_specs=[pl.BlockSpec((bsz, rows, B_WIDTH), lambda i: (0, i, 0)),
                  pl.BlockSpec((bsz, rows, B_WIDTH), lambda i: (0, i, 0)),
                  pl.BlockSpec((bsz, rows, B_WIDTH), lambda i: (0, i, 0)),
                  pl.BlockSpec((bsz, B_WIDTH, rows), lambda i: (0, 0, i)),
                  pl.BlockSpec((bsz, rows, B_HEADS * rows), lambda i: (0, i, 0)),
                  pl.BlockSpec((bsz, rows // chunk, B_HEADS, LANE), lambda i: (0, i, 0, 0)),
                  pl.BlockSpec((bsz, rows, B_WIDTH), lambda i: (0, i, P_BZ // B_WIDTH)),
                  pl.BlockSpec((1, B_DV), lambda i: (0, 0)),
                  pl.BlockSpec((bsz, B_HEADS, B_DK, B_DV), lambda i: (0, 0, 0, 0))],
        out_specs=[pl.BlockSpec((bsz, rows, B_WIDTH), lambda i: (0, i, 0)),
                   pl.BlockSpec((bsz, B_HEADS, B_DK, B_DV), lambda i: (0, 0, 0, 0))],
        compiler_params=_params(("arbitrary",)),
        name="gdn_scan",
    )(u, w, qg, kgt, qk, eg, proj, norm_w.reshape(1, B_DV), s0)


def _mla_common(cq_ref, ckv_ref, g1_ref, g2_ref, cos8_ref, sin8_ref, qnw_ref, wqn_ref, wqr_ref, wqs_ref, kvnw_ref):
    cq = cq_ref[0]
    cqn = (cq * lax.rsqrt(jnp.mean(cq * cq, axis=-1, keepdims=True) + EPS) * qnw_ref[...]).astype(BF16)
    c = ckv_ref[0]
    ckv = c * lax.rsqrt(jnp.mean(c * c, axis=-1, keepdims=True) + EPS) * kvnw_ref[...]
    cos2, sin2 = cos8_ref[...], sin8_ref[...]
    kr = g1_ref[0][:, 0:C_ROPE] * cos2[:, 0:C_ROPE] + g2_ref[0][:, 0:C_ROPE] * sin2[:, 0:C_ROPE]
    qn = _mm(cqn, wqn_ref[...])
    ntile = C_HEADS * C_ROPE // LANE
    qr = (_mm(cqn, wqr_ref[...]) * _lane_tile(cos2, ntile) + _mm(cqn, wqs_ref[...]) * _lane_tile(sin2, ntile)) * Q_SCALE
    return ckv, kr, qn, qr


def _rope_slot(qr, h):
    lane = lax.broadcasted_iota(jnp.int32, (1, LANE), 1)
    pair = qr[:, (h // 2) * LANE:(h // 2 + 1) * LANE]
    return jnp.where((lane // C_ROPE) == (h % 2), pair, 0.0).astype(BF16)


def _mla_proj_kernel(cq_ref, ckv_ref, g1_ref, g2_ref, cos8_ref, sin8_ref, qnw_ref, wqn_ref, wqr_ref, wqs_ref,
                     kvnw_ref, wukt_ref, ckv_out, kr_out, kcat_out, q_out):
    ckv, kr, qn, qr = _mla_common(cq_ref, ckv_ref, g1_ref, g2_ref, cos8_ref, sin8_ref, qnw_ref, wqn_ref, wqr_ref,
                                  wqs_ref, kvnw_ref)
    ckv_out[...] = ckv
    kr_out[...] = kr
    kcat_out[0, :, 0:KV_LORA] = ckv.astype(BF16)
    kcat_out[0, :, KV_LORA:KV_LORA + C_ROPE] = kr.astype(BF16)
    kcat_out[0, :, KV_LORA + C_ROPE:KCAT] = kr.astype(BF16)
    for h in range(C_HEADS):
        ql = _mm(qn[:, h * C_NOPE:(h + 1) * C_NOPE].astype(BF16), wukt_ref[h]) * Q_SCALE
        q_out[0, h, :, 0:KV_LORA] = ql.astype(BF16)
        q_out[0, h, :, KV_LORA:KCAT] = _rope_slot(qr, h)


def _mla_heads_kernel(cq_ref, ckv_ref, g1_ref, g2_ref, cos8_ref, sin8_ref, qnw_ref, wqn_ref, wqr_ref, wqs_ref,
                      kvnw_ref, wk_ref, wv_ref, ckv_out, kr_out, q_out, k_out, kr2_out, v_out):
    ckv, kr, qn, qr = _mla_common(cq_ref, ckv_ref, g1_ref, g2_ref, cos8_ref, sin8_ref, qnw_ref, wqn_ref, wqr_ref,
                                  wqs_ref, kvnw_ref)
    ckv_out[...] = ckv
    kr_out[...] = kr
    ckvb = ckv.astype(BF16)
    kn = _mm(ckvb, wk_ref[...])
    vv = _mm(ckvb, wv_ref[...])
    krb = kr.astype(BF16)
    kr2_out[0] = jnp.concatenate([krb, krb], axis=1)
    for h in range(C_HEADS):
        hs = slice(h * C_NOPE, (h + 1) * C_NOPE)
        q_out[0, h, :, 0:C_NOPE] = (qn[:, hs] * Q_SCALE).astype(BF16)
        q_out[0, h, :, C_NOPE:QK_DIM] = _rope_slot(qr, h)
        k_out[0, h] = kn[:, hs].astype(BF16)
        v_out[0, h] = vv[:, h * C_VDIM:(h + 1) * C_VDIM].astype(BF16)


def _mla_proj(proj, cos8, sin8, q_norm_w, wqn, wqr, wqs, kv_norm_w, head_w, absorbed, layer, depth, stacks):
    bsz, L, _ = proj.shape
    tm = min(L, 256)
    const2 = lambda b, i: (0, 0)
    head_out = lambda width: (jax.ShapeDtypeStruct((bsz, C_HEADS, L, width), BF16),
                              pl.BlockSpec((1, C_HEADS, tm, width), lambda b, i: (b, 0, i, 0)))
    stack_out = lambda width: (jax.ShapeDtypeStruct((depth, bsz, L, width), F32),
                               pl.BlockSpec((None, None, tm, width), lambda b, i: (layer, b, i, 0)))
    outs = [stack_out(KV_LORA), stack_out(C_ROPE)]
    if absorbed:
        body = _mla_proj_kernel
        w_specs = [pl.BlockSpec((C_HEADS, C_NOPE, KV_LORA), lambda b, i: (0, 0, 0))]
        outs += [(jax.ShapeDtypeStruct((bsz, L, KCAT), BF16), pl.BlockSpec((1, tm, KCAT), lambda b, i: (b, i, 0))),
                 head_out(KCAT)]
    else:
        body = _mla_heads_kernel
        w_specs = [pl.BlockSpec((KV_LORA, C_HEADS * C_NOPE), const2), pl.BlockSpec((KV_LORA, C_HEADS * C_VDIM), const2)]
        outs += [head_out(QK_DIM), head_out(C_NOPE),
                 (jax.ShapeDtypeStruct((bsz, L, LANE), BF16), pl.BlockSpec((1, tm, LANE), lambda b, i: (b, i, 0))),
                 head_out(C_VDIM)]
    if stacks is None:
        kern, alias_specs, aliases, alias_args = body, [], {}, ()
    else:
        kern = lambda ckv_stack, kr_stack, *refs: body(*refs)
        alias_specs = [pl.BlockSpec(memory_space=pl.ANY), pl.BlockSpec(memory_space=pl.ANY)]
        aliases, alias_args = {0: 0, 1: 1}, tuple(stacks)
    return pl.pallas_call(
        kern,
        out_shape=[o[0] for o in outs],
        grid=(bsz, L // tm),
        input_output_aliases=aliases,
        in_specs=alias_specs + [pl.BlockSpec((1, tm, Q_LORA), lambda b, i: (b, i, P_CQ // Q_LORA)),
                  pl.BlockSpec((1, tm, KV_LORA), lambda b, i: (b, i, P_CKV // KV_LORA)),
                  pl.BlockSpec((1, tm, LANE), lambda b, i: (b, i, P_G1 // LANE)),
                  pl.BlockSpec((1, tm, LANE), lambda b, i: (b, i, P_G2 // LANE)),
                  pl.BlockSpec((tm, LANE), lambda b, i: (i, 0)),
                  pl.BlockSpec((tm, LANE), lambda b, i: (i, 0)),
                  pl.BlockSpec((1, Q_LORA), const2),
                  pl.BlockSpec((Q_LORA, C_HEADS * C_NOPE), const2),
                  pl.BlockSpec((Q_LORA, C_HEADS * C_ROPE), const2),
                  pl.BlockSpec((Q_LORA, C_HEADS * C_ROPE), const2),
                  pl.BlockSpec((1, KV_LORA), const2)] + w_specs,
        out_specs=[o[1] for o in outs],
        compiler_params=_params(("arbitrary", "arbitrary")),
        name="mla_proj" if absorbed else "mla_heads",
    )(*alias_args, proj, proj, proj, proj, cos8, sin8, q_norm_w.reshape(1, Q_LORA), wqn, wqr, wqs,
      kv_norm_w.reshape(1, KV_LORA), *head_w)


def _attn_heads_kernel(q_ref, k_ref, kr2_ref, v_ref, yc_ref, m_scr, l_scr, acc_scr, *, tq, tk, hps):
    qi = pl.program_id(2)
    rows = hps * tq
    m_scr[...] = jnp.full((rows, LANE), -1e30, F32)
    l_scr[...] = jnp.zeros((rows, LANE), F32)
    acc_scr[...] = jnp.zeros((rows, C_VDIM), F32)

    def block(kj, masked):
        start = pl.multiple_of(kj * tk, tk)
        kr2 = kr2_ref[0, pl.ds(start, tk), :]
        if masked:
            ri = lax.broadcasted_iota(jnp.int32, (tq, tk), 0)
            ci = lax.broadcasted_iota(jnp.int32, (tq, tk), 1)
            ok = ((qi * tq + ri) // CHUNK) >= ((start + ci) // CHUNK)

        def scores(g):
            return _mm_nt(q_ref[0, g], jnp.concatenate([k_ref[0, g, pl.ds(start, tk), :], kr2], axis=1))

        s_next = scores(0)
        for g in range(hps):
            rs = slice(g * tq, (g + 1) * tq)
            s = s_next
            if g + 1 < hps:
                s_next = scores(g + 1)
            if masked:
                s = jnp.where(ok, s, -jnp.inf)
            m_prev = m_scr[rs]
            m_new = jnp.maximum(m_prev, jnp.max(s, axis=-1, keepdims=True))
            alpha = jnp.exp2(m_prev - m_new)
            p = jnp.exp2(s - _lane_tile(m_new, tk // LANE))
            l_scr[rs] = alpha * l_scr[rs] + jnp.sum(p, axis=-1, keepdims=True)
            acc_scr[rs] = acc_scr[rs] * alpha + _mm(p.astype(BF16), v_ref[0, g, pl.ds(start, tk), :])
            m_scr[rs] = m_new

    def body(kj, carry):
        block(kj, False)
        return carry

    nfull = (qi * tq) // tk
    lax.fori_loop(0, nfull, body, 0)
    block(nfull, True)
    for g in range(hps):
        rs = slice(g * tq, (g + 1) * tq)
        yc_ref[0, :, g * C_VDIM:(g + 1) * C_VDIM] = (acc_scr[rs] * (1.0 / l_scr[rs])).astype(BF16)


def _attn_heads(q, k, kr2, v):
    bsz, _, L, _ = q.shape
    tq = min(L, 512)
    tk = min(L, 512)
    hps = 4
    rows = hps * tq
    return pl.pallas_call(
        functools.partial(_attn_heads_kernel, tq=tq, tk=tk, hps=hps),
        out_shape=jax.ShapeDtypeStruct((bsz, L, C_WIDTH), BF16),
        grid=(bsz, C_HEADS // hps, L // tq),
        in_specs=[pl.BlockSpec((1, hps, tq, QK_DIM), lambda b, g, i: (b, g, i, 0)),
                  pl.BlockSpec((1, hps, L, C_NOPE), lambda b, g, i: (b, g, 0, 0)),
                  pl.BlockSpec((1, L, LANE), lambda b, g, i: (b, 0, 0)),
                  pl.BlockSpec((1, hps, L, C_VDIM), lambda b, g, i: (b, g, 0, 0))],
        out_specs=pl.BlockSpec((1, tq, hps * C_VDIM), lambda b, g, i: (b, i, g)),
        scratch_shapes=[pltpu.VMEM((rows, LANE), F32), pltpu.VMEM((rows, LANE), F32),
                        pltpu.VMEM((rows, C_VDIM), F32)],
        compiler_params=_params(("arbitrary", "arbitrary", "arbitrary")),
        name="attn_heads",
    )(q, k, kr2, v)


def _attn_finish(acc, l, wuv_ref, yc_ref, tq):
    o = acc * (1.0 / l)
    for h in range(C_HEADS):
        oh = o[h * tq:(h + 1) * tq].astype(BF16)
        yc_ref[0, :, h * C_VDIM:(h + 1) * C_VDIM] = _mm(oh, wuv_ref[h]).astype(BF16)


def _attn_full_kernel(q_ref, k_ref, wuv_ref, yc_ref, *, tq, nkeys):
    rows = C_HEADS * tq
    q = q_ref[0].reshape(rows, KCAT)
    k = k_ref[0]
    s = _mm_nt(q, k)
    ci = lax.broadcasted_iota(jnp.int32, s.shape, 1)
    s = jnp.where(ci < nkeys, s, -jnp.inf)
    m = jnp.max(s, axis=-1, keepdims=True)
    p = jnp.exp2(s - m)
    l = jnp.sum(p, axis=-1, keepdims=True)
    acc = _mm(p.astype(BF16), k[:, 0:KV_LORA])
    _attn_finish(acc, l, wuv_ref, yc_ref, tq)


def _attn_full(q, kcat, wuv, nkeys):
    bsz, _, tq, _ = q.shape
    npad = kcat.shape[1]
    return pl.pallas_call(
        functools.partial(_attn_full_kernel, tq=tq, nkeys=nkeys),
        out_shape=jax.ShapeDtypeStruct((bsz, tq, C_WIDTH), BF16),
        grid=(bsz,),
        in_specs=[pl.BlockSpec((1, C_HEADS, tq, KCAT), lambda b: (b, 0, 0, 0)),
                  pl.BlockSpec((1, npad, KCAT), lambda b: (b, 0, 0)),
                  pl.BlockSpec((C_HEADS, KV_LORA, C_VDIM), lambda b: (0, 0, 0))],
        out_specs=pl.BlockSpec((1, tq, C_WIDTH), lambda b: (b, 0, 0)),
        compiler_params=_params(("arbitrary",)),
        name="attn_full",
    )(q, kcat, wuv)


def _outproj_kernel(x_ref, ya_ref, yb_ref, yc_ref, wo_ref, o_ref):
    tn = 512
    for j in range(0, D_MODEL, tn):
        acc = _mm(ya_ref[...], wo_ref[0:A_WIDTH, j:j + tn])
        acc = acc + _mm(yb_ref[...], wo_ref[A_WIDTH:A_WIDTH + B_WIDTH, j:j + tn])
        acc = acc + _mm(yc_ref[...], wo_ref[A_WIDTH + B_WIDTH:D_MODEL, j:j + tn])
        o_ref[:, j:j + tn] = x_ref[:, j:j + tn] + acc


def _outproj(x, ya, yb, yc, wo):
    m = x.shape[0]
    tm = min(m, 256)
    row = lambda i: (i, 0)
    return pl.pallas_call(
        _outproj_kernel,
        out_shape=jax.ShapeDtypeStruct((m, D_MODEL), F32),
        grid=(m // tm,),
        in_specs=[pl.BlockSpec((tm, D_MODEL), row),
                  pl.BlockSpec((tm, A_WIDTH), row),
                  pl.BlockSpec((tm, B_WIDTH), row),
                  pl.BlockSpec((tm, C_WIDTH), row),
                  pl.BlockSpec((D_MODEL, D_MODEL), lambda i: (0, 0))],
        out_specs=pl.BlockSpec((tm, D_MODEL), row),
        compiler_params=_params(("arbitrary",)),
        name="outproj",
    )(x, ya, yb, yc, wo)


def _ffn_kernel(x_ref, nw_ref, wu_ref, wd_ref, fw_ref, o_ref, xn_scr, acc_scr, *, final_norm):
    f = pl.program_id(1)

    @pl.when(f == 0)
    def _():
        x = x_ref[...]
        ms = jnp.mean(x * x, axis=-1, keepdims=True)
        xn_scr[...] = (x * lax.rsqrt(ms + EPS) * nw_ref[...]).astype(BF16)
        acc_scr[...] = jnp.zeros_like(acc_scr)

    hid = jnp.maximum(_mm(xn_scr[...], wu_ref[...]), 0.0)
    acc_scr[...] += _mm((hid * hid).astype(BF16), wd_ref[...])

    @pl.when(f == pl.num_programs(1) - 1)
    def _():
        y = x_ref[...] + acc_scr[...]
        if final_norm:
            ms = jnp.mean(y * y, axis=-1, keepdims=True)
            y = y * lax.rsqrt(ms + EPS) * fw_ref[...]
        o_ref[...] = y


def _ffn(x, norm_w, wu, wd, final_w, final_norm):
    m = x.shape[0]
    tm = min(m, 512)
    tf = 1024
    return pl.pallas_call(
        functools.partial(_ffn_kernel, final_norm=final_norm),
        out_shape=jax.ShapeDtypeStruct((m, D_MODEL), F32),
        grid=(m // tm, D_FF // tf),
        in_specs=[pl.BlockSpec((tm, D_MODEL), lambda i, f: (i, 0)),
                  pl.BlockSpec((1, D_MODEL), lambda i, f: (0, 0)),
                  pl.BlockSpec((D_MODEL, tf), lambda i, f: (0, f)),
                  pl.BlockSpec((tf, D_MODEL), lambda i, f: (f, 0)),
                  pl.BlockSpec((1, D_MODEL), lambda i, f: (0, 0))],
        out_specs=pl.BlockSpec((tm, D_MODEL), lambda i, f: (i, 0)),
        scratch_shapes=[pltpu.VMEM((tm, D_MODEL), BF16), pltpu.VMEM((tm, D_MODEL), F32)],
        compiler_params=_params(("arbitrary", "arbitrary")),
        name="ffn",
    )(x, norm_w.reshape(1, D_MODEL), wu, wd, final_w.reshape(1, D_MODEL))


def _swap_halves(w):
    half = C_ROPE // 2
    return jnp.concatenate([w[..., half:], w[..., :half]], axis=-1)


def _prep_w_in(w_in):
    k = w_in.shape[0]
    ckr = w_in[:, OFF_CKR:IN_COLS]
    g1 = jnp.concatenate([ckr, w_in[:, OFF_BBETA:OFF_CQ], jnp.zeros((k, LANE - C_ROPE - 2 * B_HEADS), BF16)], axis=1)
    g2 = jnp.concatenate([_swap_halves(ckr), jnp.zeros((k, LANE - C_ROPE), BF16)], axis=1)
    parts = [w_in[:, OFF_BQKV:OFF_BZ], w_in[:, OFF_AU:OFF_AV], w_in[:, OFF_AV:OFF_BQKV], w_in[:, OFF_BZ:OFF_BBETA],
             w_in[:, OFF_CQ:OFF_CKV], w_in[:, OFF_CKV:OFF_CKR], g1, g2]
    return jnp.concatenate(parts, axis=1)


def _rope_tables(pos):
    half = C_ROPE // 2
    inv = ROPE_THETA ** (-jnp.arange(half, dtype=F32) / half)
    ang = pos.astype(F32)[:, None] * inv[None, :]
    cos, sin = jnp.cos(ang), jnp.sin(ang)
    cos2 = jnp.concatenate([cos, cos, cos, cos], axis=1)
    sin2 = jnp.concatenate([-sin, sin, -sin, sin], axis=1)
    return cos2, sin2


def _layer(x, bsz, L, tabs, lw, past, final_w, final_norm, layer, depth, stacks):
    proj = _inproj(x, lw['attn_norm_w'], lw['w_in']).reshape(bsz, L, P_COLS)

    ya, va = _mixer_a(proj, lw['a_ln_w'], lw['a_ln_b'], lw['a_w_s'], lw['a_b_s'], past is not None)

    if past is None:
        rows, chunk = 2 * CHUNK, CHUNK
        buf8 = jnp.zeros((bsz, 8, B_QKV), F32)
        s0 = jnp.zeros((bsz, B_HEADS, B_DK, B_DV), F32)
    else:
        rows, chunk = L, L
        ckv_past, kr_past, s0, conv_buf = past
        buf8 = jnp.concatenate([jnp.zeros((bsz, 8 - (B_CONV - 1), B_QKV), F32), conv_buf], axis=1)
    prep = _gdn_prep(proj, buf8, lw['b_conv_w'], lw['b_a_log'], lw['b_dt_bias'], rows, chunk)
    yb, s_new = _gdn_scan(prep, proj, lw['b_norm_w'], s0, rows, chunk)
    buf_new = proj[:, L - (B_CONV - 1):, P_BQKV:P_BQKV + B_QKV]

    cos8, sin8 = tabs
    mla_args = (proj, cos8, sin8, lw['c_q_norm_w'], lw['wqn'], lw['wqr'], lw['wqs'], lw['c_kv_norm_w'])
    if past is None:
        ckv, kr, q, k, kr2, v = _mla_proj(*mla_args, (lw['wk'], lw['wv']), False, layer, depth, stacks)
        yc = _attn_heads(q, k, kr2, v)
    else:
        ckv, kr, kcat, q = _mla_proj(*mla_args, (lw['wukt'],), True, layer, depth, stacks)
        past_len = ckv_past.shape[1]
        nkeys = past_len + L
        npad = -(-nkeys // LANE) * LANE
        kpast = jnp.concatenate([ckv_past, kr_past, kr_past], axis=-1).astype(BF16)
        kall = jnp.concatenate([kpast, kcat, jnp.zeros((bsz, npad - nkeys, KCAT), BF16)], axis=1)
        yc = _attn_full(q, kall, lw['wuv'], nkeys)

    m = bsz * L
    x = _outproj(x, ya.reshape(m, A_WIDTH), yb.reshape(m, B_WIDTH), yc.reshape(m, C_WIDTH), lw['w_out'])
    x = _ffn(x, lw['mlp_norm_w'], lw['w_up'], lw['w_down'], final_w, final_norm)
    return x, (ckv, kr, s_new, buf_new, va)


def kernel(x_prompt, x_sample, cache_mla_ckv, cache_mla_krope, state_gdn, state_gdn_conv,
           attn_norm_w, w_in, a_ln_w, a_ln_b, a_w_s, a_b_s, b_conv_w, b_a_log, b_dt_bias, b_norm_w,
           c_q_norm_w, c_w_uq, c_kv_norm_w, c_w_uk, c_w_uv, w_out, mlp_norm_w, w_up, w_down, final_norm_w):
    depth = w_in.shape[0]
    bp, lp, _ = x_prompt.shape
    bs, ls, _ = x_sample.shape
    past_len = cache_mla_ckv.shape[2]
    tabs_p = _rope_tables(jnp.arange(lp, dtype=jnp.int32))
    tabs_s = _rope_tables(past_len + jnp.arange(ls, dtype=jnp.int32))
    hp = x_prompt.reshape(bp * lp, D_MODEL)
    hs = x_sample.reshape(bs * ls, D_MODEL)
    sp_list, ss_list = [], []
    stacks_p = stacks_s = None
    w_in16 = lax.optimization_barrier(w_in.astype(BF16))
    for l in range(depth):
        wq_rope = c_w_uq[l][:, :, C_NOPE:]
        lw = {
            'attn_norm_w': attn_norm_w[l], 'w_in': _prep_w_in(w_in16[l]),
            'a_ln_w': a_ln_w[l], 'a_ln_b': a_ln_b[l], 'a_w_s': a_w_s[l], 'a_b_s': a_b_s[l],
            'b_conv_w': b_conv_w[l], 'b_a_log': b_a_log[l], 'b_dt_bias': b_dt_bias[l], 'b_norm_w': b_norm_w[l],
            'c_q_norm_w': c_q_norm_w[l], 'c_kv_norm_w': c_kv_norm_w[l],
            'wqn': c_w_uq[l][:, :, :C_NOPE].reshape(Q_LORA, C_HEADS * C_NOPE).astype(BF16),
            'wqr': wq_rope.reshape(Q_LORA, C_HEADS * C_ROPE).astype(BF16),
            'wqs': _swap_halves(wq_rope).reshape(Q_LORA, C_HEADS * C_ROPE).astype(BF16),
            'wk': c_w_uk[l].reshape(KV_LORA, C_HEADS * C_NOPE).astype(BF16),
            'wv': c_w_uv[l].reshape(KV_LORA, C_HEADS * C_VDIM).astype(BF16),
            'wukt': jnp.transpose(c_w_uk[l], (1, 2, 0)).astype(BF16),
            'wuv': jnp.transpose(c_w_uv[l], (1, 0, 2)).astype(BF16),
            'w_out': w_out[l].astype(BF16), 'mlp_norm_w': mlp_norm_w[l],
            'w_up': w_up[l].astype(BF16), 'w_down': w_down[l].astype(BF16),
        }
        last = l == depth - 1
        hp, sp = _layer(hp, bp, lp, tabs_p, lw, None, final_norm_w, last, l, depth, stacks_p)
        hs, ss = _layer(hs, bs, ls, tabs_s, lw,
                        (cache_mla_ckv[l], cache_mla_krope[l], state_gdn[l], state_gdn_conv[l]), final_norm_w, last,
                        l, depth, stacks_s)
        stacks_p, stacks_s = sp[:2], ss[:2]
        sp_list.append(sp)
        ss_list.append(ss)

    def stack(lst, i):
        return jnp.stack([s[i] for s in lst], axis=0)

    return (hp.reshape(bp, lp, D_MODEL), hs.reshape(bs, ls, D_MODEL),
            stacks_p[0], stacks_p[1], stack(sp_list, 2), stack(sp_list, 3),
            stacks_s[0], stacks_s[1], stack(ss_list, 2), stack(ss_list, 3), stack(ss_list, 4))
```

```python
import functools
import math

import jax
import jax.numpy as jnp
from jax import lax
from jax.experimental import pallas as pl
from jax.experimental.pallas import tpu as pltpu

F32 = jnp.float32
BF16 = jnp.bfloat16

D_MODEL = 2048
CHUNK = 64
EPS = 1e-6
LN_EPS = 1e-5
A_GROUPS = 4
A_HEAD = 128
A_WIDTH = 512
A_CHUNK = 128
B_HEADS = 4
B_DK = 128
B_DV = 128
B_WIDTH = 512
B_QKV = 1536
B_CONV = 4
C_HEADS = 8
C_NOPE = 128
C_ROPE = 64
C_VDIM = 128
C_WIDTH = 1024
Q_LORA = 512
KV_LORA = 256
ROPE_THETA = 10000.0
C_SCALE = (C_NOPE + C_ROPE) ** -0.5
Q_SCALE = C_SCALE * math.log2(math.e)
D_FF = 4 * D_MODEL

OFF_AU = 0
OFF_AV = OFF_AU + A_WIDTH
OFF_BQKV = OFF_AV + A_WIDTH
OFF_BZ = OFF_BQKV + B_QKV
OFF_BBETA = OFF_BZ + B_WIDTH
OFF_BALPHA = OFF_BBETA + B_HEADS
OFF_CQ = OFF_BALPHA + B_HEADS
OFF_CKV = OFF_CQ + Q_LORA
OFF_CKR = OFF_CKV + KV_LORA
IN_COLS = OFF_CKR + C_ROPE

P_BQKV = 0
P_AU = 1536
P_AV = 2048
P_BZ = 2560
P_CQ = 3072
P_CKV = 3584
P_G1 = 3840
P_G2 = 3968
P_COLS = 4096
LANE = 128
G1_BETA = C_ROPE
G1_ALPHA = C_ROPE + B_HEADS
KCAT = KV_LORA + 2 * C_ROPE
QK_DIM = C_NOPE + 2 * C_ROPE

VMEM_LIMIT = 56 * 1024 * 1024


def _mm(a, b):
    return jnp.dot(a, b, preferred_element_type=F32)


def _mm_nt(a, b):
    return lax.dot_general(a, b, (((1,), (1,)), ((), ())), preferred_element_type=F32)


def _split2(x):
    hi = x.astype(BF16)
    lo = (x - hi.astype(F32)).astype(BF16)
    return hi, lo


def _split3(x):
    hi = x.astype(BF16)
    r = x - hi.astype(F32)
    mid = r.astype(BF16)
    lo = (r - mid.astype(F32)).astype(BF16)
    return hi, mid, lo


def _cat3_lhs(a):
    ah, al = _split2(a)
    return jnp.concatenate([ah, ah, al], axis=1)


def _cat3_rhs(b):
    bh, bl = _split2(b)
    return jnp.concatenate([bh, bl, bh], axis=0)


def _mm3(a, b):
    return _mm(_cat3_lhs(a), _cat3_rhs(b))


def _mm_exact_lhs(a, b):
    return _mm(jnp.concatenate([a, a, a], axis=1), jnp.concatenate(_split3(b), axis=0))


def _mm_nt_exact_lhs(a, b):
    return _mm_nt(jnp.concatenate([a, a, a], axis=1), jnp.concatenate(_split3(b), axis=1))


def _lane_tile(x, n):
    return x if n == 1 else jnp.concatenate([x] * n, axis=1)


def _params(sem):
    return pltpu.CompilerParams(dimension_semantics=sem, vmem_limit_bytes=VMEM_LIMIT)


def _inproj_kernel(x_ref, nw_ref, w_ref, o_ref):
    x = x_ref[...]
    ms = jnp.mean(x * x, axis=-1, keepdims=True)
    xn = (x * lax.rsqrt(ms + EPS) * nw_ref[...]).astype(BF16)
    o_ref[...] = _mm(xn, w_ref[...])


def _inproj(x, norm_w, w_pad, layer):
    m = x.shape[0]
    tm = min(m, 512)
    tn = 1024
    return pl.pallas_call(
        _inproj_kernel,
        out_shape=jax.ShapeDtypeStruct((m, P_COLS), F32),
        grid=(P_COLS // tn, m // tm),
        in_specs=[pl.BlockSpec((tm, D_MODEL), lambda j, i: (i, 0)),
                  pl.BlockSpec((1, D_MODEL), lambda j, i: (0, 0)),
                  pl.BlockSpec((None, D_MODEL, tn), lambda j, i: (layer, 0, j))],
        out_specs=pl.BlockSpec((tm, tn), lambda j, i: (i, j)),
        compiler_params=_params(("arbitrary", "arbitrary")),
        name="inproj",
    )(x, norm_w.reshape(1, D_MODEL), w_pad)


def _gelu(x):
    return 0.5 * x * (1.0 + lax.erf(x * (2.0 ** -0.5)))


def _mixa_kernel(u_ref, v_ref, lnw_ref, lnb_ref, ws_ref, bs_ref, ya_ref, *va_ref, lc, nsub):
    u = _gelu(u_ref[0])
    v = _gelu(v_ref[0])
    mu = jnp.mean(v, axis=-1, keepdims=True)
    vc = v - mu
    var = jnp.mean(vc * vc, axis=-1, keepdims=True)
    vn = vc * lax.rsqrt(var + LN_EPS) * lnw_ref[...] + lnb_ref[...]
    if va_ref:
        va_ref[0][0] = vn
    for s in range(nsub):
        rows = slice(s * lc, (s + 1) * lc)
        for g in range(A_GROUPS):
            cols = slice(g * A_HEAD, (g + 1) * A_HEAD)
            f = _mm(ws_ref[g], vn[rows, cols].astype(BF16)) + bs_ref[g]
            ya_ref[0, rows, cols] = (u[rows, cols] * f).astype(BF16)


def _mixer_a(proj, ln_w, ln_b, w_s, b_s, want_va):
    bsz, L, _ = proj.shape
    lc = min(L, A_CHUNK)
    rows = min(L, 512)
    nsub = rows // lc
    idx = jnp.arange(lc)
    allowed = (idx[:, None] // CHUNK) >= (idx[None, :] // CHUNK)
    ws = jnp.where(allowed[None], w_s[:, :lc, :lc], 0.0).astype(BF16)
    bs = jnp.broadcast_to(b_s[:, :lc, None], (A_GROUPS, lc, A_HEAD)).astype(F32)
    out_shape = [jax.ShapeDtypeStruct((bsz, L, A_WIDTH), BF16)]
    out_specs = [pl.BlockSpec((1, rows, A_WIDTH), lambda b, i: (b, i, 0))]
    if want_va:
        out_shape.append(jax.ShapeDtypeStruct((bsz, L, A_WIDTH), F32))
        out_specs.append(pl.BlockSpec((1, rows, A_WIDTH), lambda b, i: (b, i, 0)))
    res = pl.pallas_call(
        functools.partial(_mixa_kernel, lc=lc, nsub=nsub),
        out_shape=out_shape,
        grid=(bsz, L // rows),
        in_specs=[pl.BlockSpec((1, rows, A_WIDTH), lambda b, i: (b, i, P_AU // A_WIDTH)),
                  pl.BlockSpec((1, rows, A_WIDTH), lambda b, i: (b, i, P_AV // A_WIDTH)),
                  pl.BlockSpec((1, A_WIDTH), lambda b, i: (0, 0)),
                  pl.BlockSpec((1, A_WIDTH), lambda b, i: (0, 0)),
                  pl.BlockSpec((A_GROUPS, lc, lc), lambda b, i: (0, 0, 0)),
                  pl.BlockSpec((A_GROUPS, lc, A_HEAD), lambda b, i: (0, 0, 0))],
        out_specs=out_specs,
        compiler_params=_params(("arbitrary", "arbitrary")),
        name="mixer_a",
    )(proj, proj, ln_w.reshape(1, A_WIDTH), ln_b.reshape(1, A_WIDTH), ws, bs)
    return res if want_va else (res[0], None)


def _softplus(x):
    return jnp.maximum(x, 0.0) + jnp.log1p(jnp.exp(-jnp.abs(x)))


def _gdn_prep_kernel(x_ref, halo_ref, buf_ref, g1_ref, cw_ref, alog_ref, dtb_ref,
                     u_ref, w_ref, qg_ref, kgt_ref, qk_ref, eg_ref, xp_scr, *, nb, rows, chunk):
    i = pl.program_id(1)
    nsub = rows // chunk
    ri = lax.broadcasted_iota(jnp.int32, (rows, rows), 0)
    ci = lax.broadcasted_iota(jnp.int32, (rows, rows), 1)
    same = (ri // chunk) == (ci // chunk)
    causal = same & (ri >= ci)
    strict = same & (ri > ci)
    tri = jnp.where(causal, 1.0, 0.0).astype(BF16)
    eye_r = jnp.where(ri == ci, 1.0, 0.0).astype(F32)
    li = lax.broadcasted_iota(jnp.int32, (LANE, LANE), 0)
    lj = lax.broadcasted_iota(jnp.int32, (LANE, LANE), 1)
    eye_l = jnp.where(li == lj, 1.0, 0.0).astype(BF16)

    ys, betas, gts = [], [], []
    for bb in range(nb):
        x = x_ref[bb]
        xp_scr[bb, 0:8, :] = jnp.where(i == 0, buf_ref[bb], halo_ref[bb])
        xp_scr[bb, 8:8 + rows, :] = x
        y = x * cw_ref[B_CONV - 1:B_CONV, :]
        for k in range(1, B_CONV):
            y = y + xp_scr[bb, 8 - k:8 - k + rows, :] * cw_ref[B_CONV - 1 - k:B_CONV - k, :]
        ys.append(y * jax.nn.sigmoid(y))
        g1 = g1_ref[bb]
        betas.append(jax.nn.sigmoid(g1))
        gts.append(-jnp.exp(alog_ref[...]) * _softplus(g1 + dtb_ref[...]))

    gcs = [_mm_exact_lhs(tri, g) for g in gts]
    gcts = [_mm_nt_exact_lhs(eye_l, gc) for gc in gcs]
    glasts = [jnp.concatenate(
        [jnp.broadcast_to(gc[(s + 1) * chunk - 1:(s + 1) * chunk, :], (chunk, LANE)) for s in range(nsub)], axis=0)
        for gc in gcs]

    pairs = [(bb, h) for bb in range(nb) for h in range(B_HEADS)]
    qs, ks, kbs, rhss, decs, egcs = [], [], [], [], [], []
    for bb, h in pairs:
        y, gc = ys[bb], gcs[bb]
        bcol = betas[bb][:, G1_BETA + h:G1_BETA + h + 1]
        gcol = gc[:, G1_ALPHA + h:G1_ALPHA + h + 1]
        grow = gcts[bb][G1_ALPHA + h:G1_ALPHA + h + 1, :]
        qh = y[:, h * B_DK:(h + 1) * B_DK]
        qh = qh * lax.rsqrt(jnp.sum(qh * qh, axis=-1, keepdims=True) + EPS) * (B_DK ** -0.5)
        kh = y[:, B_WIDTH + h * B_DK:B_WIDTH + (h + 1) * B_DK]
        kh = kh * lax.rsqrt(jnp.sum(kh * kh, axis=-1, keepdims=True) + EPS)
        vh = y[:, 2 * B_WIDTH + h * B_DV:2 * B_WIDTH + (h + 1) * B_DV]
        kb = kh * bcol
        egc = jnp.exp(gcol)
        qs.append(qh)
        ks.append(kh)
        kbs.append(kb)
        egcs.append(egc)
        rhss.append(jnp.concatenate([vh * bcol, kb * egc], axis=1))
        decs.append(jnp.where(causal, jnp.exp(jnp.minimum(gcol - grow, 0.0)), 0.0))

    khb = [k.astype(BF16) for k in ks]
    kk = [_mm_nt(kb.astype(BF16), kx) for kb, kx in zip(kbs, khb)]
    qkm = [_mm_nt(q.astype(BF16), kx) for q, kx in zip(qs, khb)]

    ps = [jnp.where(strict, -(a * d), 0.0) for a, d in zip(kk, decs)]
    ts = [eye_r + p for p in ps]
    ms = [_mm(_cat3_lhs(p), _cat3_rhs(p)) for p in ps]
    nsq = int(math.log2(chunk)) - 1
    for step in range(nsq):
        mr = [_cat3_rhs(m) for m in ms]
        upd = [_mm(_cat3_lhs(t), b) for t, b in zip(ts, mr)]
        if step + 1 < nsq:
            ms = [_mm(_cat3_lhs(m), b) for m, b in zip(ms, mr)]
        ts = [t + d for t, d in zip(ts, upd)]
    sols = [_mm3(t, r) for t, r in zip(ts, rhss)]

    for n, (bb, h) in enumerate(pairs):
        hs = slice(h * B_DK, (h + 1) * B_DK)
        gcol = gcs[bb][:, G1_ALPHA + h:G1_ALPHA + h + 1]
        glcol = glasts[bb][:, G1_ALPHA + h:G1_ALPHA + h + 1]
        u_ref[bb, :, hs] = sols[n][:, :B_DV]
        w_ref[bb, :, hs] = sols[n][:, B_DV:].astype(BF16)
        qg_ref[bb, :, hs] = (qs[n] * egcs[n]).astype(BF16)
        kg = (ks[n] * jnp.exp(glcol - gcol)).astype(BF16)
        kgt_ref[bb, hs, :] = _mm_nt(eye_l, kg).astype(BF16)
        qk_ref[bb, :, h * rows:(h + 1) * rows] = (qkm[n] * decs[n]).astype(BF16)
        for s in range(nsub):
            eg_ref[bb, s, h:h + 1, :] = jnp.broadcast_to(
                jnp.exp(glasts[bb][s * chunk:s * chunk + 1, G1_ALPHA + h:G1_ALPHA + h + 1]), (1, LANE))


def _gdn_prep(proj, buf8, conv_w, a_log, dt_bias, rows, chunk):
    bsz, L, _ = proj.shape
    nb = 2
    nblk = L // rows
    rpb = max(rows // 8, 1)
    alog = jnp.zeros((1, LANE), F32).at[0, G1_ALPHA:G1_ALPHA + B_HEADS].set(a_log)
    dtb = jnp.zeros((1, LANE), F32).at[0, G1_ALPHA:G1_ALPHA + B_HEADS].set(dt_bias)
    return pl.pallas_call(
        functools.partial(_gdn_prep_kernel, nb=nb, rows=rows, chunk=chunk),
        out_shape=[jax.ShapeDtypeStruct((bsz, L, B_WIDTH), F32),
                   jax.ShapeDtypeStruct((bsz, L, B_WIDTH), BF16),
                   jax.ShapeDtypeStruct((bsz, L, B_WIDTH), BF16),
                   jax.ShapeDtypeStruct((bsz, B_WIDTH, L), BF16),
                   jax.ShapeDtypeStruct((bsz, L, B_HEADS * rows), BF16),
                   jax.ShapeDtypeStruct((bsz, L // chunk, B_HEADS, LANE), F32)],
        grid=(bsz // nb, nblk),
        in_specs=[pl.BlockSpec((nb, rows, B_QKV), lambda b, i: (b, i, 0)),
                  pl.BlockSpec((nb, 8, B_QKV), lambda b, i: (b, jnp.maximum(i * rpb - 1, 0), 0)),
                  pl.BlockSpec((nb, 8, B_QKV), lambda b, i: (b, 0, 0)),
                  pl.BlockSpec((nb, rows, LANE), lambda b, i: (b, i, P_G1 // LANE)),
                  pl.BlockSpec((B_CONV, B_QKV), lambda b, i: (0, 0)),
                  pl.BlockSpec((1, LANE), lambda b, i: (0, 0)),
                  pl.BlockSpec((1, LANE), lambda b, i: (0, 0))],
        out_specs=[pl.BlockSpec((nb, rows, B_WIDTH), lambda b, i: (b, i, 0)),
                   pl.BlockSpec((nb, rows, B_WIDTH), lambda b, i: (b, i, 0)),
                   pl.BlockSpec((nb, rows, B_WIDTH), lambda b, i: (b, i, 0)),
                   pl.BlockSpec((nb, B_WIDTH, rows), lambda b, i: (b, 0, i)),
                   pl.BlockSpec((nb, rows, B_HEADS * rows), lambda b, i: (b, i, 0)),
                   pl.BlockSpec((nb, rows // chunk, B_HEADS, LANE), lambda b, i: (b, i, 0, 0))],
        scratch_shapes=[pltpu.VMEM((nb, rows + 8, B_QKV), F32)],
        compiler_params=_params(("arbitrary", "arbitrary")),
        name="gdn_prep",
    )(proj, proj, buf8, proj, conv_w, alog, dtb)


def _gdn_scan_kernel(u_ref, w_ref, qg_ref, kgt_ref, qk_ref, eg_ref, z_ref, nw_ref, s0_ref,
                     yb_ref, s_ref, *, bsz, rows, chunk):
    @pl.when(pl.program_id(0) == 0)
    def _():
        s_ref[...] = s0_ref[...]

    nsub = rows // chunk
    pairs = [(b, h) for b in range(bsz) for h in range(B_HEADS)]
    st = [s_ref[b, h] for b, h in pairs]
    for s in range(nsub):
        rs = slice(s * chunk, (s + 1) * chunk)
        r1 = [_mm(jnp.concatenate([w_ref[b, rs, h * B_DV:(h + 1) * B_DV], qg_ref[b, rs, h * B_DV:(h + 1) * B_DV]],
                                  axis=0), x.astype(BF16))
              for (b, h), x in zip(pairs, st)]
        vb = [(u_ref[b, rs, h * B_DV:(h + 1) * B_DV] - r[:chunk]).astype(BF16) for (b, h), r in zip(pairs, r1)]
        r2 = [_mm(jnp.concatenate([qk_ref[b, rs, h * rows + s * chunk:h * rows + (s + 1) * chunk],
                                   kgt_ref[b, h * B_DK:(h + 1) * B_DK, rs]], axis=0), v)
              for (b, h), v in zip(pairs, vb)]
        for n, (b, h) in enumerate(pairs):
            hs = slice(h * B_DV, (h + 1) * B_DV)
            o = r1[n][chunk:] + r2[n][:chunk]
            st[n] = st[n] * eg_ref[b, s, h:h + 1, :] + r2[n][chunk:]
            z = z_ref[b, rs, hs]
            on = o * lax.rsqrt(jnp.mean(o * o, axis=-1, keepdims=True) + EPS) * nw_ref[...]
            yb_ref[b, rs, hs] = (on * (z * jax.nn.sigmoid(z))).astype(BF16)
    for n, (b, h) in enumerate(pairs):
        s_ref[b, h] = st[n]


def _gdn_scan(prep, proj, norm_w, s0, rows, chunk):
    u, w, qg, kgt, qk, eg = prep
    bsz, L, _ = u.shape
    nblk = L // rows
    return pl.pallas_call(
        functools.partial(_gdn_scan_kernel, bsz=bsz, rows=rows, chunk=chunk),
        out_shape=[jax.ShapeDtypeStruct((bsz, L, B_WIDTH), BF16),
                   jax.ShapeDtypeStruct((bsz, B_HEADS, B_DK, B_DV), F32)],
        grid=(nblk,),
        in_specs=[pl.BlockSpec((bsz, rows, B_WIDTH), lambda i: (0, i, 0)),
                  pl.BlockSpec((bsz, rows, B_WIDTH), lambda i: (0, i, 0)),
                  pl.BlockSpec((bsz, rows, B_WIDTH), lambda i: (0, i, 0)),
                  pl.BlockSpec((bsz, B_WIDTH, rows), lambda i: (0, 0, i)),
                  pl.BlockSpec((bsz, rows, B_HEADS * rows), lambda i: (0, i, 0)),
                  pl.BlockSpec((bsz, rows // chunk, B_HEADS, LANE), lambda i: (0, i, 0, 0)),
                  pl.BlockSpec((bsz, rows, B_WIDTH), lambda i: (0, i, P_BZ // B_WIDTH)),
                  pl.BlockSpec((1, B_DV), lambda i: (0, 0)),
                  pl.BlockSpec((bsz, B_HEADS, B_DK, B_DV), lambda i: (0, 0, 0, 0))],
        out_specs=[pl.BlockSpec((bsz, rows, B_WIDTH), lambda i: (0, i, 0)),
                   pl.BlockSpec((bsz, B_HEADS, B_DK, B_DV), lambda i: (0, 0, 0, 0))],
        compiler_params=_params(("arbitrary",)),
        name="gdn_scan",
    )(u, w, qg, kgt, qk, eg, proj, norm_w.reshape(1, B_DV), s0)


def _mla_common(cq_ref, ckv_ref, g1_ref, g2_ref, cos8_ref, sin8_ref, qnw_ref, wqn_ref, wqr_ref, wqs_ref, kvnw_ref):
    cq = cq_ref[0]
    cqn = (cq * lax.rsqrt(jnp.mean(cq * cq, axis=-1, keepdims=True) + EPS) * qnw_ref[...]).astype(BF16)
    c = ckv_ref[0]
    ckv = c * lax.rsqrt(jnp.mean(c * c, axis=-1, keepdims=True) + EPS) * kvnw_ref[...]
    cos2, sin2 = cos8_ref[...], sin8_ref[...]
    kr = g1_ref[0][:, 0:C_ROPE] * cos2[:, 0:C_ROPE] + g2_ref[0][:, 0:C_ROPE] * sin2[:, 0:C_ROPE]
    qn = _mm(cqn, wqn_ref[...])
    ntile = C_HEADS * C_ROPE // LANE
    qr = (_mm(cqn, wqr_ref[...]) * _lane_tile(cos2, ntile) + _mm(cqn, wqs_ref[...]) * _lane_tile(sin2, ntile)) * Q_SCALE
    return ckv, kr, qn, qr


def _rope_slot(qr, h):
    lane = lax.broadcasted_iota(jnp.int32, (1, LANE), 1)
    pair = qr[:, (h // 2) * LANE:(h // 2 + 1) * LANE]
    return jnp.where((lane // C_ROPE) == (h % 2), pair, 0.0).astype(BF16)


def _mla_proj_kernel(cq_ref, ckv_ref, g1_ref, g2_ref, cos8_ref, sin8_ref, qnw_ref, wqn_ref, wqr_ref, wqs_ref,
                     kvnw_ref, wukt_ref, ckv_out, kr_out, kcat_out, q_out):
    ckv, kr, qn, qr = _mla_common(cq_ref, ckv_ref, g1_ref, g2_ref, cos8_ref, sin8_ref, qnw_ref, wqn_ref, wqr_ref,
                                  wqs_ref, kvnw_ref)
    ckv_out[...] = ckv
    kr_out[...] = kr
    kcat_out[0, :, 0:KV_LORA] = ckv.astype(BF16)
    kcat_out[0, :, KV_LORA:KV_LORA + C_ROPE] = kr.astype(BF16)
    kcat_out[0, :, KV_LORA + C_ROPE:KCAT] = kr.astype(BF16)
    for h in range(C_HEADS):
        ql = _mm(qn[:, h * C_NOPE:(h + 1) * C_NOPE].astype(BF16), wukt_ref[h]) * Q_SCALE
        q_out[0, h, :, 0:KV_LORA] = ql.astype(BF16)
        q_out[0, h, :, KV_LORA:KCAT] = _rope_slot(qr, h)


def _mla_heads_kernel(cq_ref, ckv_ref, g1_ref, g2_ref, cos8_ref, sin8_ref, qnw_ref, wqn_ref, wqr_ref, wqs_ref,
                      kvnw_ref, wk_ref, wv_ref, ckv_out, kr_out, q_out, k_out, kr2_out, v_out):
    ckv, kr, qn, qr = _mla_common(cq_ref, ckv_ref, g1_ref, g2_ref, cos8_ref, sin8_ref, qnw_ref, wqn_ref, wqr_ref,
                                  wqs_ref, kvnw_ref)
    ckv_out[...] = ckv
    kr_out[...] = kr
    ckvb = ckv.astype(BF16)
    kn = _mm(ckvb, wk_ref[...])
    vv = _mm(ckvb, wv_ref[...])
    krb = kr.astype(BF16)
    kr2_out[0] = jnp.concatenate([krb, krb], axis=1)
    for h in range(C_HEADS):
        hs = slice(h * C_NOPE, (h + 1) * C_NOPE)
        q_out[0, h, :, 0:C_NOPE] = (qn[:, hs] * Q_SCALE).astype(BF16)
        q_out[0, h, :, C_NOPE:QK_DIM] = _rope_slot(qr, h)
        k_out[0, h] = kn[:, hs].astype(BF16)
        v_out[0, h] = vv[:, h * C_VDIM:(h + 1) * C_VDIM].astype(BF16)


def _mla_proj(proj, cos8, sin8, q_norm_w, wqn, wqr, wqs, kv_norm_w, head_w, absorbed, layer, depth, stacks):
    bsz, L, _ = proj.shape
    tm = min(L, 256)
    const2 = lambda b, i: (0, 0)
    head_out = lambda width: (jax.ShapeDtypeStruct((bsz, C_HEADS, L, width), BF16),
                              pl.BlockSpec((1, C_HEADS, tm, width), lambda b, i: (b, 0, i, 0)))
    stack_out = lambda width: (jax.ShapeDtypeStruct((depth, bsz, L, width), F32),
                               pl.BlockSpec((None, None, tm, width), lambda b, i: (layer, b, i, 0)))
    outs = [stack_out(KV_LORA), stack_out(C_ROPE)]
    if absorbed:
        body = _mla_proj_kernel
        w_specs = [pl.BlockSpec((C_HEADS, C_NOPE, KV_LORA), lambda b, i: (0, 0, 0))]
        outs += [(jax.ShapeDtypeStruct((bsz, L, KCAT), BF16), pl.BlockSpec((1, tm, KCAT), lambda b, i: (b, i, 0))),
                 head_out(KCAT)]
    else:
        body = _mla_heads_kernel
        w_specs = [pl.BlockSpec((KV_LORA, C_HEADS * C_NOPE), const2), pl.BlockSpec((KV_LORA, C_HEADS * C_VDIM), const2)]
        outs += [head_out(QK_DIM), head_out(C_NOPE),
                 (jax.ShapeDtypeStruct((bsz, L, LANE), BF16), pl.BlockSpec((1, tm, LANE), lambda b, i: (b, i, 0))),
                 head_out(C_VDIM)]
    if stacks is None:
        kern, alias_specs, aliases, alias_args = body, [], {}, ()
    else:
        kern = lambda ckv_stack, kr_stack, *refs: body(*refs)
        alias_specs = [pl.BlockSpec(memory_space=pl.ANY), pl.BlockSpec(memory_space=pl.ANY)]
        aliases, alias_args = {0: 0, 1: 1}, tuple(stacks)
    return pl.pallas_call(
        kern,
        out_shape=[o[0] for o in outs],
        grid=(bsz, L // tm),
        input_output_aliases=aliases,
        in_specs=alias_specs + [pl.BlockSpec((1, tm, Q_LORA), lambda b, i: (b, i, P_CQ // Q_LORA)),
                  pl.BlockSpec((1, tm, KV_LORA), lambda b, i: (b, i, P_CKV // KV_LORA)),
                  pl.BlockSpec((1, tm, LANE), lambda b, i: (b, i, P_G1 // LANE)),
                  pl.BlockSpec((1, tm, LANE), lambda b, i: (b, i, P_G2 // LANE)),
                  pl.BlockSpec((tm, LANE), lambda b, i: (i, 0)),
                  pl.BlockSpec((tm, LANE), lambda b, i: (i, 0)),
                  pl.BlockSpec((1, Q_LORA), const2),
                  pl.BlockSpec((Q_LORA, C_HEADS * C_NOPE), const2),
                  pl.BlockSpec((Q_LORA, C_HEADS * C_ROPE), const2),
                  pl.BlockSpec((Q_LORA, C_HEADS * C_ROPE), const2),
                  pl.BlockSpec((1, KV_LORA), const2)] + w_specs,
        out_specs=[o[1] for o in outs],
        compiler_params=_params(("arbitrary", "arbitrary")),
        name="mla_proj" if absorbed else "mla_heads",
    )(*alias_args, proj, proj, proj, proj, cos8, sin8, q_norm_w.reshape(1, Q_LORA), wqn, wqr, wqs,
      kv_norm_w.reshape(1, KV_LORA), *head_w)


def _attn_heads_kernel(q_ref, k_ref, kr2_ref, v_ref, yc_ref, m_scr, l_scr, acc_scr, *, tq, tk, hps):
    qi = pl.program_id(2)
    rows = hps * tq
    m_scr[...] = jnp.full((rows, LANE), -1e30, F32)
    l_scr[...] = jnp.zeros((rows, LANE), F32)
    acc_scr[...] = jnp.zeros((rows, C_VDIM), F32)

    def block(kj, masked):
        start = pl.multiple_of(kj * tk, tk)
        kr2 = kr2_ref[0, pl.ds(start, tk), :]
        if masked:
            ri = lax.broadcasted_iota(jnp.int32, (tq, tk), 0)
            ci = lax.broadcasted_iota(jnp.int32, (tq, tk), 1)
            ok = ((qi * tq + ri) // CHUNK) >= ((start + ci) // CHUNK)

        def scores(g):
            return _mm_nt(q_ref[0, g], jnp.concatenate([k_ref[0, g, pl.ds(start, tk), :], kr2], axis=1))

        s_next = scores(0)
        for g in range(hps):
            rs = slice(g * tq, (g + 1) * tq)
            s = s_next
            if g + 1 < hps:
                s_next = scores(g + 1)
            if masked:
                s = jnp.where(ok, s, -jnp.inf)
            m_prev = m_scr[rs]
            m_new = jnp.maximum(m_prev, jnp.max(s, axis=-1, keepdims=True))
            alpha = jnp.exp2(m_prev - m_new)
            p = jnp.exp2(s - _lane_tile(m_new, tk // LANE))
            l_scr[rs] = alpha * l_scr[rs] + jnp.sum(p, axis=-1, keepdims=True)
            acc_scr[rs] = acc_scr[rs] * alpha + _mm(p.astype(BF16), v_ref[0, g, pl.ds(start, tk), :])
            m_scr[rs] = m_new

    def body(kj, carry):
        block(kj, False)
        return carry

    nfull = (qi * tq) // tk
    lax.fori_loop(0, nfull, body, 0)
    block(nfull, True)
    for g in range(hps):
        rs = slice(g * tq, (g + 1) * tq)
        yc_ref[0, :, g * C_VDIM:(g + 1) * C_VDIM] = (acc_scr[rs] * (1.0 / l_scr[rs])).astype(BF16)


def _attn_heads(q, k, kr2, v):
    bsz, _, L, _ = q.shape
    tq = min(L, 512)
    tk = min(L, 512)
    hps = 4
    rows = hps * tq
    return pl.pallas_call(
        functools.partial(_attn_heads_kernel, tq=tq, tk=tk, hps=hps),
        out_shape=jax.ShapeDtypeStruct((bsz, L, C_WIDTH), BF16),
        grid=(bsz, C_HEADS // hps, L // tq),
        in_specs=[pl.BlockSpec((1, hps, tq, QK_DIM), lambda b, g, i: (b, g, i, 0)),
                  pl.BlockSpec((1, hps, L, C_NOPE), lambda b, g, i: (b, g, 0, 0)),
                  pl.BlockSpec((1, L, LANE), lambda b, g, i: (b, 0, 0)),
                  pl.BlockSpec((1, hps, L, C_VDIM), lambda b, g, i: (b, g, 0, 0))],
        out_specs=pl.BlockSpec((1, tq, hps * C_VDIM), lambda b, g, i: (b, i, g)),
        scratch_shapes=[pltpu.VMEM((rows, LANE), F32), pltpu.VMEM((rows, LANE), F32),
                        pltpu.VMEM((rows, C_VDIM), F32)],
        compiler_params=_params(("arbitrary", "arbitrary", "arbitrary")),
        name="attn_heads",
    )(q, k, kr2, v)


def _attn_finish(acc, l, wuv_ref, yc_ref, tq):
    o = acc * (1.0 / l)
    for h in range(C_HEADS):
        oh = o[h * tq:(h + 1) * tq].astype(BF16)
        yc_ref[0, :, h * C_VDIM:(h + 1) * C_VDIM] = _mm(oh, wuv_ref[h]).astype(BF16)


def _attn_full_kernel(q_ref, k_ref, wuv_ref, yc_ref, *, tq, nkeys):
    rows = C_HEADS * tq
    q = q_ref[0].reshape(rows, KCAT)
    k = k_ref[0]
    s = _mm_nt(q, k)
    ci = lax.broadcasted_iota(jnp.int32, s.shape, 1)
    s = jnp.where(ci < nkeys, s, -jnp.inf)
    m = jnp.max(s, axis=-1, keepdims=True)
    p = jnp.exp2(s - m)
    l = jnp.sum(p, axis=-1, keepdims=True)
    acc = _mm(p.astype(BF16), k[:, 0:KV_LORA])
    _attn_finish(acc, l, wuv_ref, yc_ref, tq)


def _attn_full(q, kcat, wuv, nkeys):
    bsz, _, tq, _ = q.shape
    npad = kcat.shape[1]
    return pl.pallas_call(
        functools.partial(_attn_full_kernel, tq=tq, nkeys=nkeys),
        out_shape=jax.ShapeDtypeStruct((bsz, tq, C_WIDTH), BF16),
        grid=(bsz,),
        in_specs=[pl.BlockSpec((1, C_HEADS, tq, KCAT), lambda b: (b, 0, 0, 0)),
                  pl.BlockSpec((1, npad, KCAT), lambda b: (b, 0, 0)),
                  pl.BlockSpec((C_HEADS, KV_LORA, C_VDIM), lambda b: (0, 0, 0))],
        out_specs=pl.BlockSpec((1, tq, C_WIDTH), lambda b: (b, 0, 0)),
        compiler_params=_params(("arbitrary",)),
        name="attn_full",
    )(q, kcat, wuv)


def _outproj_kernel(x_ref, ya_ref, yb_ref, yc_ref, wo_ref, o_ref):
    tn = 512
    for j in range(0, D_MODEL, tn):
        acc = _mm(ya_ref[...], wo_ref[0:A_WIDTH, j:j + tn])
        acc = acc + _mm(yb_ref[...], wo_ref[A_WIDTH:A_WIDTH + B_WIDTH, j:j + tn])
        acc = acc + _mm(yc_ref[...], wo_ref[A_WIDTH + B_WIDTH:D_MODEL, j:j + tn])
        o_ref[:, j:j + tn] = x_ref[:, j:j + tn] + acc


def _outproj(x, ya, yb, yc, wo, layer):
    m = x.shape[0]
    tm = min(m, 256)
    row = lambda i: (i, 0)
    return pl.pallas_call(
        _outproj_kernel,
        out_shape=jax.ShapeDtypeStruct((m, D_MODEL), F32),
        grid=(m // tm,),
        in_specs=[pl.BlockSpec((tm, D_MODEL), row),
                  pl.BlockSpec((tm, A_WIDTH), row),
                  pl.BlockSpec((tm, B_WIDTH), row),
                  pl.BlockSpec((tm, C_WIDTH), row),
                  pl.BlockSpec((None, D_MODEL, D_MODEL), lambda i: (layer, 0, 0))],
        out_specs=pl.BlockSpec((tm, D_MODEL), row),
        compiler_params=_params(("arbitrary",)),
        name="outproj",
    )(x, ya, yb, yc, wo)


def _ffn_kernel(x_ref, nw_ref, wu_ref, wd_ref, fw_ref, o_ref, xn_scr, acc_scr, *, final_norm):
    f = pl.program_id(1)

    @pl.when(f == 0)
    def _():
        x = x_ref[...]
        ms = jnp.mean(x * x, axis=-1, keepdims=True)
        xn_scr[...] = (x * lax.rsqrt(ms + EPS) * nw_ref[...]).astype(BF16)
        acc_scr[...] = jnp.zeros_like(acc_scr)

    hid = jnp.maximum(_mm(xn_scr[...], wu_ref[...]), 0.0)
    acc_scr[...] += _mm((hid * hid).astype(BF16), wd_ref[...])

    @pl.when(f == pl.num_programs(1) - 1)
    def _():
        y = x_ref[...] + acc_scr[...]
        if final_norm:
            ms = jnp.mean(y * y, axis=-1, keepdims=True)
            y = y * lax.rsqrt(ms + EPS) * fw_ref[...]
        o_ref[...] = y


def _ffn(x, norm_w, wu, wd, final_w, final_norm, layer):
    m = x.shape[0]
    tm = min(m, 512)
    tf = 1024
    return pl.pallas_call(
        functools.partial(_ffn_kernel, final_norm=final_norm),
        out_shape=jax.ShapeDtypeStruct((m, D_MODEL), F32),
        grid=(m // tm, D_FF // tf),
        in_specs=[pl.BlockSpec((tm, D_MODEL), lambda i, f: (i, 0)),
                  pl.BlockSpec((1, D_MODEL), lambda i, f: (0, 0)),
                  pl.BlockSpec((None, D_MODEL, tf), lambda i, f: (layer, 0, f)),
                  pl.BlockSpec((None, tf, D_MODEL), lambda i, f: (layer, f, 0)),
                  pl.BlockSpec((1, D_MODEL), lambda i, f: (0, 0))],
        out_specs=pl.BlockSpec((tm, D_MODEL), lambda i, f: (i, 0)),
        scratch_shapes=[pltpu.VMEM((tm, D_MODEL), BF16), pltpu.VMEM((tm, D_MODEL), F32)],
        compiler_params=_params(("arbitrary", "arbitrary")),
        name="ffn",
    )(x, norm_w.reshape(1, D_MODEL), wu, wd, final_w.reshape(1, D_MODEL))


def _swap_halves(w):
    half = C_ROPE // 2
    return jnp.concatenate([w[..., half:], w[..., :half]], axis=-1)


def _prep_w_in(w_in):
    w_in = w_in.astype(BF16)
    lead = w_in.shape[:-1]
    ckr = w_in[..., OFF_CKR:IN_COLS]
    g1 = jnp.concatenate([ckr, w_in[..., OFF_BBETA:OFF_CQ], jnp.zeros(lead + (LANE - C_ROPE - 2 * B_HEADS,), BF16)],
                         axis=-1)
    g2 = jnp.concatenate([_swap_halves(ckr), jnp.zeros(lead + (LANE - C_ROPE,), BF16)], axis=-1)
    parts = [w_in[..., OFF_BQKV:OFF_BZ], w_in[..., OFF_AU:OFF_AV], w_in[..., OFF_AV:OFF_BQKV],
             w_in[..., OFF_BZ:OFF_BBETA], w_in[..., OFF_CQ:OFF_CKV], w_in[..., OFF_CKV:OFF_CKR], g1, g2]
    return jnp.concatenate(parts, axis=-1)


def _rope_tables(pos):
    half = C_ROPE // 2
    inv = ROPE_THETA ** (-jnp.arange(half, dtype=F32) / half)
    ang = pos.astype(F32)[:, None] * inv[None, :]
    cos, sin = jnp.cos(ang), jnp.sin(ang)
    cos2 = jnp.concatenate([cos, cos, cos, cos], axis=1)
    sin2 = jnp.concatenate([-sin, sin, -sin, sin], axis=1)
    return cos2, sin2


def _layer(x, bsz, L, tabs, lw, past, final_w, final_norm, layer, depth, stacks):
    proj = _inproj(x, lw['attn_norm_w'], lw['w_in'], layer).reshape(bsz, L, P_COLS)

    ya, va = _mixer_a(proj, lw['a_ln_w'], lw['a_ln_b'], lw['a_w_s'], lw['a_b_s'], past is not None)

    if past is None:
        rows, chunk = 2 * CHUNK, CHUNK
        buf8 = jnp.zeros((bsz, 8, B_QKV), F32)
        s0 = jnp.zeros((bsz, B_HEADS, B_DK, B_DV), F32)
    else:
        rows, chunk = L, L
        ckv_past, kr_past, s0, conv_buf = past
        buf8 = jnp.concatenate([jnp.zeros((bsz, 8 - (B_CONV - 1), B_QKV), F32), conv_buf], axis=1)
    prep = _gdn_prep(proj, buf8, lw['b_conv_w'], lw['b_a_log'], lw['b_dt_bias'], rows, chunk)
    yb, s_new = _gdn_scan(prep, proj, lw['b_norm_w'], s0, rows, chunk)
    buf_new = proj[:, L - (B_CONV - 1):, P_BQKV:P_BQKV + B_QKV]

    cos8, sin8 = tabs
    mla_args = (proj, cos8, sin8, lw['c_q_norm_w'], lw['wqn'], lw['wqr'], lw['wqs'], lw['c_kv_norm_w'])
    if past is None:
        ckv, kr, q, k, kr2, v = _mla_proj(*mla_args, (lw['wk'], lw['wv']), False, layer, depth, stacks)
        yc = _attn_heads(q, k, kr2, v)
    else:
        ckv, kr, kcat, q = _mla_proj(*mla_args, (lw['wukt'],), True, layer, depth, stacks)
        past_len = ckv_past.shape[1]
        nkeys = past_len + L
        npad = -(-nkeys // LANE) * LANE
        kpast = jnp.concatenate([ckv_past, kr_past, kr_past], axis=-1).astype(BF16)
        kall = jnp.concatenate([kpast, kcat, jnp.zeros((bsz, npad - nkeys, KCAT), BF16)], axis=1)
        yc = _attn_full(q, kall, lw['wuv'], nkeys)

    m = bsz * L
    x = _outproj(x, ya.reshape(m, A_WIDTH), yb.reshape(m, B_WIDTH), yc.reshape(m, C_WIDTH), lw['w_out'], layer)
    x = _ffn(x, lw['mlp_norm_w'], lw['w_up'], lw['w_down'], final_w, final_norm, layer)
    return x, (ckv, kr, s_new, buf_new, va)


def kernel(x_prompt, x_sample, cache_mla_ckv, cache_mla_krope, state_gdn, state_gdn_conv,
           attn_norm_w, w_in, a_ln_w, a_ln_b, a_w_s, a_b_s, b_conv_w, b_a_log, b_dt_bias, b_norm_w,
           c_q_norm_w, c_w_uq, c_kv_norm_w, c_w_uk, c_w_uv, w_out, mlp_norm_w, w_up, w_down, final_norm_w):
    depth = w_in.shape[0]
    bp, lp, _ = x_prompt.shape
    bs, ls, _ = x_sample.shape
    past_len = cache_mla_ckv.shape[2]
    tabs_p = _rope_tables(jnp.arange(lp, dtype=jnp.int32))
    tabs_s = _rope_tables(past_len + jnp.arange(ls, dtype=jnp.int32))
    hp = x_prompt.reshape(bp * lp, D_MODEL)
    hs = x_sample.reshape(bs * ls, D_MODEL)
    sp_list, ss_list = [], []
    stacks_p = stacks_s = None
    w_in16, w_out16, w_up16, w_down16 = _prep_w_in(w_in), w_out.astype(BF16), w_up.astype(BF16), w_down.astype(BF16)
    for l in range(depth):
        wq_rope = c_w_uq[l][:, :, C_NOPE:]
        lw = {
            'attn_norm_w': attn_norm_w[l], 'w_in': w_in16,
            'a_ln_w': a_ln_w[l], 'a_ln_b': a_ln_b[l], 'a_w_s': a_w_s[l], 'a_b_s': a_b_s[l],
            'b_conv_w': b_conv_w[l], 'b_a_log': b_a_log[l], 'b_dt_bias': b_dt_bias[l], 'b_norm_w': b_norm_w[l],
            'c_q_norm_w': c_q_norm_w[l], 'c_kv_norm_w': c_kv_norm_w[l],
            'wqn': c_w_uq[l][:, :, :C_NOPE].reshape(Q_LORA, C_HEADS * C_NOPE).astype(BF16),
            'wqr': wq_rope.reshape(Q_LORA, C_HEADS * C_ROPE).astype(BF16),
            'wqs': _swap_halves(wq_rope).reshape(Q_LORA, C_HEADS * C_ROPE).astype(BF16),
            'wk': c_w_uk[l].reshape(KV_LORA, C_HEADS * C_NOPE).astype(BF16),
            'wv': c_w_uv[l].reshape(KV_LORA, C_HEADS * C_VDIM).astype(BF16),
            'wukt': jnp.transpose(c_w_uk[l], (1, 2, 0)).astype(BF16),
            'wuv': jnp.transpose(c_w_uv[l], (1, 0, 2)).astype(BF16),
            'w_out': w_out16, 'mlp_norm_w': mlp_norm_w[l], 'w_up': w_up16, 'w_down': w_down16,
        }
        last = l == depth - 1
        hp, sp = _layer(hp, bp, lp, tabs_p, lw, None, final_norm_w, last, l, depth, stacks_p)
        hs, ss = _layer(hs, bs, ls, tabs_s, lw,
                        (cache_mla_ckv[l], cache_mla_krope[l], state_gdn[l], state_gdn_conv[l]), final_norm_w, last,
                        l, depth, stacks_s)
        stacks_p, stacks_s = sp[:2], ss[:2]
        sp_list.append(sp)
        ss_list.append(ss)

    def stack(lst, i):
        return jnp.stack([s[i] for s in lst], axis=0)

    return (hp.reshape(bp, lp, D_MODEL), hs.reshape(bs, ls, D_MODEL),
            stacks_p[0], stacks_p[1], stack(sp_list, 2), stack(sp_list, 3),
            stacks_s[0], stacks_s[1], stack(ss_list, 2), stack(ss_list, 3), stack(ss_list, 4))
```

```python
import functools
import math

import jax
import jax.numpy as jnp
from jax import lax
from jax.experimental import pallas as pl
from jax.experimental.pallas import tpu as pltpu

F32 = jnp.float32
BF16 = jnp.bfloat16

D_MODEL = 2048
CHUNK = 64
EPS = 1e-6
LN_EPS = 1e-5
A_GROUPS = 4
A_HEAD = 128
A_WIDTH = 512
A_CHUNK = 128
B_HEADS = 4
B_DK = 128
B_DV = 128
B_WIDTH = 512
B_QKV = 1536
B_CONV = 4
C_HEADS = 8
C_NOPE = 128
C_ROPE = 64
C_VDIM = 128
C_WIDTH = 1024
Q_LORA = 512
KV_LORA = 256
ROPE_THETA = 10000.0
C_SCALE = (C_NOPE + C_ROPE) ** -0.5
Q_SCALE = C_SCALE * math.log2(math.e)
D_FF = 4 * D_MODEL

OFF_AU = 0
OFF_AV = OFF_AU + A_WIDTH
OFF_BQKV = OFF_AV + A_WIDTH
OFF_BZ = OFF_BQKV + B_QKV
OFF_BBETA = OFF_BZ + B_WIDTH
OFF_BALPHA = OFF_BBETA + B_HEADS
OFF_CQ = OFF_BALPHA + B_HEADS
OFF_CKV = OFF_CQ + Q_LORA
OFF_CKR = OFF_CKV + KV_LORA
IN_COLS = OFF_CKR + C_ROPE

P_BQKV = 0
P_AU = 1536
P_AV = 2048
P_BZ = 2560
P_CQ = 3072
P_CKV = 3584
P_G1 = 3840
P_G2 = 3968
P_COLS = 4096
LANE = 128
G1_BETA = C_ROPE
G1_ALPHA = C_ROPE + B_HEADS
KCAT = KV_LORA + 2 * C_ROPE
QK_DIM = C_NOPE + 2 * C_ROPE

VMEM_LIMIT = 56 * 1024 * 1024


def _mm(a, b):
    return jnp.dot(a, b, preferred_element_type=F32)


def _mm_nt(a, b):
    return lax.dot_general(a, b, (((1,), (1,)), ((), ())), preferred_element_type=F32)


def _split2(x):
    hi = x.astype(BF16)
    lo = (x - hi.astype(F32)).astype(BF16)
    return hi, lo


def _split3(x):
    hi = x.astype(BF16)
    r = x - hi.astype(F32)
    mid = r.astype(BF16)
    lo = (r - mid.astype(F32)).astype(BF16)
    return hi, mid, lo


def _cat3_lhs(a):
    ah, al = _split2(a)
    return jnp.concatenate([ah, ah, al], axis=1)


def _cat3_rhs(b):
    bh, bl = _split2(b)
    return jnp.concatenate([bh, bl, bh], axis=0)


def _mm3(a, b):
    return _mm(_cat3_lhs(a), _cat3_rhs(b))


def _mm_exact_lhs(a, b):
    return _mm(jnp.concatenate([a, a, a], axis=1), jnp.concatenate(_split3(b), axis=0))


def _mm_nt_exact_lhs(a, b):
    return _mm_nt(jnp.concatenate([a, a, a], axis=1), jnp.concatenate(_split3(b), axis=1))


def _lane_tile(x, n):
    return x if n == 1 else jnp.concatenate([x] * n, axis=1)


def _params(sem):
    return pltpu.CompilerParams(dimension_semantics=sem, vmem_limit_bytes=VMEM_LIMIT)


def _inproj_kernel(x_ref, nw_ref, w_ref, o_ref):
    x = x_ref[...]
    ms = jnp.mean(x * x, axis=-1, keepdims=True)
    xn = (x * lax.rsqrt(ms + EPS) * nw_ref[...]).astype(BF16)
    o_ref[...] = _mm(xn, w_ref[...])


def _inproj(x, norm_w, w_pad, layer):
    m = x.shape[0]
    tm = min(m, 512)
    tn = 1024
    return pl.pallas_call(
        _inproj_kernel,
        out_shape=jax.ShapeDtypeStruct((m, P_COLS), F32),
        grid=(P_COLS // tn, m // tm),
        in_specs=[pl.BlockSpec((tm, D_MODEL), lambda j, i: (i, 0)),
                  pl.BlockSpec((1, D_MODEL), lambda j, i: (0, 0)),
                  pl.BlockSpec((None, D_MODEL, tn), lambda j, i: (layer, 0, j))],
        out_specs=pl.BlockSpec((tm, tn), lambda j, i: (i, j)),
        compiler_params=_params(("arbitrary", "arbitrary")),
        name="inproj",
    )(x, norm_w.reshape(1, D_MODEL), w_pad)


def _gelu(x):
    return 0.5 * x * (1.0 + lax.erf(x * (2.0 ** -0.5)))


def _mixa_kernel(u_ref, v_ref, lnw_ref, lnb_ref, ws_ref, bs_ref, ya_ref, *va_ref, lc, nsub):
    u = _gelu(u_ref[0])
    v = _gelu(v_ref[0])
    mu = jnp.mean(v, axis=-1, keepdims=True)
    vc = v - mu
    var = jnp.mean(vc * vc, axis=-1, keepdims=True)
    vn = vc * lax.rsqrt(var + LN_EPS) * lnw_ref[...] + lnb_ref[...]
    if va_ref:
        va_ref[0][0] = vn
    for s in range(nsub):
        rows = slice(s * lc, (s + 1) * lc)
        for g in range(A_GROUPS):
            cols = slice(g * A_HEAD, (g + 1) * A_HEAD)
            f = _mm(ws_ref[g], vn[rows, cols].astype(BF16)) + bs_ref[g]
            ya_ref[0, rows, cols] = (u[rows, cols] * f).astype(BF16)


def _mixer_a(proj, ln_w, ln_b, w_s, b_s, want_va):
    bsz, L, _ = proj.shape
    lc = min(L, A_CHUNK)
    rows = min(L, 512)
    nsub = rows // lc
    idx = jnp.arange(lc)
    allowed = (idx[:, None] // CHUNK) >= (idx[None, :] // CHUNK)
    ws = jnp.where(allowed[None], w_s[:, :lc, :lc], 0.0).astype(BF16)
    bs = jnp.broadcast_to(b_s[:, :lc, None], (A_GROUPS, lc, A_HEAD)).astype(F32)
    out_shape = [jax.ShapeDtypeStruct((bsz, L, A_WIDTH), BF16)]
    out_specs = [pl.BlockSpec((1, rows, A_WIDTH), lambda b, i: (b, i, 0))]
    if want_va:
        out_shape.append(jax.ShapeDtypeStruct((bsz, L, A_WIDTH), F32))
        out_specs.append(pl.BlockSpec((1, rows, A_WIDTH), lambda b, i: (b, i, 0)))
    res = pl.pallas_call(
        functools.partial(_mixa_kernel, lc=lc, nsub=nsub),
        out_shape=out_shape,
        grid=(bsz, L // rows),
        in_specs=[pl.BlockSpec((1, rows, A_WIDTH), lambda b, i: (b, i, P_AU // A_WIDTH)),
                  pl.BlockSpec((1, rows, A_WIDTH), lambda b, i: (b, i, P_AV // A_WIDTH)),
                  pl.BlockSpec((1, A_WIDTH), lambda b, i: (0, 0)),
                  pl.BlockSpec((1, A_WIDTH), lambda b, i: (0, 0)),
                  pl.BlockSpec((A_GROUPS, lc, lc), lambda b, i: (0, 0, 0)),
                  pl.BlockSpec((A_GROUPS, lc, A_HEAD), lambda b, i: (0, 0, 0))],
        out_specs=out_specs,
        compiler_params=_params(("arbitrary", "arbitrary")),
        name="mixer_a",
    )(proj, proj, ln_w.reshape(1, A_WIDTH), ln_b.reshape(1, A_WIDTH), ws, bs)
    return res if want_va else (res[0], None)


def _softplus(x):
    return jnp.maximum(x, 0.0) + jnp.log1p(jnp.exp(-jnp.abs(x)))


def _gdn_prep_kernel(x_ref, halo_ref, buf_ref, g1_ref, cw_ref, alog_ref, dtb_ref,
                     u_ref, w_ref, qg_ref, kgt_ref, qk_ref, eg_ref, xp_scr, *, nb, rows, chunk):
    i = pl.program_id(1)
    nsub = rows // chunk
    ri = lax.broadcasted_iota(jnp.int32, (rows, rows), 0)
    ci = lax.broadcasted_iota(jnp.int32, (rows, rows), 1)
    same = (ri // chunk) == (ci // chunk)
    causal = same & (ri >= ci)
    strict = same & (ri > ci)
    tri = jnp.where(causal, 1.0, 0.0).astype(BF16)
    eye_r = jnp.where(ri == ci, 1.0, 0.0).astype(F32)
    li = lax.broadcasted_iota(jnp.int32, (LANE, LANE), 0)
    lj = lax.broadcasted_iota(jnp.int32, (LANE, LANE), 1)
    eye_l = jnp.where(li == lj, 1.0, 0.0).astype(BF16)

    ys, betas, gts = [], [], []
    for bb in range(nb):
        x = x_ref[bb]
        xp_scr[bb, 0:8, :] = jnp.where(i == 0, buf_ref[bb], halo_ref[bb])
        xp_scr[bb, 8:8 + rows, :] = x
        y = x * cw_ref[B_CONV - 1:B_CONV, :]
        for k in range(1, B_CONV):
            y = y + xp_scr[bb, 8 - k:8 - k + rows, :] * cw_ref[B_CONV - 1 - k:B_CONV - k, :]
        ys.append(y * jax.nn.sigmoid(y))
        g1 = g1_ref[bb]
        betas.append(jax.nn.sigmoid(g1))
        gts.append(-jnp.exp(alog_ref[...]) * _softplus(g1 + dtb_ref[...]))

    gcs = [_mm_exact_lhs(tri, g) for g in gts]
    gcts = [_mm_nt_exact_lhs(eye_l, gc) for gc in gcs]
    glasts = [jnp.concatenate(
        [jnp.broadcast_to(gc[(s + 1) * chunk - 1:(s + 1) * chunk, :], (chunk, LANE)) for s in range(nsub)], axis=0)
        for gc in gcs]

    pairs = [(bb, h) for bb in range(nb) for h in range(B_HEADS)]
    qs, ks, kbs, rhss, decs, egcs = [], [], [], [], [], []
    for bb, h in pairs:
        y, gc = ys[bb], gcs[bb]
        bcol = betas[bb][:, G1_BETA + h:G1_BETA + h + 1]
        gcol = gc[:, G1_ALPHA + h:G1_ALPHA + h + 1]
        grow = gcts[bb][G1_ALPHA + h:G1_ALPHA + h + 1, :]
        qh = y[:, h * B_DK:(h + 1) * B_DK]
        qh = qh * lax.rsqrt(jnp.sum(qh * qh, axis=-1, keepdims=True) + EPS) * (B_DK ** -0.5)
        kh = y[:, B_WIDTH + h * B_DK:B_WIDTH + (h + 1) * B_DK]
        kh = kh * lax.rsqrt(jnp.sum(kh * kh, axis=-1, keepdims=True) + EPS)
        vh = y[:, 2 * B_WIDTH + h * B_DV:2 * B_WIDTH + (h + 1) * B_DV]
        kb = kh * bcol
        egc = jnp.exp(gcol)
        qs.append(qh)
        ks.append(kh)
        kbs.append(kb)
        egcs.append(egc)
        rhss.append(jnp.concatenate([vh * bcol, kb * egc], axis=1))
        decs.append(jnp.where(causal, jnp.exp(jnp.minimum(gcol - grow, 0.0)), 0.0))

    khb = [k.astype(BF16) for k in ks]
    kk = [_mm_nt(kb.astype(BF16), kx) for kb, kx in zip(kbs, khb)]
    qkm = [_mm_nt(q.astype(BF16), kx) for q, kx in zip(qs, khb)]

    ps = [jnp.where(strict, -(a * d), 0.0) for a, d in zip(kk, decs)]
    ts = [eye_r + p for p in ps]
    ms = [_mm(_cat3_lhs(p), _cat3_rhs(p)) for p in ps]
    nsq = int(math.log2(chunk)) - 1
    for step in range(nsq):
        mr = [_cat3_rhs(m) for m in ms]
        upd = [_mm(_cat3_lhs(t), b) for t, b in zip(ts, mr)]
        if step + 1 < nsq:
            ms = [_mm(_cat3_lhs(m), b) for m, b in zip(ms, mr)]
        ts = [t + d for t, d in zip(ts, upd)]
    sols = [_mm3(t, r) for t, r in zip(ts, rhss)]

    for n, (bb, h) in enumerate(pairs):
        hs = slice(h * B_DK, (h + 1) * B_DK)
        gcol = gcs[bb][:, G1_ALPHA + h:G1_ALPHA + h + 1]
        glcol = glasts[bb][:, G1_ALPHA + h:G1_ALPHA + h + 1]
        u_ref[bb, :, hs] = sols[n][:, :B_DV]
        w_ref[bb, :, hs] = sols[n][:, B_DV:].astype(BF16)
        qg_ref[bb, :, hs] = (qs[n] * egcs[n]).astype(BF16)
        kg = (ks[n] * jnp.exp(glcol - gcol)).astype(BF16)
        kgt_ref[bb, hs, :] = _mm_nt(eye_l, kg).astype(BF16)
        qk_ref[bb, :, h * rows:(h + 1) * rows] = (qkm[n] * decs[n]).astype(BF16)
        for s in range(nsub):
            eg_ref[bb, s, h:h + 1, :] = jnp.broadcast_to(
                jnp.exp(glasts[bb][s * chunk:s * chunk + 1, G1_ALPHA + h:G1_ALPHA + h + 1]), (1, LANE))


def _gdn_prep(proj, buf8, conv_w, a_log, dt_bias, rows, chunk):
    bsz, L, _ = proj.shape
    nb = 2
    nblk = L // rows
    rpb = max(rows // 8, 1)
    alog = jnp.zeros((1, LANE), F32).at[0, G1_ALPHA:G1_ALPHA + B_HEADS].set(a_log)
    dtb = jnp.zeros((1, LANE), F32).at[0, G1_ALPHA:G1_ALPHA + B_HEADS].set(dt_bias)
    return pl.pallas_call(
        functools.partial(_gdn_prep_kernel, nb=nb, rows=rows, chunk=chunk),
        out_shape=[jax.ShapeDtypeStruct((bsz, L, B_WIDTH), F32),
                   jax.ShapeDtypeStruct((bsz, L, B_WIDTH), BF16),
                   jax.ShapeDtypeStruct((bsz, L, B_WIDTH), BF16),
                   jax.ShapeDtypeStruct((bsz, B_WIDTH, L), BF16),
                   jax.ShapeDtypeStruct((bsz, L, B_HEADS * rows), BF16),
                   jax.ShapeDtypeStruct((bsz, L // chunk, B_HEADS, LANE), F32)],
        grid=(bsz // nb, nblk),
        in_specs=[pl.BlockSpec((nb, rows, B_QKV), lambda b, i: (b, i, 0)),
                  pl.BlockSpec((nb, 8, B_QKV), lambda b, i: (b, jnp.maximum(i * rpb - 1, 0), 0)),
                  pl.BlockSpec((nb, 8, B_QKV), lambda b, i: (b, 0, 0)),
                  pl.BlockSpec((nb, rows, LANE), lambda b, i: (b, i, P_G1 // LANE)),
                  pl.BlockSpec((B_CONV, B_QKV), lambda b, i: (0, 0)),
                  pl.BlockSpec((1, LANE), lambda b, i: (0, 0)),
                  pl.BlockSpec((1, LANE), lambda b, i: (0, 0))],
        out_specs=[pl.BlockSpec((nb, rows, B_WIDTH), lambda b, i: (b, i, 0)),
                   pl.BlockSpec((nb, rows, B_WIDTH), lambda b, i: (b, i, 0)),
                   pl.BlockSpec((nb, rows, B_WIDTH), lambda b, i: (b, i, 0)),
                   pl.BlockSpec((nb, B_WIDTH, rows), lambda b, i: (b, 0, i)),
                   pl.BlockSpec((nb, rows, B_HEADS * rows), lambda b, i: (b, i, 0)),
                   pl.BlockSpec((nb, rows // chunk, B_HEADS, LANE), lambda b, i: (b, i, 0, 0))],
        scratch_shapes=[pltpu.VMEM((nb, rows + 8, B_QKV), F32)],
        compiler_params=_params(("arbitrary", "arbitrary")),
        name="gdn_prep",
    )(proj, proj, buf8, proj, conv_w, alog, dtb)


def _gdn_scan_kernel(u_ref, w_ref, qg_ref, kgt_ref, qk_ref, eg_ref, z_ref, nw_ref, s0_ref,
                     yb_ref, s_ref, *, bsz, rows, chunk):
    @pl.when(pl.program_id(0) == 0)
    def _():
        s_ref[...] = s0_ref[...]

    nsub = rows // chunk
    pairs = [(b, h) for b in range(bsz) for h in range(B_HEADS)]
    st = [s_ref[b, h] for b, h in pairs]
    for s in range(nsub):
        rs = slice(s * chunk, (s + 1) * chunk)
        r1 = [_mm(jnp.concatenate([w_ref[b, rs, h * B_DV:(h + 1) * B_DV], qg_ref[b, rs, h * B_DV:(h + 1) * B_DV]],
                                  axis=0), x.astype(BF16))
              for (b, h), x in zip(pairs, st)]
        vb = [(u_ref[b, rs, h * B_DV:(h + 1) * B_DV] - r[:chunk]).astype(BF16) for (b, h), r in zip(pairs, r1)]
        r2 = [_mm(jnp.concatenate([qk_ref[b, rs, h * rows + s * chunk:h * rows + (s + 1) * chunk],
                                   kgt_ref[b, h * B_DK:(h + 1) * B_DK, rs]], axis=0), v)
              for (b, h), v in zip(pairs, vb)]
        for n, (b, h) in enumerate(pairs):
            hs = slice(h * B_DV, (h + 1) * B_DV)
            o = r1[n][chunk:] + r2[n][:chunk]
            st[n] = st[n] * eg_ref[b, s, h:h + 1, :] + r2[n][chunk:]
            z = z_ref[b, rs, hs]
            on = o * lax.rsqrt(jnp.mean(o * o, axis=-1, keepdims=True) + EPS) * nw_ref[...]
            yb_ref[b, rs, hs] = (on * (z * jax.nn.sigmoid(z))).astype(BF16)
    for n, (b, h) in enumerate(pairs):
        s_ref[b, h] = st[n]


def _gdn_scan(prep, proj, norm_w, s0, rows, chunk):
    u, w, qg, kgt, qk, eg = prep
    bsz, L, _ = u.shape
    nblk = L // rows
    return pl.pallas_call(
        functools.partial(_gdn_scan_kernel, bsz=bsz, rows=rows, chunk=chunk),
        out_shape=[jax.ShapeDtypeStruct((bsz, L, B_WIDTH), BF16),
                   jax.ShapeDtypeStruct((bsz, B_HEADS, B_DK, B_DV), F32)],
        grid=(nblk,),
        in_specs=[pl.BlockSpec((bsz, rows, B_WIDTH), lambda i: (0, i, 0)),
                  pl.BlockSpec((bsz, rows, B_WIDTH), lambda i: (0, i, 0)),
                  pl.BlockSpec((bsz, rows, B_WIDTH), lambda i: (0, i, 0)),
                  pl.BlockSpec((bsz, B_WIDTH, rows), lambda i: (0, 0, i)),
                  pl.BlockSpec((bsz, rows, B_HEADS * rows), lambda i: (0, i, 0)),
                  pl.BlockSpec((bsz, rows // chunk, B_HEADS, LANE), lambda i: (0, i, 0, 0)),
                  pl.BlockSpec((bsz, rows, B_WIDTH), lambda i: (0, i, P_BZ // B_WIDTH)),
                  pl.BlockSpec((1, B_DV), lambda i: (0, 0)),
                  pl.BlockSpec((bsz, B_HEADS, B_DK, B_DV), lambda i: (0, 0, 0, 0))],
        out_specs=[pl.BlockSpec((bsz, rows, B_WIDTH), lambda i: (0, i, 0)),
                   pl.BlockSpec((bsz, B_HEADS, B_DK, B_DV), lambda i: (0, 0, 0, 0))],
        compiler_params=_params(("arbitrary",)),
        name="gdn_scan",
    )(u, w, qg, kgt, qk, eg, proj, norm_w.reshape(1, B_DV), s0)


def _mla_common(cq_ref, ckv_ref, g1_ref, g2_ref, cos8_ref, sin8_ref, qnw_ref, wqn_ref, wqr_ref, wqs_ref, kvnw_ref):
    cq = cq_ref[0]
    cqn = (cq * lax.rsqrt(jnp.mean(cq * cq, axis=-1, keepdims=True) + EPS) * qnw_ref[...]).astype(BF16)
    c = ckv_ref[0]
    ckv = c * lax.rsqrt(jnp.mean(c * c, axis=-1, keepdims=True) + EPS) * kvnw_ref[...]
    cos2, sin2 = cos8_ref[...], sin8_ref[...]
    kr = g1_ref[0][:, 0:C_ROPE] * cos2[:, 0:C_ROPE] + g2_ref[0][:, 0:C_ROPE] * sin2[:, 0:C_ROPE]
    qn = _mm(cqn, wqn_ref[...])
    ntile = C_HEADS * C_ROPE // LANE
    qr = (_mm(cqn, wqr_ref[...]) * _lane_tile(cos2, ntile) + _mm(cqn, wqs_ref[...]) * _lane_tile(sin2, ntile)) * Q_SCALE
    return ckv, kr, qn, qr


def _rope_slot(qr, h):
    lane = lax.broadcasted_iota(jnp.int32, (1, LANE), 1)
    pair = qr[:, (h // 2) * LANE:(h // 2 + 1) * LANE]
    return jnp.where((lane // C_ROPE) == (h % 2), pair, 0.0).astype(BF16)


def _write_stacked(out_ref, val, layer, prev_ref):
    for d in range(out_ref.shape[0]):
        if d == layer:
            out_ref[d] = val
        elif prev_ref is None:
            out_ref[d] = jnp.zeros_like(val)
        else:
            out_ref[d] = prev_ref[d]


def _mla_proj_kernel(cq_ref, ckv_ref, g1_ref, g2_ref, cos8_ref, sin8_ref, qnw_ref, wqn_ref, wqr_ref, wqs_ref,
                     kvnw_ref, wukt_ref, ckv_out, kr_out, kcat_out, q_out, *, layer, prev):
    ckv, kr, qn, qr = _mla_common(cq_ref, ckv_ref, g1_ref, g2_ref, cos8_ref, sin8_ref, qnw_ref, wqn_ref, wqr_ref,
                                  wqs_ref, kvnw_ref)
    _write_stacked(ckv_out, ckv, layer, prev and prev[0])
    _write_stacked(kr_out, kr, layer, prev and prev[1])
    kcat_out[0, :, 0:KV_LORA] = ckv.astype(BF16)
    kcat_out[0, :, KV_LORA:KV_LORA + C_ROPE] = kr.astype(BF16)
    kcat_out[0, :, KV_LORA + C_ROPE:KCAT] = kr.astype(BF16)
    for h in range(C_HEADS):
        ql = _mm(qn[:, h * C_NOPE:(h + 1) * C_NOPE].astype(BF16), wukt_ref[h]) * Q_SCALE
        q_out[0, h, :, 0:KV_LORA] = ql.astype(BF16)
        q_out[0, h, :, KV_LORA:KCAT] = _rope_slot(qr, h)


def _mla_heads_kernel(cq_ref, ckv_ref, g1_ref, g2_ref, cos8_ref, sin8_ref, qnw_ref, wqn_ref, wqr_ref, wqs_ref,
                      kvnw_ref, wk_ref, wv_ref, ckv_out, kr_out, q_out, k_out, kr2_out, v_out, *, layer, prev):
    ckv, kr, qn, qr = _mla_common(cq_ref, ckv_ref, g1_ref, g2_ref, cos8_ref, sin8_ref, qnw_ref, wqn_ref, wqr_ref,
                                  wqs_ref, kvnw_ref)
    _write_stacked(ckv_out, ckv, layer, prev and prev[0])
    _write_stacked(kr_out, kr, layer, prev and prev[1])
    ckvb = ckv.astype(BF16)
    kn = _mm(ckvb, wk_ref[...])
    vv = _mm(ckvb, wv_ref[...])
    krb = kr.astype(BF16)
    kr2_out[0] = jnp.concatenate([krb, krb], axis=1)
    for h in range(C_HEADS):
        hs = slice(h * C_NOPE, (h + 1) * C_NOPE)
        q_out[0, h, :, 0:C_NOPE] = (qn[:, hs] * Q_SCALE).astype(BF16)
        q_out[0, h, :, C_NOPE:QK_DIM] = _rope_slot(qr, h)
        k_out[0, h] = kn[:, hs].astype(BF16)
        v_out[0, h] = vv[:, h * C_VDIM:(h + 1) * C_VDIM].astype(BF16)


def _mla_proj(proj, cos8, sin8, q_norm_w, wqn, wqr, wqs, kv_norm_w, head_w, absorbed, layer, depth, stacks):
    bsz, L, _ = proj.shape
    tm = min(L, 256)
    const2 = lambda b, i: (0, 0)
    head_out = lambda width: (jax.ShapeDtypeStruct((bsz, C_HEADS, L, width), BF16),
                              pl.BlockSpec((1, C_HEADS, tm, width), lambda b, i: (b, 0, i, 0)))
    stack_out = lambda width: (jax.ShapeDtypeStruct((depth, bsz, L, width), F32),
                               pl.BlockSpec((depth, None, tm, width), lambda b, i: (0, b, i, 0)))
    outs = [stack_out(KV_LORA), stack_out(C_ROPE)]
    if absorbed:
        body = _mla_proj_kernel
        w_specs = [pl.BlockSpec((C_HEADS, C_NOPE, KV_LORA), lambda b, i: (0, 0, 0))]
        outs += [(jax.ShapeDtypeStruct((bsz, L, KCAT), BF16), pl.BlockSpec((1, tm, KCAT), lambda b, i: (b, i, 0))),
                 head_out(KCAT)]
    else:
        body = _mla_heads_kernel
        w_specs = [pl.BlockSpec((KV_LORA, C_HEADS * C_NOPE), const2), pl.BlockSpec((KV_LORA, C_HEADS * C_VDIM), const2)]
        outs += [head_out(QK_DIM), head_out(C_NOPE),
                 (jax.ShapeDtypeStruct((bsz, L, LANE), BF16), pl.BlockSpec((1, tm, LANE), lambda b, i: (b, i, 0))),
                 head_out(C_VDIM)]
    if stacks is None:
        kern, prev_specs, prev_args = functools.partial(body, layer=layer, prev=None), [], ()
    else:
        kern = lambda ckv_prev, kr_prev, *refs: body(*refs, layer=layer, prev=(ckv_prev, kr_prev))
        prev_specs, prev_args = [outs[0][1], outs[1][1]], tuple(stacks)
    return pl.pallas_call(
        kern,
        out_shape=[o[0] for o in outs],
        grid=(bsz, L // tm),
        in_specs=prev_specs + [pl.BlockSpec((1, tm, Q_LORA), lambda b, i: (b, i, P_CQ // Q_LORA)),
                  pl.BlockSpec((1, tm, KV_LORA), lambda b, i: (b, i, P_CKV // KV_LORA)),
                  pl.BlockSpec((1, tm, LANE), lambda b, i: (b, i, P_G1 // LANE)),
                  pl.BlockSpec((1, tm, LANE), lambda b, i: (b, i, P_G2 // LANE)),
                  pl.BlockSpec((tm, LANE), lambda b, i: (i, 0)),
                  pl.BlockSpec((tm, LANE), lambda b, i: (i, 0)),
                  pl.BlockSpec((1, Q_LORA), const2),
                  pl.BlockSpec((Q_LORA, C_HEADS * C_NOPE), const2),
                  pl.BlockSpec((Q_LORA, C_HEADS * C_ROPE), const2),
                  pl.BlockSpec((Q_LORA, C_HEADS * C_ROPE), const2),
                  pl.BlockSpec((1, KV_LORA), const2)] + w_specs,
        out_specs=[o[1] for o in outs],
        compiler_params=_params(("arbitrary", "arbitrary")),
        name="mla_proj" if absorbed else "mla_heads",
    )(*prev_args, proj, proj, proj, proj, cos8, sin8, q_norm_w.reshape(1, Q_LORA), wqn, wqr, wqs,
      kv_norm_w.reshape(1, KV_LORA), *head_w)


def _attn_heads_kernel(q_ref, k_ref, kr2_ref, v_ref, yc_ref, m_scr, l_scr, acc_scr, *, tq, tk, hps):
    qi = pl.program_id(2)
    rows = hps * tq
    m_scr[...] = jnp.full((rows, LANE), -1e30, F32)
    l_scr[...] = jnp.zeros((rows, LANE), F32)
    acc_scr[...] = jnp.zeros((rows, C_VDIM), F32)

    def block(kj, masked):
        start = pl.multiple_of(kj * tk, tk)
        kr2 = kr2_ref[0, pl.ds(start, tk), :]
        if masked:
            ri = lax.broadcasted_iota(jnp.int32, (tq, tk), 0)
            ci = lax.broadcasted_iota(jnp.int32, (tq, tk), 1)
            ok = ((qi * tq + ri) // CHUNK) >= ((start + ci) // CHUNK)

        def scores(g):
            return _mm_nt(q_ref[0, g], jnp.concatenate([k_ref[0, g, pl.ds(start, tk), :], kr2], axis=1))

        s_next = scores(0)
        for g in range(hps):
            rs = slice(g * tq, (g + 1) * tq)
            s = s_next
            if g + 1 < hps:
                s_next = scores(g + 1)
            if masked:
                s = jnp.where(ok, s, -jnp.inf)
            m_prev = m_scr[rs]
            m_new = jnp.maximum(m_prev, jnp.max(s, axis=-1, keepdims=True))
            alpha = jnp.exp2(m_prev - m_new)
            p = jnp.exp2(s - _lane_tile(m_new, tk // LANE))
            l_scr[rs] = alpha * l_scr[rs] + jnp.sum(p, axis=-1, keepdims=True)
            acc_scr[rs] = acc_scr[rs] * alpha + _mm(p.astype(BF16), v_ref[0, g, pl.ds(start, tk), :])
            m_scr[rs] = m_new

    def body(kj, carry):
        block(kj, False)
        return carry

    nfull = (qi * tq) // tk
    lax.fori_loop(0, nfull, body, 0)
    block(nfull, True)
    for g in range(hps):
        rs = slice(g * tq, (g + 1) * tq)
        yc_ref[0, :, g * C_VDIM:(g + 1) * C_VDIM] = (acc_scr[rs] * (1.0 / l_scr[rs])).astype(BF16)


def _attn_heads(q, k, kr2, v):
    bsz, _, L, _ = q.shape
    tq = min(L, 512)
    tk = min(L, 512)
    hps = 4
    rows = hps * tq
    return pl.pallas_call(
        functools.partial(_attn_heads_kernel, tq=tq, tk=tk, hps=hps),
        out_shape=jax.ShapeDtypeStruct((bsz, L, C_WIDTH), BF16),
        grid=(bsz, C_HEADS // hps, L // tq),
        in_specs=[pl.BlockSpec((1, hps, tq, QK_DIM), lambda b, g, i: (b, g, i, 0)),
                  pl.BlockSpec((1, hps, L, C_NOPE), lambda b, g, i: (b, g, 0, 0)),
                  pl.BlockSpec((1, L, LANE), lambda b, g, i: (b, 0, 0)),
                  pl.BlockSpec((1, hps, L, C_VDIM), lambda b, g, i: (b, g, 0, 0))],
        out_specs=pl.BlockSpec((1, tq, hps * C_VDIM), lambda b, g, i: (b, i, g)),
        scratch_shapes=[pltpu.VMEM((rows, LANE), F32), pltpu.VMEM((rows, LANE), F32),
                        pltpu.VMEM((rows, C_VDIM), F32)],
        compiler_params=_params(("arbitrary", "arbitrary", "arbitrary")),
        name="attn_heads",
    )(q, k, kr2, v)


def _attn_finish(acc, l, wuv_ref, yc_ref, tq):
    o = acc * (1.0 / l)
    for h in range(C_HEADS):
        oh = o[h * tq:(h + 1) * tq].astype(BF16)
        yc_ref[0, :, h * C_VDIM:(h + 1) * C_VDIM] = _mm(oh, wuv_ref[h]).astype(BF16)


def _attn_full_kernel(q_ref, cpast_ref, rpast_ref, knew_ref, wuv_ref, yc_ref, *, tq):
    rows = C_HEADS * tq
    q = q_ref[0].reshape(rows, KCAT)
    rpast = rpast_ref[0].astype(BF16)
    kpast = jnp.concatenate([cpast_ref[0].astype(BF16), rpast, rpast], axis=1)
    knew = knew_ref[0]
    s_past = _mm_nt(q, kpast)
    s_new = _mm_nt(q, knew)
    m = jnp.maximum(jnp.max(s_past, axis=-1, keepdims=True), jnp.max(s_new, axis=-1, keepdims=True))
    p_past = jnp.exp2(s_past - m)
    p_new = jnp.exp2(s_new - m)
    l = jnp.sum(p_past, axis=-1, keepdims=True) + jnp.sum(p_new, axis=-1, keepdims=True)
    acc = _mm(p_past.astype(BF16), kpast[:, 0:KV_LORA]) + _mm(p_new.astype(BF16), knew[:, 0:KV_LORA])
    _attn_finish(acc, l, wuv_ref, yc_ref, tq)


def _attn_full(q, ckv_past, kr_past, kcat, wuv, layer):
    bsz, _, tq, _ = q.shape
    npast = ckv_past.shape[2]
    return pl.pallas_call(
        functools.partial(_attn_full_kernel, tq=tq),
        out_shape=jax.ShapeDtypeStruct((bsz, tq, C_WIDTH), BF16),
        grid=(bsz,),
        in_specs=[pl.BlockSpec((1, C_HEADS, tq, KCAT), lambda b: (b, 0, 0, 0)),
                  pl.BlockSpec((None, 1, npast, KV_LORA), lambda b: (layer, b, 0, 0)),
                  pl.BlockSpec((None, 1, npast, C_ROPE), lambda b: (layer, b, 0, 0)),
                  pl.BlockSpec((1, tq, KCAT), lambda b: (b, 0, 0)),
                  pl.BlockSpec((C_HEADS, KV_LORA, C_VDIM), lambda b: (0, 0, 0))],
        out_specs=pl.BlockSpec((1, tq, C_WIDTH), lambda b: (b, 0, 0)),
        compiler_params=_params(("arbitrary",)),
        name="attn_full",
    )(q, ckv_past, kr_past, kcat, wuv)


def _outproj_kernel(x_ref, ya_ref, yb_ref, yc_ref, wo_ref, o_ref):
    tn = 512
    for j in range(0, D_MODEL, tn):
        acc = _mm(ya_ref[...], wo_ref[0:A_WIDTH, j:j + tn])
        acc = acc + _mm(yb_ref[...], wo_ref[A_WIDTH:A_WIDTH + B_WIDTH, j:j + tn])
        acc = acc + _mm(yc_ref[...], wo_ref[A_WIDTH + B_WIDTH:D_MODEL, j:j + tn])
        o_ref[:, j:j + tn] = x_ref[:, j:j + tn] + acc


def _outproj(x, ya, yb, yc, wo, layer):
    m = x.shape[0]
    tm = min(m, 256)
    row = lambda i: (i, 0)
    return pl.pallas_call(
        _outproj_kernel,
        out_shape=jax.ShapeDtypeStruct((m, D_MODEL), F32),
        grid=(m // tm,),
        in_specs=[pl.BlockSpec((tm, D_MODEL), row),
                  pl.BlockSpec((tm, A_WIDTH), row),
                  pl.BlockSpec((tm, B_WIDTH), row),
                  pl.BlockSpec((tm, C_WIDTH), row),
                  pl.BlockSpec((None, D_MODEL, D_MODEL), lambda i: (layer, 0, 0))],
        out_specs=pl.BlockSpec((tm, D_MODEL), row),
        compiler_params=_params(("arbitrary",)),
        name="outproj",
    )(x, ya, yb, yc, wo)


def _ffn_kernel(x_ref, nw_ref, wu_ref, wd_ref, fw_ref, o_ref, xn_scr, acc_scr, *, final_norm):
    f = pl.program_id(1)

    @pl.when(f == 0)
    def _():
        x = x_ref[...]
        ms = jnp.mean(x * x, axis=-1, keepdims=True)
        xn_scr[...] = (x * lax.rsqrt(ms + EPS) * nw_ref[...]).astype(BF16)
        acc_scr[...] = jnp.zeros_like(acc_scr)

    hid = jnp.maximum(_mm(xn_scr[...], wu_ref[...]), 0.0)
    acc_scr[...] += _mm((hid * hid).astype(BF16), wd_ref[...])

    @pl.when(f == pl.num_programs(1) - 1)
    def _():
        y = x_ref[...] + acc_scr[...]
        if final_norm:
            ms = jnp.mean(y * y, axis=-1, keepdims=True)
            y = y * lax.rsqrt(ms + EPS) * fw_ref[...]
        o_ref[...] = y


def _ffn(x, norm_w, wu, wd, final_w, final_norm, layer):
    m = x.shape[0]
    tm = min(m, 512)
    tf = 1024
    return pl.pallas_call(
        functools.partial(_ffn_kernel, final_norm=final_norm),
        out_shape=jax.ShapeDtypeStruct((m, D_MODEL), F32),
        grid=(m // tm, D_FF // tf),
        in_specs=[pl.BlockSpec((tm, D_MODEL), lambda i, f: (i, 0)),
                  pl.BlockSpec((1, D_MODEL), lambda i, f: (0, 0)),
                  pl.BlockSpec((None, D_MODEL, tf), lambda i, f: (layer, 0, f)),
                  pl.BlockSpec((None, tf, D_MODEL), lambda i, f: (layer, f, 0)),
                  pl.BlockSpec((1, D_MODEL), lambda i, f: (0, 0))],
        out_specs=pl.BlockSpec((tm, D_MODEL), lambda i, f: (i, 0)),
        scratch_shapes=[pltpu.VMEM((tm, D_MODEL), BF16), pltpu.VMEM((tm, D_MODEL), F32)],
        compiler_params=_params(("arbitrary", "arbitrary")),
        name="ffn",
    )(x, norm_w.reshape(1, D_MODEL), wu, wd, final_w.reshape(1, D_MODEL))


def _swap_halves(w):
    half = C_ROPE // 2
    return jnp.concatenate([w[..., half:], w[..., :half]], axis=-1)


def _prep_w_in(w_in):
    w_in = w_in.astype(BF16)
    lead = w_in.shape[:-1]
    ckr = w_in[..., OFF_CKR:IN_COLS]
    g1 = jnp.concatenate([ckr, w_in[..., OFF_BBETA:OFF_CQ], jnp.zeros(lead + (LANE - C_ROPE - 2 * B_HEADS,), BF16)],
                         axis=-1)
    g2 = jnp.concatenate([_swap_halves(ckr), jnp.zeros(lead + (LANE - C_ROPE,), BF16)], axis=-1)
    parts = [w_in[..., OFF_BQKV:OFF_BZ], w_in[..., OFF_AU:OFF_AV], w_in[..., OFF_AV:OFF_BQKV],
             w_in[..., OFF_BZ:OFF_BBETA], w_in[..., OFF_CQ:OFF_CKV], w_in[..., OFF_CKV:OFF_CKR], g1, g2]
    return jnp.concatenate(parts, axis=-1)


def _rope_tables(pos):
    half = C_ROPE // 2
    inv = ROPE_THETA ** (-jnp.arange(half, dtype=F32) / half)
    ang = pos.astype(F32)[:, None] * inv[None, :]
    cos, sin = jnp.cos(ang), jnp.sin(ang)
    cos2 = jnp.concatenate([cos, cos, cos, cos], axis=1)
    sin2 = jnp.concatenate([-sin, sin, -sin, sin], axis=1)
    return cos2, sin2


def _layer(x, bsz, L, tabs, lw, past, final_w, final_norm, layer, depth, stacks):
    proj = _inproj(x, lw['attn_norm_w'], lw['w_in'], layer).reshape(bsz, L, P_COLS)

    ya, va = _mixer_a(proj, lw['a_ln_w'], lw['a_ln_b'], lw['a_w_s'], lw['a_b_s'], past is not None)

    if past is None:
        rows, chunk = 2 * CHUNK, CHUNK
        buf8 = jnp.zeros((bsz, 8, B_QKV), F32)
        s0 = jnp.zeros((bsz, B_HEADS, B_DK, B_DV), F32)
    else:
        rows, chunk = L, L
        ckv_past, kr_past, s0, conv_buf = past
        buf8 = jnp.concatenate([jnp.zeros((bsz, 8 - (B_CONV - 1), B_QKV), F32), conv_buf], axis=1)
    prep = _gdn_prep(proj, buf8, lw['b_conv_w'], lw['b_a_log'], lw['b_dt_bias'], rows, chunk)
    yb, s_new = _gdn_scan(prep, proj, lw['b_norm_w'], s0, rows, chunk)
    buf_new = proj[:, L - (B_CONV - 1):, P_BQKV:P_BQKV + B_QKV]

    cos8, sin8 = tabs
    mla_args = (proj, cos8, sin8, lw['c_q_norm_w'], lw['wqn'], lw['wqr'], lw['wqs'], lw['c_kv_norm_w'])
    if past is None:
        ckv, kr, q, k, kr2, v = _mla_proj(*mla_args, (lw['wk'], lw['wv']), False, layer, depth, stacks)
        yc = _attn_heads(q, k, kr2, v)
    else:
        ckv, kr, kcat, q = _mla_proj(*mla_args, (lw['wukt'],), True, layer, depth, stacks)
        yc = _attn_full(q, ckv_past, kr_past, kcat, lw['wuv'], layer)

    m = bsz * L
    x = _outproj(x, ya.reshape(m, A_WIDTH), yb.reshape(m, B_WIDTH), yc.reshape(m, C_WIDTH), lw['w_out'], layer)
    x = _ffn(x, lw['mlp_norm_w'], lw['w_up'], lw['w_down'], final_w, final_norm, layer)
    return x, (ckv, kr, s_new, buf_new, va)


def kernel(x_prompt, x_sample, cache_mla_ckv, cache_mla_krope, state_gdn, state_gdn_conv,
           attn_norm_w, w_in, a_ln_w, a_ln_b, a_w_s, a_b_s, b_conv_w, b_a_log, b_dt_bias, b_norm_w,
           c_q_norm_w, c_w_uq, c_kv_norm_w, c_w_uk, c_w_uv, w_out, mlp_norm_w, w_up, w_down, final_norm_w):
    depth = w_in.shape[0]
    bp, lp, _ = x_prompt.shape
    bs, ls, _ = x_sample.shape
    past_len = cache_mla_ckv.shape[2]
    tabs_p = _rope_tables(jnp.arange(lp, dtype=jnp.int32))
    tabs_s = _rope_tables(past_len + jnp.arange(ls, dtype=jnp.int32))
    hp = x_prompt.reshape(bp * lp, D_MODEL)
    hs = x_sample.reshape(bs * ls, D_MODEL)
    sp_list, ss_list = [], []
    stacks_p = stacks_s = None
    w_in16, w_out16, w_up16, w_down16 = _prep_w_in(w_in), w_out.astype(BF16), w_up.astype(BF16), w_down.astype(BF16)
    for l in range(depth):
        wq_rope = c_w_uq[l][:, :, C_NOPE:]
        lw = {
            'attn_norm_w': attn_norm_w[l], 'w_in': w_in16,
            'a_ln_w': a_ln_w[l], 'a_ln_b': a_ln_b[l], 'a_w_s': a_w_s[l], 'a_b_s': a_b_s[l],
            'b_conv_w': b_conv_w[l], 'b_a_log': b_a_log[l], 'b_dt_bias': b_dt_bias[l], 'b_norm_w': b_norm_w[l],
            'c_q_norm_w': c_q_norm_w[l], 'c_kv_norm_w': c_kv_norm_w[l],
            'wqn': c_w_uq[l][:, :, :C_NOPE].reshape(Q_LORA, C_HEADS * C_NOPE).astype(BF16),
            'wqr': wq_rope.reshape(Q_LORA, C_HEADS * C_ROPE).astype(BF16),
            'wqs': _swap_halves(wq_rope).reshape(Q_LORA, C_HEADS * C_ROPE).astype(BF16),
            'wk': c_w_uk[l].reshape(KV_LORA, C_HEADS * C_NOPE).astype(BF16),
            'wv': c_w_uv[l].reshape(KV_LORA, C_HEADS * C_VDIM).astype(BF16),
            'wukt': jnp.transpose(c_w_uk[l], (1, 2, 0)).astype(BF16),
            'wuv': jnp.transpose(c_w_uv[l], (1, 0, 2)).astype(BF16),
            'w_out': w_out16, 'mlp_norm_w': mlp_norm_w[l], 'w_up': w_up16, 'w_down': w_down16,
        }
        last = l == depth - 1
        hp, sp = _layer(hp, bp, lp, tabs_p, lw, None, final_norm_w, last, l, depth, stacks_p)
        hs, ss = _layer(hs, bs, ls, tabs_s, lw,
                        (cache_mla_ckv, cache_mla_krope, state_gdn[l], state_gdn_conv[l]), final_norm_w, last,
                        l, depth, stacks_s)
        stacks_p, stacks_s = sp[:2], ss[:2]
        sp_list.append(sp)
        ss_list.append(ss)

    def stack(lst, i):
        return jnp.stack([s[i] for s in lst], axis=0)

    return (hp.reshape(bp, lp, D_MODEL), hs.reshape(bs, ls, D_MODEL),
            stacks_p[0], stacks_p[1], stack(sp_list, 2), stack(sp_list, 3),
            stacks_s[0], stacks_s[1], stack(ss_list, 2), stack(ss_list, 3), stack(ss_list, 4))
```

```python
import functools
import math

import jax
import jax.numpy as jnp
from jax import lax
from jax.experimental import pallas as pl
from jax.experimental.pallas import tpu as pltpu

F32 = jnp.float32
BF16 = jnp.bfloat16

D_MODEL = 2048
CHUNK = 64
EPS = 1e-6
LN_EPS = 1e-5
A_GROUPS = 4
A_HEAD = 128
A_WIDTH = 512
A_CHUNK = 128
B_HEADS = 4
B_DK = 128
B_DV = 128
B_WIDTH = 512
B_QKV = 1536
B_CONV = 4
C_HEADS = 8
C_NOPE = 128
C_ROPE = 64
C_VDIM = 128
C_WIDTH = 1024
Q_LORA = 512
KV_LORA = 256
ROPE_THETA = 10000.0
C_SCALE = (C_NOPE + C_ROPE) ** -0.5
Q_SCALE = C_SCALE * math.log2(math.e)
D_FF = 4 * D_MODEL

OFF_AU = 0
OFF_AV = OFF_AU + A_WIDTH
OFF_BQKV = OFF_AV + A_WIDTH
OFF_BZ = OFF_BQKV + B_QKV
OFF_BBETA = OFF_BZ + B_WIDTH
OFF_BALPHA = OFF_BBETA + B_HEADS
OFF_CQ = OFF_BALPHA + B_HEADS
OFF_CKV = OFF_CQ + Q_LORA
OFF_CKR = OFF_CKV + KV_LORA
IN_COLS = OFF_CKR + C_ROPE

P_AU = OFF_AU
P_AV = OFF_AV
P_BQKV = OFF_BQKV
P_BZ = OFF_BZ
P1_COLS = OFF_BBETA
P_CQ = 0
P_CKV = 512
P_G1 = 768
P_G2 = 896
P2_COLS = 1024
LANE = 128
G1_BETA = C_ROPE
G1_ALPHA = C_ROPE + B_HEADS
KCAT = KV_LORA + 2 * C_ROPE
QK_DIM = C_NOPE + 2 * C_ROPE

VMEM_LIMIT = 56 * 1024 * 1024


def _mm(a, b):
    return jnp.dot(a, b, preferred_element_type=F32)


def _mm_nt(a, b):
    return lax.dot_general(a, b, (((1,), (1,)), ((), ())), preferred_element_type=F32)


def _split2(x):
    hi = x.astype(BF16)
    lo = (x - hi.astype(F32)).astype(BF16)
    return hi, lo


def _split3(x):
    hi = x.astype(BF16)
    r = x - hi.astype(F32)
    mid = r.astype(BF16)
    lo = (r - mid.astype(F32)).astype(BF16)
    return hi, mid, lo


def _cat3_lhs(a):
    ah, al = _split2(a)
    return jnp.concatenate([ah, ah, al], axis=1)


def _cat3_rhs(b):
    bh, bl = _split2(b)
    return jnp.concatenate([bh, bl, bh], axis=0)


def _mm3(a, b):
    return _mm(_cat3_lhs(a), _cat3_rhs(b))


def _mm_exact_lhs(a, b):
    return _mm(jnp.concatenate([a, a, a], axis=1), jnp.concatenate(_split3(b), axis=0))


def _mm_nt_exact_lhs(a, b):
    return _mm_nt(jnp.concatenate([a, a, a], axis=1), jnp.concatenate(_split3(b), axis=1))


def _lane_tile(x, n):
    return x if n == 1 else jnp.concatenate([x] * n, axis=1)


def _params(sem):
    return pltpu.CompilerParams(dimension_semantics=sem, vmem_limit_bytes=VMEM_LIMIT)


def _inproj_kernel(x_ref, nw_ref, w_ref, o_ref):
    x = x_ref[...]
    ms = jnp.mean(x * x, axis=-1, keepdims=True)
    xn = (x * lax.rsqrt(ms + EPS) * nw_ref[...]).astype(BF16)
    o_ref[...] = _mm(xn, w_ref[...])


def _inproj(x, norm_w, w_pad, layer, ncols):
    m = x.shape[0]
    tm = min(m, 512)
    tn = 1024
    return pl.pallas_call(
        _inproj_kernel,
        out_shape=jax.ShapeDtypeStruct((m, ncols), F32),
        grid=(ncols // tn, m // tm),
        in_specs=[pl.BlockSpec((tm, D_MODEL), lambda j, i: (i, 0)),
                  pl.BlockSpec((1, D_MODEL), lambda j, i: (0, 0)),
                  pl.BlockSpec((None, D_MODEL, tn), lambda j, i: (layer, 0, j))],
        out_specs=pl.BlockSpec((tm, tn), lambda j, i: (i, j)),
        compiler_params=_params(("arbitrary", "arbitrary")),
        name="inproj",
    )(x, norm_w.reshape(1, D_MODEL), w_pad)


def _gelu(x):
    return 0.5 * x * (1.0 + lax.erf(x * (2.0 ** -0.5)))


def _mixa_kernel(u_ref, v_ref, lnw_ref, lnb_ref, ws_ref, bs_ref, ya_ref, *va_ref, lc, nsub):
    u = _gelu(u_ref[0])
    v = _gelu(v_ref[0])
    mu = jnp.mean(v, axis=-1, keepdims=True)
    vc = v - mu
    var = jnp.mean(vc * vc, axis=-1, keepdims=True)
    vn = vc * lax.rsqrt(var + LN_EPS) * lnw_ref[...] + lnb_ref[...]
    if va_ref:
        va_ref[0][0] = vn
    for s in range(nsub):
        rows = slice(s * lc, (s + 1) * lc)
        for g in range(A_GROUPS):
            cols = slice(g * A_HEAD, (g + 1) * A_HEAD)
            f = _mm(ws_ref[g], vn[rows, cols].astype(BF16)) + bs_ref[g]
            ya_ref[0, rows, cols] = (u[rows, cols] * f).astype(BF16)


def _mixer_a(proj, ln_w, ln_b, w_s, b_s, want_va):
    bsz, L, _ = proj.shape
    lc = min(L, A_CHUNK)
    rows = min(L, 512)
    nsub = rows // lc
    idx = jnp.arange(lc)
    allowed = (idx[:, None] // CHUNK) >= (idx[None, :] // CHUNK)
    ws = jnp.where(allowed[None], w_s[:, :lc, :lc], 0.0).astype(BF16)
    bs = jnp.broadcast_to(b_s[:, :lc, None], (A_GROUPS, lc, A_HEAD)).astype(F32)
    out_shape = [jax.ShapeDtypeStruct((bsz, L, A_WIDTH), BF16)]
    out_specs = [pl.BlockSpec((1, rows, A_WIDTH), lambda b, i: (b, i, 0))]
    if want_va:
        out_shape.append(jax.ShapeDtypeStruct((bsz, L, A_WIDTH), F32))
        out_specs.append(pl.BlockSpec((1, rows, A_WIDTH), lambda b, i: (b, i, 0)))
    res = pl.pallas_call(
        functools.partial(_mixa_kernel, lc=lc, nsub=nsub),
        out_shape=out_shape,
        grid=(bsz, L // rows),
        in_specs=[pl.BlockSpec((1, rows, A_WIDTH), lambda b, i: (b, i, P_AU // A_WIDTH)),
                  pl.BlockSpec((1, rows, A_WIDTH), lambda b, i: (b, i, P_AV // A_WIDTH)),
                  pl.BlockSpec((1, A_WIDTH), lambda b, i: (0, 0)),
                  pl.BlockSpec((1, A_WIDTH), lambda b, i: (0, 0)),
                  pl.BlockSpec((A_GROUPS, lc, lc), lambda b, i: (0, 0, 0)),
                  pl.BlockSpec((A_GROUPS, lc, A_HEAD), lambda b, i: (0, 0, 0))],
        out_specs=out_specs,
        compiler_params=_params(("arbitrary", "arbitrary")),
        name="mixer_a",
    )(proj, proj, ln_w.reshape(1, A_WIDTH), ln_b.reshape(1, A_WIDTH), ws, bs)
    return res if want_va else (res[0], None)


def _softplus(x):
    return jnp.maximum(x, 0.0) + jnp.log1p(jnp.exp(-jnp.abs(x)))


def _gdn_prep_kernel(xq_ref, xk_ref, xv_ref, hq_ref, hk_ref, hv_ref, buf_ref, g1_ref, cw_ref, alog_ref, dtb_ref,
                     u_ref, w_ref, qg_ref, kgt_ref, qk_ref, eg_ref, xp_scr, *, nb, rows, chunk):
    i = pl.program_id(1)
    nsub = rows // chunk
    ri = lax.broadcasted_iota(jnp.int32, (rows, rows), 0)
    ci = lax.broadcasted_iota(jnp.int32, (rows, rows), 1)
    same = (ri // chunk) == (ci // chunk)
    causal = same & (ri >= ci)
    strict = same & (ri > ci)
    tri = jnp.where(causal, 1.0, 0.0).astype(BF16)
    eye_r = jnp.where(ri == ci, 1.0, 0.0).astype(F32)
    li = lax.broadcasted_iota(jnp.int32, (LANE, LANE), 0)
    lj = lax.broadcasted_iota(jnp.int32, (LANE, LANE), 1)
    eye_l = jnp.where(li == lj, 1.0, 0.0).astype(BF16)

    ys, betas, gts = [], [], []
    for bb in range(nb):
        x = jnp.concatenate([xq_ref[bb], xk_ref[bb], xv_ref[bb]], axis=1)
        halo = jnp.concatenate([hq_ref[bb], hk_ref[bb], hv_ref[bb]], axis=1)
        xp_scr[bb, 0:8, :] = jnp.where(i == 0, buf_ref[bb], halo)
        xp_scr[bb, 8:8 + rows, :] = x
        y = x * cw_ref[B_CONV - 1:B_CONV, :]
        for k in range(1, B_CONV):
            y = y + xp_scr[bb, 8 - k:8 - k + rows, :] * cw_ref[B_CONV - 1 - k:B_CONV - k, :]
        ys.append(y * jax.nn.sigmoid(y))
        g1 = g1_ref[bb]
        betas.append(jax.nn.sigmoid(g1))
        gts.append(-jnp.exp(alog_ref[...]) * _softplus(g1 + dtb_ref[...]))

    gcs = [_mm_exact_lhs(tri, g) for g in gts]
    gcts = [_mm_nt_exact_lhs(eye_l, gc) for gc in gcs]
    glasts = [jnp.concatenate(
        [jnp.broadcast_to(gc[(s + 1) * chunk - 1:(s + 1) * chunk, :], (chunk, LANE)) for s in range(nsub)], axis=0)
        for gc in gcs]

    pairs = [(bb, h) for bb in range(nb) for h in range(B_HEADS)]
    qs, ks, kbs, rhss, decs, egcs = [], [], [], [], [], []
    for bb, h in pairs:
        y, gc = ys[bb], gcs[bb]
        bcol = betas[bb][:, G1_BETA + h:G1_BETA + h + 1]
        gcol = gc[:, G1_ALPHA + h:G1_ALPHA + h + 1]
        grow = gcts[bb][G1_ALPHA + h:G1_ALPHA + h + 1, :]
        qh = y[:, h * B_DK:(h + 1) * B_DK]
        qh = qh * lax.rsqrt(jnp.sum(qh * qh, axis=-1, keepdims=True) + EPS) * (B_DK ** -0.5)
        kh = y[:, B_WIDTH + h * B_DK:B_WIDTH + (h + 1) * B_DK]
        kh = kh * lax.rsqrt(jnp.sum(kh * kh, axis=-1, keepdims=True) + EPS)
        vh = y[:, 2 * B_WIDTH + h * B_DV:2 * B_WIDTH + (h + 1) * B_DV]
        kb = kh * bcol
        egc = jnp.exp(gcol)
        qs.append(qh)
        ks.append(kh)
        kbs.append(kb)
        egcs.append(egc)
        rhss.append(jnp.concatenate([vh * bcol, kb * egc], axis=1))
        decs.append(jnp.where(causal, jnp.exp(jnp.minimum(gcol - grow, 0.0)), 0.0))

    khb = [k.astype(BF16) for k in ks]
    kk = [_mm_nt(kb.astype(BF16), kx) for kb, kx in zip(kbs, khb)]
    qkm = [_mm_nt(q.astype(BF16), kx) for q, kx in zip(qs, khb)]

    ps = [jnp.where(strict, -(a * d), 0.0) for a, d in zip(kk, decs)]
    ts = [eye_r + p for p in ps]
    ms = [_mm(_cat3_lhs(p), _cat3_rhs(p)) for p in ps]
    nsq = int(math.log2(chunk)) - 1
    for step in range(nsq):
        mr = [_cat3_rhs(m) for m in ms]
        upd = [_mm(_cat3_lhs(t), b) for t, b in zip(ts, mr)]
        if step + 1 < nsq:
            ms = [_mm(_cat3_lhs(m), b) for m, b in zip(ms, mr)]
        ts = [t + d for t, d in zip(ts, upd)]
    sols = [_mm3(t, r) for t, r in zip(ts, rhss)]

    for n, (bb, h) in enumerate(pairs):
        hs = slice(h * B_DK, (h + 1) * B_DK)
        gcol = gcs[bb][:, G1_ALPHA + h:G1_ALPHA + h + 1]
        glcol = glasts[bb][:, G1_ALPHA + h:G1_ALPHA + h + 1]
        u_ref[bb, :, hs] = sols[n][:, :B_DV]
        w_ref[bb, :, hs] = sols[n][:, B_DV:].astype(BF16)
        qg_ref[bb, :, hs] = (qs[n] * egcs[n]).astype(BF16)
        kg = (ks[n] * jnp.exp(glcol - gcol)).astype(BF16)
        kgt_ref[bb, hs, :] = _mm_nt(eye_l, kg).astype(BF16)
        qk_ref[bb, :, h * rows:(h + 1) * rows] = (qkm[n] * decs[n]).astype(BF16)
        for s in range(nsub):
            eg_ref[bb, s, h:h + 1, :] = jnp.broadcast_to(
                jnp.exp(glasts[bb][s * chunk:s * chunk + 1, G1_ALPHA + h:G1_ALPHA + h + 1]), (1, LANE))


def _gdn_prep(proj, proj2, buf8, conv_w, a_log, dt_bias, rows, chunk):
    bsz, L, _ = proj.shape
    nb = 2
    nblk = L // rows
    rpb = max(rows // 8, 1)
    alog = jnp.zeros((1, LANE), F32).at[0, G1_ALPHA:G1_ALPHA + B_HEADS].set(a_log)
    dtb = jnp.zeros((1, LANE), F32).at[0, G1_ALPHA:G1_ALPHA + B_HEADS].set(dt_bias)
    return pl.pallas_call(
        functools.partial(_gdn_prep_kernel, nb=nb, rows=rows, chunk=chunk),
        out_shape=[jax.ShapeDtypeStruct((bsz, L, B_WIDTH), F32),
                   jax.ShapeDtypeStruct((bsz, L, B_WIDTH), BF16),
                   jax.ShapeDtypeStruct((bsz, L, B_WIDTH), BF16),
                   jax.ShapeDtypeStruct((bsz, B_WIDTH, L), BF16),
                   jax.ShapeDtypeStruct((bsz, L, B_HEADS * rows), BF16),
                   jax.ShapeDtypeStruct((bsz, L // chunk, B_HEADS, LANE), F32)],
        grid=(bsz // nb, nblk),
        in_specs=[pl.BlockSpec((nb, rows, B_WIDTH), lambda b, i, c=P_BQKV // B_WIDTH + part: (b, i, c))
                  for part in range(3)] + [
                  pl.BlockSpec((nb, 8, B_WIDTH),
                               lambda b, i, c=P_BQKV // B_WIDTH + part: (b, jnp.maximum(i * rpb - 1, 0), c))
                  for part in range(3)] + [
                  pl.BlockSpec((nb, 8, B_QKV), lambda b, i: (b, 0, 0)),
                  pl.BlockSpec((nb, rows, LANE), lambda b, i: (b, i, P_G1 // LANE)),
                  pl.BlockSpec((B_CONV, B_QKV), lambda b, i: (0, 0)),
                  pl.BlockSpec((1, LANE), lambda b, i: (0, 0)),
                  pl.BlockSpec((1, LANE), lambda b, i: (0, 0))],
        out_specs=[pl.BlockSpec((nb, rows, B_WIDTH), lambda b, i: (b, i, 0)),
                   pl.BlockSpec((nb, rows, B_WIDTH), lambda b, i: (b, i, 0)),
                   pl.BlockSpec((nb, rows, B_WIDTH), lambda b, i: (b, i, 0)),
                   pl.BlockSpec((nb, B_WIDTH, rows), lambda b, i: (b, 0, i)),
                   pl.BlockSpec((nb, rows, B_HEADS * rows), lambda b, i: (b, i, 0)),
                   pl.BlockSpec((nb, rows // chunk, B_HEADS, LANE), lambda b, i: (b, i, 0, 0))],
        scratch_shapes=[pltpu.VMEM((nb, rows + 8, B_QKV), F32)],
        compiler_params=_params(("arbitrary", "arbitrary")),
        name="gdn_prep",
    )(proj, proj, proj, proj, proj, proj, buf8, proj2, conv_w, alog, dtb)


def _gdn_scan_kernel(u_ref, w_ref, qg_ref, kgt_ref, qk_ref, eg_ref, z_ref, nw_ref, s0_ref,
                     yb_ref, s_ref, *, bsz, rows, chunk):
    @pl.when(pl.program_id(0) == 0)
    def _():
        s_ref[...] = s0_ref[...]

    nsub = rows // chunk
    pairs = [(b, h) for b in range(bsz) for h in range(B_HEADS)]
    st = [s_ref[b, h] for b, h in pairs]
    for s in range(nsub):
        rs = slice(s * chunk, (s + 1) * chunk)
        r1 = [_mm(jnp.concatenate([w_ref[b, rs, h * B_DV:(h + 1) * B_DV], qg_ref[b, rs, h * B_DV:(h + 1) * B_DV]],
                                  axis=0), x.astype(BF16))
              for (b, h), x in zip(pairs, st)]
        vb = [(u_ref[b, rs, h * B_DV:(h + 1) * B_DV] - r[:chunk]).astype(BF16) for (b, h), r in zip(pairs, r1)]
        r2 = [_mm(jnp.concatenate([qk_ref[b, rs, h * rows + s * chunk:h * rows + (s + 1) * chunk],
                                   kgt_ref[b, h * B_DK:(h + 1) * B_DK, rs]], axis=0), v)
              for (b, h), v in zip(pairs, vb)]
        for n, (b, h) in enumerate(pairs):
            hs = slice(h * B_DV, (h + 1) * B_DV)
            o = r1[n][chunk:] + r2[n][:chunk]
            st[n] = st[n] * eg_ref[b, s, h:h + 1, :] + r2[n][chunk:]
            z = z_ref[b, rs, hs]
            on = o * lax.rsqrt(jnp.mean(o * o, axis=-1, keepdims=True) + EPS) * nw_ref[...]
            yb_ref[b, rs, hs] = (on * (z * jax.nn.sigmoid(z))).astype(BF16)
    for n, (b, h) in enumerate(pairs):
        s_ref[b, h] = st[n]


def _gdn_scan(prep, proj, norm_w, s0, rows, chunk):
    u, w, qg, kgt, qk, eg = prep
    bsz, L, _ = u.shape
    nblk = L // rows
    return pl.pallas_call(
        functools.partial(_gdn_scan_kernel, bsz=bsz, rows=rows, chunk=chunk),
        out_shape=[jax.ShapeDtypeStruct((bsz, L, B_WIDTH), BF16),
                   jax.ShapeDtypeStruct((bsz, B_HEADS, B_DK, B_DV), F32)],
        grid=(nblk,),
        in_specs=[pl.BlockSpec((bsz, rows, B_WIDTH), lambda i: (0, i, 0)),
                  pl.BlockSpec((bsz, rows, B_WIDTH), lambda i: (0, i, 0)),
                  pl.BlockSpec((bsz, rows, B_WIDTH), lambda i: (0, i, 0)),
                  pl.BlockSpec((bsz, B_WIDTH, rows), lambda i: (0, 0, i)),
                  pl.BlockSpec((bsz, rows, B_HEADS * rows), lambda i: (0, i, 0)),
                  pl.BlockSpec((bsz, rows // chunk, B_HEADS, LANE), lambda i: (0, i, 0, 0)),
                  pl.BlockSpec((bsz, rows, B_WIDTH), lambda i: (0, i, P_BZ // B_WIDTH)),
                  pl.BlockSpec((1, B_DV), lambda i: (0, 0)),
                  pl.BlockSpec((bsz, B_HEADS, B_DK, B_DV), lambda i: (0, 0, 0, 0))],
        out_specs=[pl.BlockSpec((bsz, rows, B_WIDTH), lambda i: (0, i, 0)),
                   pl.BlockSpec((bsz, B_HEADS, B_DK, B_DV), lambda i: (0, 0, 0, 0))],
        compiler_params=_params(("arbitrary",)),
        name="gdn_scan",
    )(u, w, qg, kgt, qk, eg, proj, norm_w.reshape(1, B_DV), s0)


def _mla_common(cq_ref, ckv_ref, g1_ref, g2_ref, cos8_ref, sin8_ref, qnw_ref, wqn_ref, wqr_ref, wqs_ref, kvnw_ref):
    cq = cq_ref[0]
    cqn = (cq * lax.rsqrt(jnp.mean(cq * cq, axis=-1, keepdims=True) + EPS) * qnw_ref[...]).astype(BF16)
    c = ckv_ref[0]
    ckv = c * lax.rsqrt(jnp.mean(c * c, axis=-1, keepdims=True) + EPS) * kvnw_ref[...]
    cos2, sin2 = cos8_ref[...], sin8_ref[...]
    kr = g1_ref[0][:, 0:C_ROPE] * cos2[:, 0:C_ROPE] + g2_ref[0][:, 0:C_ROPE] * sin2[:, 0:C_ROPE]
    qn = _mm(cqn, wqn_ref[...])
    ntile = C_HEADS * C_ROPE // LANE
    qr = (_mm(cqn, wqr_ref[...]) * _lane_tile(cos2, ntile) + _mm(cqn, wqs_ref[...]) * _lane_tile(sin2, ntile)) * Q_SCALE
    return ckv, kr, qn, qr


def _rope_slot(qr, h):
    lane = lax.broadcasted_iota(jnp.int32, (1, LANE), 1)
    pair = qr[:, (h // 2) * LANE:(h // 2 + 1) * LANE]
    return jnp.where((lane // C_ROPE) == (h % 2), pair, 0.0).astype(BF16)


def _write_stacked(out_ref, val, layer, prev_ref):
    for d in range(out_ref.shape[0]):
        if d == layer:
            out_ref[d] = val
        elif prev_ref is None:
            out_ref[d] = jnp.zeros_like(val)
        else:
            out_ref[d] = prev_ref[d]


def _mla_proj_kernel(cq_ref, ckv_ref, g1_ref, g2_ref, cos8_ref, sin8_ref, qnw_ref, wqn_ref, wqr_ref, wqs_ref,
                     kvnw_ref, wukt_ref, ckv_out, kr_out, kcat_out, q_out, *, layer, prev):
    ckv, kr, qn, qr = _mla_common(cq_ref, ckv_ref, g1_ref, g2_ref, cos8_ref, sin8_ref, qnw_ref, wqn_ref, wqr_ref,
                                  wqs_ref, kvnw_ref)
    _write_stacked(ckv_out, ckv, layer, prev and prev[0])
    _write_stacked(kr_out, kr, layer, prev and prev[1])
    kcat_out[0, :, 0:KV_LORA] = ckv.astype(BF16)
    kcat_out[0, :, KV_LORA:KV_LORA + C_ROPE] = kr.astype(BF16)
    kcat_out[0, :, KV_LORA + C_ROPE:KCAT] = kr.astype(BF16)
    for h in range(C_HEADS):
        ql = _mm(qn[:, h * C_NOPE:(h + 1) * C_NOPE].astype(BF16), wukt_ref[h]) * Q_SCALE
        q_out[0, h, :, 0:KV_LORA] = ql.astype(BF16)
        q_out[0, h, :, KV_LORA:KCAT] = _rope_slot(qr, h)


def _mla_heads_kernel(cq_ref, ckv_ref, g1_ref, g2_ref, cos8_ref, sin8_ref, qnw_ref, wqn_ref, wqr_ref, wqs_ref,
                      kvnw_ref, wk_ref, wv_ref, ckv_out, kr_out, q_out, k_out, kr2_out, v_out, *, layer, prev):
    ckv, kr, qn, qr = _mla_common(cq_ref, ckv_ref, g1_ref, g2_ref, cos8_ref, sin8_ref, qnw_ref, wqn_ref, wqr_ref,
                                  wqs_ref, kvnw_ref)
    _write_stacked(ckv_out, ckv, layer, prev and prev[0])
    _write_stacked(kr_out, kr, layer, prev and prev[1])
    ckvb = ckv.astype(BF16)
    kn = _mm(ckvb, wk_ref[...])
    vv = _mm(ckvb, wv_ref[...])
    krb = kr.astype(BF16)
    kr2_out[0] = jnp.concatenate([krb, krb], axis=1)
    for h in range(C_HEADS):
        hs = slice(h * C_NOPE, (h + 1) * C_NOPE)
        q_out[0, h, :, 0:C_NOPE] = (qn[:, hs] * Q_SCALE).astype(BF16)
        q_out[0, h, :, C_NOPE:QK_DIM] = _rope_slot(qr, h)
        k_out[0, h] = kn[:, hs].astype(BF16)
        v_out[0, h] = vv[:, h * C_VDIM:(h + 1) * C_VDIM].astype(BF16)


def _mla_proj(proj, cos8, sin8, q_norm_w, wqn, wqr, wqs, kv_norm_w, head_w, absorbed, layer, depth, stacks):
    bsz, L, _ = proj.shape
    tm = min(L, 256)
    const2 = lambda b, i: (0, 0)
    head_out = lambda width: (jax.ShapeDtypeStruct((bsz, C_HEADS, L, width), BF16),
                              pl.BlockSpec((1, C_HEADS, tm, width), lambda b, i: (b, 0, i, 0)))
    stack_out = lambda width: (jax.ShapeDtypeStruct((depth, bsz, L, width), F32),
                               pl.BlockSpec((depth, None, tm, width), lambda b, i: (0, b, i, 0)))
    outs = [stack_out(KV_LORA), stack_out(C_ROPE)]
    if absorbed:
        body = _mla_proj_kernel
        w_specs = [pl.BlockSpec((C_HEADS, C_NOPE, KV_LORA), lambda b, i: (0, 0, 0))]
        outs += [(jax.ShapeDtypeStruct((bsz, L, KCAT), BF16), pl.BlockSpec((1, tm, KCAT), lambda b, i: (b, i, 0))),
                 head_out(KCAT)]
    else:
        body = _mla_heads_kernel
        w_specs = [pl.BlockSpec((KV_LORA, C_HEADS * C_NOPE), const2), pl.BlockSpec((KV_LORA, C_HEADS * C_VDIM), const2)]
        outs += [head_out(QK_DIM), head_out(C_NOPE),
                 (jax.ShapeDtypeStruct((bsz, L, LANE), BF16), pl.BlockSpec((1, tm, LANE), lambda b, i: (b, i, 0))),
                 head_out(C_VDIM)]
    if stacks is None:
        kern, prev_specs, prev_args = functools.partial(body, layer=layer, prev=None), [], ()
    else:
        kern = lambda ckv_prev, kr_prev, *refs: body(*refs, layer=layer, prev=(ckv_prev, kr_prev))
        prev_specs, prev_args = [outs[0][1], outs[1][1]], tuple(stacks)
    return pl.pallas_call(
        kern,
        out_shape=[o[0] for o in outs],
        grid=(bsz, L // tm),
        in_specs=prev_specs + [pl.BlockSpec((1, tm, Q_LORA), lambda b, i: (b, i, P_CQ // Q_LORA)),
                  pl.BlockSpec((1, tm, KV_LORA), lambda b, i: (b, i, P_CKV // KV_LORA)),
                  pl.BlockSpec((1, tm, LANE), lambda b, i: (b, i, P_G1 // LANE)),
                  pl.BlockSpec((1, tm, LANE), lambda b, i: (b, i, P_G2 // LANE)),
                  pl.BlockSpec((tm, LANE), lambda b, i: (i, 0)),
                  pl.BlockSpec((tm, LANE), lambda b, i: (i, 0)),
                  pl.BlockSpec((1, Q_LORA), const2),
                  pl.BlockSpec((Q_LORA, C_HEADS * C_NOPE), const2),
                  pl.BlockSpec((Q_LORA, C_HEADS * C_ROPE), const2),
                  pl.BlockSpec((Q_LORA, C_HEADS * C_ROPE), const2),
                  pl.BlockSpec((1, KV_LORA), const2)] + w_specs,
        out_specs=[o[1] for o in outs],
        compiler_params=_params(("arbitrary", "arbitrary")),
        name="mla_proj" if absorbed else "mla_heads",
    )(*prev_args, proj, proj, proj, proj, cos8, sin8, q_norm_w.reshape(1, Q_LORA), wqn, wqr, wqs,
      kv_norm_w.reshape(1, KV_LORA), *head_w)


def _attn_heads_kernel(q_ref, k_ref, kr2_ref, v_ref, yc_ref, m_scr, l_scr, acc_scr, *, tq, tk, hps):
    qi = pl.program_id(2)
    rows = hps * tq
    m_scr[...] = jnp.full((rows, LANE), -1e30, F32)
    l_scr[...] = jnp.zeros((rows, LANE), F32)
    acc_scr[...] = jnp.zeros((rows, C_VDIM), F32)

    def block(kj, masked):
        start = pl.multiple_of(kj * tk, tk)
        kr2 = kr2_ref[0, pl.ds(start, tk), :]
        if masked:
            ri = lax.broadcasted_iota(jnp.int32, (tq, tk), 0)
            ci = lax.broadcasted_iota(jnp.int32, (tq, tk), 1)
            ok = ((qi * tq + ri) // CHUNK) >= ((start + ci) // CHUNK)

        def scores(g):
            return _mm_nt(q_ref[0, g], jnp.concatenate([k_ref[0, g, pl.ds(start, tk), :], kr2], axis=1))

        s_next = scores(0)
        for g in range(hps):
            rs = slice(g * tq, (g + 1) * tq)
            s = s_next
            if g + 1 < hps:
                s_next = scores(g + 1)
            if masked:
                s = jnp.where(ok, s, -jnp.inf)
            m_prev = m_scr[rs]
            m_new = jnp.maximum(m_prev, jnp.max(s, axis=-1, keepdims=True))
            alpha = jnp.exp2(m_prev - m_new)
            p = jnp.exp2(s - _lane_tile(m_new, tk // LANE))
            l_scr[rs] = alpha * l_scr[rs] + jnp.sum(p, axis=-1, keepdims=True)
            acc_scr[rs] = acc_scr[rs] * alpha + _mm(p.astype(BF16), v_ref[0, g, pl.ds(start, tk), :])
            m_scr[rs] = m_new

    def body(kj, carry):
        block(kj, False)
        return carry

    nfull = (qi * tq) // tk
    lax.fori_loop(0, nfull, body, 0)
    block(nfull, True)
    for g in range(hps):
        rs = slice(g * tq, (g + 1) * tq)
        yc_ref[0, :, g * C_VDIM:(g + 1) * C_VDIM] = (acc_scr[rs] * (1.0 / l_scr[rs])).astype(BF16)


def _attn_heads(q, k, kr2, v):
    bsz, _, L, _ = q.shape
    tq = min(L, 512)
    tk = min(L, 512)
    hps = 4
    rows = hps * tq
    return pl.pallas_call(
        functools.partial(_attn_heads_kernel, tq=tq, tk=tk, hps=hps),
        out_shape=jax.ShapeDtypeStruct((bsz, L, C_WIDTH), BF16),
        grid=(bsz, C_HEADS // hps, L // tq),
        in_specs=[pl.BlockSpec((1, hps, tq, QK_DIM), lambda b, g, i: (b, g, i, 0)),
                  pl.BlockSpec((1, hps, L, C_NOPE), lambda b, g, i: (b, g, 0, 0)),
                  pl.BlockSpec((1, L, LANE), lambda b, g, i: (b, 0, 0)),
                  pl.BlockSpec((1, hps, L, C_VDIM), lambda b, g, i: (b, g, 0, 0))],
        out_specs=pl.BlockSpec((1, tq, hps * C_VDIM), lambda b, g, i: (b, i, g)),
        scratch_shapes=[pltpu.VMEM((rows, LANE), F32), pltpu.VMEM((rows, LANE), F32),
                        pltpu.VMEM((rows, C_VDIM), F32)],
        compiler_params=_params(("arbitrary", "arbitrary", "arbitrary")),
        name="attn_heads",
    )(q, k, kr2, v)


def _attn_finish(acc, l, wuv_ref, yc_ref, tq):
    o = acc * (1.0 / l)
    for h in range(C_HEADS):
        oh = o[h * tq:(h + 1) * tq].astype(BF16)
        yc_ref[0, :, h * C_VDIM:(h + 1) * C_VDIM] = _mm(oh, wuv_ref[h]).astype(BF16)


def _attn_full_kernel(q_ref, cpast_ref, rpast_ref, knew_ref, wuv_ref, yc_ref, *, tq):
    rows = C_HEADS * tq
    q = q_ref[0].reshape(rows, KCAT)
    rpast = rpast_ref[0].astype(BF16)
    kpast = jnp.concatenate([cpast_ref[0].astype(BF16), rpast, rpast], axis=1)
    knew = knew_ref[0]
    s_past = _mm_nt(q, kpast)
    s_new = _mm_nt(q, knew)
    m = jnp.maximum(jnp.max(s_past, axis=-1, keepdims=True), jnp.max(s_new, axis=-1, keepdims=True))
    p_past = jnp.exp2(s_past - m)
    p_new = jnp.exp2(s_new - m)
    l = jnp.sum(p_past, axis=-1, keepdims=True) + jnp.sum(p_new, axis=-1, keepdims=True)
    acc = _mm(p_past.astype(BF16), kpast[:, 0:KV_LORA]) + _mm(p_new.astype(BF16), knew[:, 0:KV_LORA])
    _attn_finish(acc, l, wuv_ref, yc_ref, tq)


def _attn_full(q, ckv_past, kr_past, kcat, wuv, layer):
    bsz, _, tq, _ = q.shape
    npast = ckv_past.shape[2]
    return pl.pallas_call(
        functools.partial(_attn_full_kernel, tq=tq),
        out_shape=jax.ShapeDtypeStruct((bsz, tq, C_WIDTH), BF16),
        grid=(bsz,),
        in_specs=[pl.BlockSpec((1, C_HEADS, tq, KCAT), lambda b: (b, 0, 0, 0)),
                  pl.BlockSpec((None, 1, npast, KV_LORA), lambda b: (layer, b, 0, 0)),
                  pl.BlockSpec((None, 1, npast, C_ROPE), lambda b: (layer, b, 0, 0)),
                  pl.BlockSpec((1, tq, KCAT), lambda b: (b, 0, 0)),
                  pl.BlockSpec((C_HEADS, KV_LORA, C_VDIM), lambda b: (0, 0, 0))],
        out_specs=pl.BlockSpec((1, tq, C_WIDTH), lambda b: (b, 0, 0)),
        compiler_params=_params(("arbitrary",)),
        name="attn_full",
    )(q, ckv_past, kr_past, kcat, wuv)


def _outproj_kernel(x_ref, ya_ref, yb_ref, yc_ref, wo_ref, o_ref):
    tn = 512
    for j in range(0, D_MODEL, tn):
        acc = _mm(ya_ref[...], wo_ref[0:A_WIDTH, j:j + tn])
        acc = acc + _mm(yb_ref[...], wo_ref[A_WIDTH:A_WIDTH + B_WIDTH, j:j + tn])
        acc = acc + _mm(yc_ref[...], wo_ref[A_WIDTH + B_WIDTH:D_MODEL, j:j + tn])
        o_ref[:, j:j + tn] = x_ref[:, j:j + tn] + acc


def _outproj(x, ya, yb, yc, wo, layer):
    m = x.shape[0]
    tm = min(m, 256)
    row = lambda i: (i, 0)
    return pl.pallas_call(
        _outproj_kernel,
        out_shape=jax.ShapeDtypeStruct((m, D_MODEL), F32),
        grid=(m // tm,),
        in_specs=[pl.BlockSpec((tm, D_MODEL), row),
                  pl.BlockSpec((tm, A_WIDTH), row),
                  pl.BlockSpec((tm, B_WIDTH), row),
                  pl.BlockSpec((tm, C_WIDTH), row),
                  pl.BlockSpec((None, D_MODEL, D_MODEL), lambda i: (layer, 0, 0))],
        out_specs=pl.BlockSpec((tm, D_MODEL), row),
        compiler_params=_params(("arbitrary",)),
        name="outproj",
    )(x, ya, yb, yc, wo)


def _ffn_kernel(x_ref, nw_ref, wu_ref, wd_ref, fw_ref, o_ref, xn_scr, acc_scr, *, final_norm):
    f = pl.program_id(1)

    @pl.when(f == 0)
    def _():
        x = x_ref[...]
        ms = jnp.mean(x * x, axis=-1, keepdims=True)
        xn_scr[...] = (x * lax.rsqrt(ms + EPS) * nw_ref[...]).astype(BF16)
        acc_scr[...] = jnp.zeros_like(acc_scr)

    hid = jnp.maximum(_mm(xn_scr[...], wu_ref[...]), 0.0)
    acc_scr[...] += _mm((hid * hid).astype(BF16), wd_ref[...])

    @pl.when(f == pl.num_programs(1) - 1)
    def _():
        y = x_ref[...] + acc_scr[...]
        if final_norm:
            ms = jnp.mean(y * y, axis=-1, keepdims=True)
            y = y * lax.rsqrt(ms + EPS) * fw_ref[...]
        o_ref[...] = y


def _ffn(x, norm_w, wu, wd, final_w, final_norm, layer):
    m = x.shape[0]
    tm = min(m, 512)
    tf = 1024
    return pl.pallas_call(
        functools.partial(_ffn_kernel, final_norm=final_norm),
        out_shape=jax.ShapeDtypeStruct((m, D_MODEL), F32),
        grid=(m // tm, D_FF // tf),
        in_specs=[pl.BlockSpec((tm, D_MODEL), lambda i, f: (i, 0)),
                  pl.BlockSpec((1, D_MODEL), lambda i, f: (0, 0)),
                  pl.BlockSpec((None, D_MODEL, tf), lambda i, f: (layer, 0, f)),
                  pl.BlockSpec((None, tf, D_MODEL), lambda i, f: (layer, f, 0)),
                  pl.BlockSpec((1, D_MODEL), lambda i, f: (0, 0))],
        out_specs=pl.BlockSpec((tm, D_MODEL), lambda i, f: (i, 0)),
        scratch_shapes=[pltpu.VMEM((tm, D_MODEL), BF16), pltpu.VMEM((tm, D_MODEL), F32)],
        compiler_params=_params(("arbitrary", "arbitrary")),
        name="ffn",
    )(x, norm_w.reshape(1, D_MODEL), wu, wd, final_w.reshape(1, D_MODEL))


def _swap_halves(w):
    half = C_ROPE // 2
    return jnp.concatenate([w[..., half:], w[..., :half]], axis=-1)


def _prep_w_in(w_in):
    lead = w_in.shape[:-1]
    ckr = w_in[..., OFF_CKR:IN_COLS]
    g1 = jnp.concatenate([ckr, w_in[..., OFF_BBETA:OFF_CQ], jnp.zeros(lead + (LANE - C_ROPE - 2 * B_HEADS,), F32)],
                         axis=-1)
    g2 = jnp.concatenate([_swap_halves(ckr), jnp.zeros(lead + (LANE - C_ROPE,), F32)], axis=-1)
    tail = jnp.concatenate([w_in[..., OFF_CQ:OFF_CKV], w_in[..., OFF_CKV:OFF_CKR], g1, g2], axis=-1)
    return w_in.astype(BF16), tail.astype(BF16)


def _rope_tables(pos):
    half = C_ROPE // 2
    inv = ROPE_THETA ** (-jnp.arange(half, dtype=F32) / half)
    ang = pos.astype(F32)[:, None] * inv[None, :]
    cos, sin = jnp.cos(ang), jnp.sin(ang)
    cos2 = jnp.concatenate([cos, cos, cos, cos], axis=1)
    sin2 = jnp.concatenate([-sin, sin, -sin, sin], axis=1)
    return cos2, sin2


def _layer(x, bsz, L, tabs, lw, past, final_w, final_norm, layer, depth, stacks):
    proj = _inproj(x, lw['attn_norm_w'], lw['w_in'][0], layer, P1_COLS).reshape(bsz, L, P1_COLS)
    proj2 = _inproj(x, lw['attn_norm_w'], lw['w_in'][1], layer, P2_COLS).reshape(bsz, L, P2_COLS)

    ya, va = _mixer_a(proj, lw['a_ln_w'], lw['a_ln_b'], lw['a_w_s'], lw['a_b_s'], past is not None)

    if past is None:
        rows, chunk = 2 * CHUNK, CHUNK
        buf8 = jnp.zeros((bsz, 8, B_QKV), F32)
        s0 = jnp.zeros((bsz, B_HEADS, B_DK, B_DV), F32)
    else:
        rows, chunk = L, L
        ckv_past, kr_past, s0, conv_buf = past
        buf8 = jnp.concatenate([jnp.zeros((bsz, 8 - (B_CONV - 1), B_QKV), F32), conv_buf], axis=1)
    prep = _gdn_prep(proj, proj2, buf8, lw['b_conv_w'], lw['b_a_log'], lw['b_dt_bias'], rows, chunk)
    yb, s_new = _gdn_scan(prep, proj, lw['b_norm_w'], s0, rows, chunk)
    buf_new = proj[:, L - (B_CONV - 1):, P_BQKV:P_BQKV + B_QKV]

    cos8, sin8 = tabs
    mla_args = (proj2, cos8, sin8, lw['c_q_norm_w'], lw['wqn'], lw['wqr'], lw['wqs'], lw['c_kv_norm_w'])
    if past is None:
        ckv, kr, q, k, kr2, v = _mla_proj(*mla_args, (lw['wk'], lw['wv']), False, layer, depth, stacks)
        yc = _attn_heads(q, k, kr2, v)
    else:
        ckv, kr, kcat, q = _mla_proj(*mla_args, (lw['wukt'],), True, layer, depth, stacks)
        yc = _attn_full(q, ckv_past, kr_past, kcat, lw['wuv'], layer)

    m = bsz * L
    x = _outproj(x, ya.reshape(m, A_WIDTH), yb.reshape(m, B_WIDTH), yc.reshape(m, C_WIDTH), lw['w_out'], layer)
    x = _ffn(x, lw['mlp_norm_w'], lw['w_up'], lw['w_down'], final_w, final_norm, layer)
    return x, (ckv, kr, s_new, buf_new, va)


def kernel(x_prompt, x_sample, cache_mla_ckv, cache_mla_krope, state_gdn, state_gdn_conv,
           attn_norm_w, w_in, a_ln_w, a_ln_b, a_w_s, a_b_s, b_conv_w, b_a_log, b_dt_bias, b_norm_w,
           c_q_norm_w, c_w_uq, c_kv_norm_w, c_w_uk, c_w_uv, w_out, mlp_norm_w, w_up, w_down, final_norm_w):
    depth = w_in.shape[0]
    bp, lp, _ = x_prompt.shape
    bs, ls, _ = x_sample.shape
    past_len = cache_mla_ckv.shape[2]
    tabs_p = _rope_tables(jnp.arange(lp, dtype=jnp.int32))
    tabs_s = _rope_tables(past_len + jnp.arange(ls, dtype=jnp.int32))
    hp = x_prompt.reshape(bp * lp, D_MODEL)
    hs = x_sample.reshape(bs * ls, D_MODEL)
    sp_list, ss_list = [], []
    stacks_p = stacks_s = None
    w_in16, w_out16, w_up16, w_down16 = _prep_w_in(w_in), w_out.astype(BF16), w_up.astype(BF16), w_down.astype(BF16)
    for l in range(depth):
        wq_rope = c_w_uq[l][:, :, C_NOPE:]
        lw = {
            'attn_norm_w': attn_norm_w[l], 'w_in': w_in16,
            'a_ln_w': a_ln_w[l], 'a_ln_b': a_ln_b[l], 'a_w_s': a_w_s[l], 'a_b_s': a_b_s[l],
            'b_conv_w': b_conv_w[l], 'b_a_log': b_a_log[l], 'b_dt_bias': b_dt_bias[l], 'b_norm_w': b_norm_w[l],
            'c_q_norm_w': c_q_norm_w[l], 'c_kv_norm_w': c_kv_norm_w[l],
            'wqn': c_w_uq[l][:, :, :C_NOPE].reshape(Q_LORA, C_HEADS * C_NOPE).astype(BF16),
            'wqr': wq_rope.reshape(Q_LORA, C_HEADS * C_ROPE).astype(BF16),
            'wqs': _swap_halves(wq_rope).reshape(Q_LORA, C_HEADS * C_ROPE).astype(BF16),
            'wk': c_w_uk[l].reshape(KV_LORA, C_HEADS * C_NOPE).astype(BF16),
            'wv': c_w_uv[l].reshape(KV_LORA, C_HEADS * C_VDIM).astype(BF16),
            'wukt': jnp.transpose(c_w_uk[l], (1, 2, 0)).astype(BF16),
            'wuv': jnp.transpose(c_w_uv[l], (1, 0, 2)).astype(BF16),
            'w_out': w_out16, 'mlp_norm_w': mlp_norm_w[l], 'w_up': w_up16, 'w_down': w_down16,
        }
        last = l == depth - 1
        hp, sp = _layer(hp, bp, lp, tabs_p, lw, None, final_norm_w, last, l, depth, stacks_p)
        hs, ss = _layer(hs, bs, ls, tabs_s, lw,
                        (cache_mla_ckv, cache_mla_krope, state_gdn[l], state_gdn_conv[l]), final_norm_w, last,
                        l, depth, stacks_s)
        stacks_p, stacks_s = sp[:2], ss[:2]
        sp_list.append(sp)
        ss_list.append(ss)

    def stack(lst, i):
        return jnp.stack([s[i] for s in lst], axis=0)

    return (hp.reshape(bp, lp, D_MODEL), hs.reshape(bs, ls, D_MODEL),
            stacks_p[0], stacks_p[1], stack(sp_list, 2), stack(sp_list, 3),
            stacks_s[0], stacks_s[1], stack(ss_list, 2), stack(ss_list, 3), stack(ss_list, 4))
```

```python
import functools
import math

import jax
import jax.numpy as jnp
from jax import lax
from jax.experimental import pallas as pl
from jax.experimental.pallas import tpu as pltpu

F32 = jnp.float32
BF16 = jnp.bfloat16

D_MODEL = 2048
CHUNK = 64
EPS = 1e-6
LN_EPS = 1e-5
A_GROUPS = 4
A_HEAD = 128
A_WIDTH = 512
A_CHUNK = 128
B_HEADS = 4
B_DK = 128
B_DV = 128
B_WIDTH = 512
B_QKV = 1536
B_CONV = 4
C_HEADS = 8
C_NOPE = 128
C_ROPE = 64
C_VDIM = 128
C_WIDTH = 1024
Q_LORA = 512
KV_LORA = 256
ROPE_THETA = 10000.0
C_SCALE = (C_NOPE + C_ROPE) ** -0.5
Q_SCALE = C_SCALE * math.log2(math.e)
D_FF = 4 * D_MODEL

OFF_AU = 0
OFF_AV = OFF_AU + A_WIDTH
OFF_BQKV = OFF_AV + A_WIDTH
OFF_BZ = OFF_BQKV + B_QKV
OFF_BBETA = OFF_BZ + B_WIDTH
OFF_BALPHA = OFF_BBETA + B_HEADS
OFF_CQ = OFF_BALPHA + B_HEADS
OFF_CKV = OFF_CQ + Q_LORA
OFF_CKR = OFF_CKV + KV_LORA
IN_COLS = OFF_CKR + C_ROPE

P_AU = OFF_AU
P_AV = OFF_AV
P_BQKV = OFF_BQKV
P_BZ = OFF_BZ
P1_COLS = OFF_BBETA
P_CQ = 0
P_CKV = 512
P_G1 = 768
P_G2 = 896
P2_COLS = 1024
LANE = 128
G1_BETA = C_ROPE
G1_ALPHA = C_ROPE + B_HEADS
KCAT = KV_LORA + 2 * C_ROPE
QK_DIM = C_NOPE + 2 * C_ROPE

VMEM_LIMIT = 56 * 1024 * 1024


def _mm(a, b):
    return jnp.dot(a, b, preferred_element_type=F32)


def _mm_nt(a, b):
    return lax.dot_general(a, b, (((1,), (1,)), ((), ())), preferred_element_type=F32)


def _split2(x):
    hi = x.astype(BF16)
    lo = (x - hi.astype(F32)).astype(BF16)
    return hi, lo


def _split3(x):
    hi = x.astype(BF16)
    r = x - hi.astype(F32)
    mid = r.astype(BF16)
    lo = (r - mid.astype(F32)).astype(BF16)
    return hi, mid, lo


def _cat3_lhs(a):
    ah, al = _split2(a)
    return jnp.concatenate([ah, ah, al], axis=1)


def _cat3_rhs(b):
    bh, bl = _split2(b)
    return jnp.concatenate([bh, bl, bh], axis=0)


def _mm3(a, b):
    return _mm(_cat3_lhs(a), _cat3_rhs(b))


def _mm_exact_lhs(a, b):
    return _mm(jnp.concatenate([a, a, a], axis=1), jnp.concatenate(_split3(b), axis=0))


def _mm_nt_exact_lhs(a, b):
    return _mm_nt(jnp.concatenate([a, a, a], axis=1), jnp.concatenate(_split3(b), axis=1))


def _lane_tile(x, n):
    return x if n == 1 else jnp.concatenate([x] * n, axis=1)


def _params(sem):
    return pltpu.CompilerParams(dimension_semantics=sem, vmem_limit_bytes=VMEM_LIMIT)


def _inproj_kernel(x_ref, nw_ref, w_ref, o_ref):
    x = x_ref[...]
    ms = jnp.mean(x * x, axis=-1, keepdims=True)
    xn = (x * lax.rsqrt(ms + EPS) * nw_ref[...]).astype(BF16)
    o_ref[...] = _mm(xn, w_ref[...])


def _inproj(x, norm_w, w_pad, layer, ncols):
    m = x.shape[0]
    tm = min(m, 512)
    tn = 1536 if ncols % 1536 == 0 else 1024
    return pl.pallas_call(
        _inproj_kernel,
        out_shape=jax.ShapeDtypeStruct((m, ncols), F32),
        grid=(ncols // tn, m // tm),
        in_specs=[pl.BlockSpec((tm, D_MODEL), lambda j, i: (i, 0)),
                  pl.BlockSpec((1, D_MODEL), lambda j, i: (0, 0)),
                  pl.BlockSpec((None, D_MODEL, tn), lambda j, i: (layer, 0, j))],
        out_specs=pl.BlockSpec((tm, tn), lambda j, i: (i, j)),
        compiler_params=_params(("arbitrary", "arbitrary")),
        name="inproj",
    )(x, norm_w.reshape(1, D_MODEL), w_pad)


def _gelu(x):
    return 0.5 * x * (1.0 + lax.erf(x * (2.0 ** -0.5)))


def _mixa_kernel(u_ref, v_ref, lnw_ref, lnb_ref, ws_ref, bs_ref, ya_ref, *va_ref, lc, nsub):
    u = _gelu(u_ref[0])
    v = _gelu(v_ref[0])
    mu = jnp.mean(v, axis=-1, keepdims=True)
    vc = v - mu
    var = jnp.mean(vc * vc, axis=-1, keepdims=True)
    vn = vc * lax.rsqrt(var + LN_EPS) * lnw_ref[...] + lnb_ref[...]
    if va_ref:
        va_ref[0][0] = vn
    for s in range(nsub):
        rows = slice(s * lc, (s + 1) * lc)
        for g in range(A_GROUPS):
            cols = slice(g * A_HEAD, (g + 1) * A_HEAD)
            f = _mm(ws_ref[g], vn[rows, cols].astype(BF16)) + bs_ref[g]
            ya_ref[0, rows, cols] = (u[rows, cols] * f).astype(BF16)


def _mixer_a(proj, ln_w, ln_b, w_s, b_s, want_va):
    bsz, L, _ = proj.shape
    lc = min(L, A_CHUNK)
    rows = min(L, 512)
    nsub = rows // lc
    idx = jnp.arange(lc)
    allowed = (idx[:, None] // CHUNK) >= (idx[None, :] // CHUNK)
    ws = jnp.where(allowed[None], w_s[:, :lc, :lc], 0.0).astype(BF16)
    bs = jnp.broadcast_to(b_s[:, :lc, None], (A_GROUPS, lc, A_HEAD)).astype(F32)
    out_shape = [jax.ShapeDtypeStruct((bsz, L, A_WIDTH), BF16)]
    out_specs = [pl.BlockSpec((1, rows, A_WIDTH), lambda b, i: (b, i, 0))]
    if want_va:
        out_shape.append(jax.ShapeDtypeStruct((bsz, L, A_WIDTH), F32))
        out_specs.append(pl.BlockSpec((1, rows, A_WIDTH), lambda b, i: (b, i, 0)))
    res = pl.pallas_call(
        functools.partial(_mixa_kernel, lc=lc, nsub=nsub),
        out_shape=out_shape,
        grid=(bsz, L // rows),
        in_specs=[pl.BlockSpec((1, rows, A_WIDTH), lambda b, i: (b, i, P_AU // A_WIDTH)),
                  pl.BlockSpec((1, rows, A_WIDTH), lambda b, i: (b, i, P_AV // A_WIDTH)),
                  pl.BlockSpec((1, A_WIDTH), lambda b, i: (0, 0)),
                  pl.BlockSpec((1, A_WIDTH), lambda b, i: (0, 0)),
                  pl.BlockSpec((A_GROUPS, lc, lc), lambda b, i: (0, 0, 0)),
                  pl.BlockSpec((A_GROUPS, lc, A_HEAD), lambda b, i: (0, 0, 0))],
        out_specs=out_specs,
        compiler_params=_params(("arbitrary", "arbitrary")),
        name="mixer_a",
    )(proj, proj, ln_w.reshape(1, A_WIDTH), ln_b.reshape(1, A_WIDTH), ws, bs)
    return res if want_va else (res[0], None)


def _softplus(x):
    return jnp.maximum(x, 0.0) + jnp.log1p(jnp.exp(-jnp.abs(x)))


def _gdn_prep_kernel(xq_ref, xk_ref, xv_ref, hq_ref, hk_ref, hv_ref, buf_ref, g1_ref, cw_ref, alog_ref, dtb_ref,
                     u_ref, w_ref, qg_ref, kgt_ref, qk_ref, eg_ref, xp_scr, *, nb, rows, chunk):
    i = pl.program_id(1)
    nsub = rows // chunk
    ri = lax.broadcasted_iota(jnp.int32, (rows, rows), 0)
    ci = lax.broadcasted_iota(jnp.int32, (rows, rows), 1)
    same = (ri // chunk) == (ci // chunk)
    causal = same & (ri >= ci)
    strict = same & (ri > ci)
    tri = jnp.where(causal, 1.0, 0.0).astype(BF16)
    eye_r = jnp.where(ri == ci, 1.0, 0.0).astype(F32)
    li = lax.broadcasted_iota(jnp.int32, (LANE, LANE), 0)
    lj = lax.broadcasted_iota(jnp.int32, (LANE, LANE), 1)
    eye_l = jnp.where(li == lj, 1.0, 0.0).astype(BF16)

    ys, betas, gts = [], [], []
    for bb in range(nb):
        x = jnp.concatenate([xq_ref[bb], xk_ref[bb], xv_ref[bb]], axis=1)
        halo = jnp.concatenate([hq_ref[bb], hk_ref[bb], hv_ref[bb]], axis=1)
        xp_scr[bb, 0:8, :] = jnp.where(i == 0, buf_ref[bb], halo)
        xp_scr[bb, 8:8 + rows, :] = x
        y = x * cw_ref[B_CONV - 1:B_CONV, :]
        for k in range(1, B_CONV):
            y = y + xp_scr[bb, 8 - k:8 - k + rows, :] * cw_ref[B_CONV - 1 - k:B_CONV - k, :]
        ys.append(y * jax.nn.sigmoid(y))
        g1 = g1_ref[bb]
        betas.append(jax.nn.sigmoid(g1))
        gts.append(-jnp.exp(alog_ref[...]) * _softplus(g1 + dtb_ref[...]))

    gcs = [_mm_exact_lhs(tri, g) for g in gts]
    gcts = [_mm_nt_exact_lhs(eye_l, gc) for gc in gcs]
    glasts = [jnp.concatenate(
        [jnp.broadcast_to(gc[(s + 1) * chunk - 1:(s + 1) * chunk, :], (chunk, LANE)) for s in range(nsub)], axis=0)
        for gc in gcs]

    pairs = [(bb, h) for bb in range(nb) for h in range(B_HEADS)]
    qs, ks, kbs, rhss, decs, egcs = [], [], [], [], [], []
    for bb, h in pairs:
        y, gc = ys[bb], gcs[bb]
        bcol = betas[bb][:, G1_BETA + h:G1_BETA + h + 1]
        gcol = gc[:, G1_ALPHA + h:G1_ALPHA + h + 1]
        grow = gcts[bb][G1_ALPHA + h:G1_ALPHA + h + 1, :]
        qh = y[:, h * B_DK:(h + 1) * B_DK]
        qh = qh * lax.rsqrt(jnp.sum(qh * qh, axis=-1, keepdims=True) + EPS) * (B_DK ** -0.5)
        kh = y[:, B_WIDTH + h * B_DK:B_WIDTH + (h + 1) * B_DK]
        kh = kh * lax.rsqrt(jnp.sum(kh * kh, axis=-1, keepdims=True) + EPS)
        vh = y[:, 2 * B_WIDTH + h * B_DV:2 * B_WIDTH + (h + 1) * B_DV]
        kb = kh * bcol
        egc = jnp.exp(gcol)
        qs.append(qh)
        ks.append(kh)
        kbs.append(kb)
        egcs.append(egc)
        rhss.append(jnp.concatenate([vh * bcol, kb * egc], axis=1))
        decs.append(jnp.where(causal, jnp.exp(jnp.minimum(gcol - grow, 0.0)), 0.0))

    khb = [k.astype(BF16) for k in ks]
    kk = [_mm_nt(kb.astype(BF16), kx) for kb, kx in zip(kbs, khb)]
    qkm = [_mm_nt(q.astype(BF16), kx) for q, kx in zip(qs, khb)]

    ps = [jnp.where(strict, -(a * d), 0.0) for a, d in zip(kk, decs)]
    ts = [eye_r + p for p in ps]
    ms = [_mm(_cat3_lhs(p), _cat3_rhs(p)) for p in ps]
    nsq = int(math.log2(chunk)) - 1
    for step in range(nsq):
        mr = [_cat3_rhs(m) for m in ms]
        upd = [_mm(_cat3_lhs(t), b) for t, b in zip(ts, mr)]
        if step + 1 < nsq:
            ms = [_mm(_cat3_lhs(m), b) for m, b in zip(ms, mr)]
        ts = [t + d for t, d in zip(ts, upd)]
    sols = [_mm3(t, r) for t, r in zip(ts, rhss)]

    for n, (bb, h) in enumerate(pairs):
        hs = slice(h * B_DK, (h + 1) * B_DK)
        gcol = gcs[bb][:, G1_ALPHA + h:G1_ALPHA + h + 1]
        glcol = glasts[bb][:, G1_ALPHA + h:G1_ALPHA + h + 1]
        u_ref[bb, :, hs] = sols[n][:, :B_DV]
        w_ref[bb, :, hs] = sols[n][:, B_DV:].astype(BF16)
        qg_ref[bb, :, hs] = (qs[n] * egcs[n]).astype(BF16)
        kg = (ks[n] * jnp.exp(glcol - gcol)).astype(BF16)
        kgt_ref[bb, hs, :] = _mm_nt(eye_l, kg).astype(BF16)
        qk_ref[bb, :, h * rows:(h + 1) * rows] = (qkm[n] * decs[n]).astype(BF16)
        for s in range(nsub):
            eg_ref[bb, s, h:h + 1, :] = jnp.broadcast_to(
                jnp.exp(glasts[bb][s * chunk:s * chunk + 1, G1_ALPHA + h:G1_ALPHA + h + 1]), (1, LANE))


def _gdn_prep(proj, proj2, buf8, conv_w, a_log, dt_bias, rows, chunk):
    bsz, L, _ = proj.shape
    nb = 2
    nblk = L // rows
    rpb = max(rows // 8, 1)
    alog = jnp.zeros((1, LANE), F32).at[0, G1_ALPHA:G1_ALPHA + B_HEADS].set(a_log)
    dtb = jnp.zeros((1, LANE), F32).at[0, G1_ALPHA:G1_ALPHA + B_HEADS].set(dt_bias)
    return pl.pallas_call(
        functools.partial(_gdn_prep_kernel, nb=nb, rows=rows, chunk=chunk),
        out_shape=[jax.ShapeDtypeStruct((bsz, L, B_WIDTH), F32),
                   jax.ShapeDtypeStruct((bsz, L, B_WIDTH), BF16),
                   jax.ShapeDtypeStruct((bsz, L, B_WIDTH), BF16),
                   jax.ShapeDtypeStruct((bsz, B_WIDTH, L), BF16),
                   jax.ShapeDtypeStruct((bsz, L, B_HEADS * rows), BF16),
                   jax.ShapeDtypeStruct((bsz, L // chunk, B_HEADS, LANE), F32)],
        grid=(bsz // nb, nblk),
        in_specs=[pl.BlockSpec((nb, rows, B_WIDTH), lambda b, i, c=P_BQKV // B_WIDTH + part: (b, i, c))
                  for part in range(3)] + [
                  pl.BlockSpec((nb, 8, B_WIDTH),
                               lambda b, i, c=P_BQKV // B_WIDTH + part: (b, jnp.maximum(i * rpb - 1, 0), c))
                  for part in range(3)] + [
                  pl.BlockSpec((nb, 8, B_QKV), lambda b, i: (b, 0, 0)),
                  pl.BlockSpec((nb, rows, LANE), lambda b, i: (b, i, P_G1 // LANE)),
                  pl.BlockSpec((B_CONV, B_QKV), lambda b, i: (0, 0)),
                  pl.BlockSpec((1, LANE), lambda b, i: (0, 0)),
                  pl.BlockSpec((1, LANE), lambda b, i: (0, 0))],
        out_specs=[pl.BlockSpec((nb, rows, B_WIDTH), lambda b, i: (b, i, 0)),
                   pl.BlockSpec((nb, rows, B_WIDTH), lambda b, i: (b, i, 0)),
                   pl.BlockSpec((nb, rows, B_WIDTH), lambda b, i: (b, i, 0)),
                   pl.BlockSpec((nb, B_WIDTH, rows), lambda b, i: (b, 0, i)),
                   pl.BlockSpec((nb, rows, B_HEADS * rows), lambda b, i: (b, i, 0)),
                   pl.BlockSpec((nb, rows // chunk, B_HEADS, LANE), lambda b, i: (b, i, 0, 0))],
        scratch_shapes=[pltpu.VMEM((nb, rows + 8, B_QKV), F32)],
        compiler_params=_params(("arbitrary", "arbitrary")),
        name="gdn_prep",
    )(proj, proj, proj, proj, proj, proj, buf8, proj2, conv_w, alog, dtb)


def _gdn_scan_kernel(u_ref, w_ref, qg_ref, kgt_ref, qk_ref, eg_ref, z_ref, nw_ref, s0_ref,
                     yb_ref, s_ref, *, bsz, rows, chunk):
    @pl.when(pl.program_id(0) == 0)
    def _():
        s_ref[...] = s0_ref[...]

    nsub = rows // chunk
    pairs = [(b, h) for b in range(bsz) for h in range(B_HEADS)]
    st = [s_ref[b, h] for b, h in pairs]
    for s in range(nsub):
        rs = slice(s * chunk, (s + 1) * chunk)
        r1 = [_mm(jnp.concatenate([w_ref[b, rs, h * B_DV:(h + 1) * B_DV], qg_ref[b, rs, h * B_DV:(h + 1) * B_DV]],
                                  axis=0), x.astype(BF16))
              for (b, h), x in zip(pairs, st)]
        vb = [(u_ref[b, rs, h * B_DV:(h + 1) * B_DV] - r[:chunk]).astype(BF16) for (b, h), r in zip(pairs, r1)]
        r2 = [_mm(jnp.concatenate([qk_ref[b, rs, h * rows + s * chunk:h * rows + (s + 1) * chunk],
                                   kgt_ref[b, h * B_DK:(h + 1) * B_DK, rs]], axis=0), v)
              for (b, h), v in zip(pairs, vb)]
        for n, (b, h) in enumerate(pairs):
            hs = slice(h * B_DV, (h + 1) * B_DV)
            o = r1[n][chunk:] + r2[n][:chunk]
            st[n] = st[n] * eg_ref[b, s, h:h + 1, :] + r2[n][chunk:]
            z = z_ref[b, rs, hs]
            on = o * lax.rsqrt(jnp.mean(o * o, axis=-1, keepdims=True) + EPS) * nw_ref[...]
            yb_ref[b, rs, hs] = (on * (z * jax.nn.sigmoid(z))).astype(BF16)
    for n, (b, h) in enumerate(pairs):
        s_ref[b, h] = st[n]


def _gdn_scan(prep, proj, norm_w, s0, rows, chunk):
    u, w, qg, kgt, qk, eg = prep
    bsz, L, _ = u.shape
    nblk = L // rows
    return pl.pallas_call(
        functools.partial(_gdn_scan_kernel, bsz=bsz, rows=rows, chunk=chunk),
        out_shape=[jax.ShapeDtypeStruct((bsz, L, B_WIDTH), BF16),
                   jax.ShapeDtypeStruct((bsz, B_HEADS, B_DK, B_DV), F32)],
        grid=(nblk,),
        in_specs=[pl.BlockSpec((bsz, rows, B_WIDTH), lambda i: (0, i, 0)),
                  pl.BlockSpec((bsz, rows, B_WIDTH), lambda i: (0, i, 0)),
                  pl.BlockSpec((bsz, rows, B_WIDTH), lambda i: (0, i, 0)),
                  pl.BlockSpec((bsz, B_WIDTH, rows), lambda i: (0, 0, i)),
                  pl.BlockSpec((bsz, rows, B_HEADS * rows), lambda i: (0, i, 0)),
                  pl.BlockSpec((bsz, rows // chunk, B_HEADS, LANE), lambda i: (0, i, 0, 0)),
                  pl.BlockSpec((bsz, rows, B_WIDTH), lambda i: (0, i, P_BZ // B_WIDTH)),
                  pl.BlockSpec((1, B_DV), lambda i: (0, 0)),
                  pl.BlockSpec((bsz, B_HEADS, B_DK, B_DV), lambda i: (0, 0, 0, 0))],
        out_specs=[pl.BlockSpec((bsz, rows, B_WIDTH), lambda i: (0, i, 0)),
                   pl.BlockSpec((bsz, B_HEADS, B_DK, B_DV), lambda i: (0, 0, 0, 0))],
        compiler_params=_params(("arbitrary",)),
        name="gdn_scan",
    )(u, w, qg, kgt, qk, eg, proj, norm_w.reshape(1, B_DV), s0)


def _mla_common(cq_ref, ckv_ref, g1_ref, g2_ref, cos8_ref, sin8_ref, qnw_ref, wqn_ref, wqr_ref, wqs_ref, kvnw_ref):
    cq = cq_ref[0]
    cqn = (cq * lax.rsqrt(jnp.mean(cq * cq, axis=-1, keepdims=True) + EPS) * qnw_ref[...]).astype(BF16)
    c = ckv_ref[0]
    ckv = c * lax.rsqrt(jnp.mean(c * c, axis=-1, keepdims=True) + EPS) * kvnw_ref[...]
    cos2, sin2 = cos8_ref[...], sin8_ref[...]
    kr = g1_ref[0][:, 0:C_ROPE] * cos2[:, 0:C_ROPE] + g2_ref[0][:, 0:C_ROPE] * sin2[:, 0:C_ROPE]
    qn = _mm(cqn, wqn_ref[...])
    ntile = C_HEADS * C_ROPE // LANE
    qr = (_mm(cqn, wqr_ref[...]) * _lane_tile(cos2, ntile) + _mm(cqn, wqs_ref[...]) * _lane_tile(sin2, ntile)) * Q_SCALE
    return ckv, kr, qn, qr


def _rope_slot(qr, h):
    lane = lax.broadcasted_iota(jnp.int32, (1, LANE), 1)
    pair = qr[:, (h // 2) * LANE:(h // 2 + 1) * LANE]
    return jnp.where((lane // C_ROPE) == (h % 2), pair, 0.0).astype(BF16)


def _write_stacked(out_ref, val, layer, prev_ref):
    for d in range(out_ref.shape[0]):
        if d == layer:
            out_ref[d] = val
        elif prev_ref is None:
            out_ref[d] = jnp.zeros_like(val)
        else:
            out_ref[d] = prev_ref[d]


def _mla_proj_kernel(cq_ref, ckv_ref, g1_ref, g2_ref, cos8_ref, sin8_ref, qnw_ref, wqn_ref, wqr_ref, wqs_ref,
                     kvnw_ref, wukt_ref, ckv_out, kr_out, kcat_out, q_out, *, layer, prev):
    ckv, kr, qn, qr = _mla_common(cq_ref, ckv_ref, g1_ref, g2_ref, cos8_ref, sin8_ref, qnw_ref, wqn_ref, wqr_ref,
                                  wqs_ref, kvnw_ref)
    _write_stacked(ckv_out, ckv, layer, prev and prev[0])
    _write_stacked(kr_out, kr, layer, prev and prev[1])
    kcat_out[0, :, 0:KV_LORA] = ckv.astype(BF16)
    kcat_out[0, :, KV_LORA:KV_LORA + C_ROPE] = kr.astype(BF16)
    kcat_out[0, :, KV_LORA + C_ROPE:KCAT] = kr.astype(BF16)
    for h in range(C_HEADS):
        ql = _mm(qn[:, h * C_NOPE:(h + 1) * C_NOPE].astype(BF16), wukt_ref[h]) * Q_SCALE
        q_out[0, h, :, 0:KV_LORA] = ql.astype(BF16)
        q_out[0, h, :, KV_LORA:KCAT] = _rope_slot(qr, h)


def _mla_heads_kernel(cq_ref, ckv_ref, g1_ref, g2_ref, cos8_ref, sin8_ref, qnw_ref, wqn_ref, wqr_ref, wqs_ref,
                      kvnw_ref, wk_ref, wv_ref, ckv_out, kr_out, q_out, k_out, kr2_out, v_out, *, layer, prev):
    ckv, kr, qn, qr = _mla_common(cq_ref, ckv_ref, g1_ref, g2_ref, cos8_ref, sin8_ref, qnw_ref, wqn_ref, wqr_ref,
                                  wqs_ref, kvnw_ref)
    _write_stacked(ckv_out, ckv, layer, prev and prev[0])
    _write_stacked(kr_out, kr, layer, prev and prev[1])
    ckvb = ckv.astype(BF16)
    kn = _mm(ckvb, wk_ref[...])
    vv = _mm(ckvb, wv_ref[...])
    krb = kr.astype(BF16)
    kr2_out[0] = jnp.concatenate([krb, krb], axis=1)
    for h in range(C_HEADS):
        hs = slice(h * C_NOPE, (h + 1) * C_NOPE)
        q_out[0, h, :, 0:C_NOPE] = (qn[:, hs] * Q_SCALE).astype(BF16)
        q_out[0, h, :, C_NOPE:QK_DIM] = _rope_slot(qr, h)
        k_out[0, h] = kn[:, hs].astype(BF16)
        v_out[0, h] = vv[:, h * C_VDIM:(h + 1) * C_VDIM].astype(BF16)


def _mla_proj(proj, cos8, sin8, q_norm_w, wqn, wqr, wqs, kv_norm_w, head_w, absorbed, layer, depth, stacks):
    bsz, L, _ = proj.shape
    tm = min(L, 256)
    const2 = lambda b, i: (0, 0)
    head_out = lambda width: (jax.ShapeDtypeStruct((bsz, C_HEADS, L, width), BF16),
                              pl.BlockSpec((1, C_HEADS, tm, width), lambda b, i: (b, 0, i, 0)))
    stack_out = lambda width: (jax.ShapeDtypeStruct((depth, bsz, L, width), F32),
                               pl.BlockSpec((depth, None, tm, width), lambda b, i: (0, b, i, 0)))
    outs = [stack_out(KV_LORA), stack_out(C_ROPE)]
    if absorbed:
        body = _mla_proj_kernel
        w_specs = [pl.BlockSpec((C_HEADS, C_NOPE, KV_LORA), lambda b, i: (0, 0, 0))]
        outs += [(jax.ShapeDtypeStruct((bsz, L, KCAT), BF16), pl.BlockSpec((1, tm, KCAT), lambda b, i: (b, i, 0))),
                 head_out(KCAT)]
    else:
        body = _mla_heads_kernel
        w_specs = [pl.BlockSpec((KV_LORA, C_HEADS * C_NOPE), const2), pl.BlockSpec((KV_LORA, C_HEADS * C_VDIM), const2)]
        outs += [head_out(QK_DIM), head_out(C_NOPE),
                 (jax.ShapeDtypeStruct((bsz, L, LANE), BF16), pl.BlockSpec((1, tm, LANE), lambda b, i: (b, i, 0))),
                 head_out(C_VDIM)]
    if stacks is None:
        kern, prev_specs, prev_args = functools.partial(body, layer=layer, prev=None), [], ()
    else:
        kern = lambda ckv_prev, kr_prev, *refs: body(*refs, layer=layer, prev=(ckv_prev, kr_prev))
        prev_specs, prev_args = [outs[0][1], outs[1][1]], tuple(stacks)
    return pl.pallas_call(
        kern,
        out_shape=[o[0] for o in outs],
        grid=(bsz, L // tm),
        in_specs=prev_specs + [pl.BlockSpec((1, tm, Q_LORA), lambda b, i: (b, i, P_CQ // Q_LORA)),
                  pl.BlockSpec((1, tm, KV_LORA), lambda b, i: (b, i, P_CKV // KV_LORA)),
                  pl.BlockSpec((1, tm, LANE), lambda b, i: (b, i, P_G1 // LANE)),
                  pl.BlockSpec((1, tm, LANE), lambda b, i: (b, i, P_G2 // LANE)),
                  pl.BlockSpec((tm, LANE), lambda b, i: (i, 0)),
                  pl.BlockSpec((tm, LANE), lambda b, i: (i, 0)),
                  pl.BlockSpec((1, Q_LORA), const2),
                  pl.BlockSpec((Q_LORA, C_HEADS * C_NOPE), const2),
                  pl.BlockSpec((Q_LORA, C_HEADS * C_ROPE), const2),
                  pl.BlockSpec((Q_LORA, C_HEADS * C_ROPE), const2),
                  pl.BlockSpec((1, KV_LORA), const2)] + w_specs,
        out_specs=[o[1] for o in outs],
        compiler_params=_params(("arbitrary", "arbitrary")),
        name="mla_proj" if absorbed else "mla_heads",
    )(*prev_args, proj, proj, proj, proj, cos8, sin8, q_norm_w.reshape(1, Q_LORA), wqn, wqr, wqs,
      kv_norm_w.reshape(1, KV_LORA), *head_w)


def _attn_heads_kernel(q_ref, k_ref, kr2_ref, v_ref, yc_ref, m_scr, l_scr, acc_scr, *, tq, tk, hps):
    qi = pl.program_id(2)
    rows = hps * tq
    m_scr[...] = jnp.full((rows, LANE), -1e30, F32)
    l_scr[...] = jnp.zeros((rows, LANE), F32)
    acc_scr[...] = jnp.zeros((rows, C_VDIM), F32)

    def block(kj, masked):
        start = pl.multiple_of(kj * tk, tk)
        kr2 = kr2_ref[0, pl.ds(start, tk), :]
        if masked:
            ri = lax.broadcasted_iota(jnp.int32, (tq, tk), 0)
            ci = lax.broadcasted_iota(jnp.int32, (tq, tk), 1)
            ok = ((qi * tq + ri) // CHUNK) >= ((start + ci) // CHUNK)

        def scores(g):
            return _mm_nt(q_ref[0, g], jnp.concatenate([k_ref[0, g, pl.ds(start, tk), :], kr2], axis=1))

        s_next = scores(0)
        for g in range(hps):
            rs = slice(g * tq, (g + 1) * tq)
            s = s_next
            if g + 1 < hps:
                s_next = scores(g + 1)
            if masked:
                s = jnp.where(ok, s, -jnp.inf)
            m_prev = m_scr[rs]
            m_new = jnp.maximum(m_prev, jnp.max(s, axis=-1, keepdims=True))
            alpha = jnp.exp2(m_prev - m_new)
            p = jnp.exp2(s - _lane_tile(m_new, tk // LANE))
            l_scr[rs] = alpha * l_scr[rs] + jnp.sum(p, axis=-1, keepdims=True)
            acc_scr[rs] = acc_scr[rs] * alpha + _mm(p.astype(BF16), v_ref[0, g, pl.ds(start, tk), :])
            m_scr[rs] = m_new

    def body(kj, carry):
        block(kj, False)
        return carry

    nfull = (qi * tq) // tk
    lax.fori_loop(0, nfull, body, 0)
    block(nfull, True)
    for g in range(hps):
        rs = slice(g * tq, (g + 1) * tq)
        yc_ref[0, :, g * C_VDIM:(g + 1) * C_VDIM] = (acc_scr[rs] * (1.0 / l_scr[rs])).astype(BF16)


def _attn_heads(q, k, kr2, v):
    bsz, _, L, _ = q.shape
    tq = min(L, 512)
    tk = min(L, 512)
    hps = 4
    rows = hps * tq
    return pl.pallas_call(
        functools.partial(_attn_heads_kernel, tq=tq, tk=tk, hps=hps),
        out_shape=jax.ShapeDtypeStruct((bsz, L, C_WIDTH), BF16),
        grid=(bsz, C_HEADS // hps, L // tq),
        in_specs=[pl.BlockSpec((1, hps, tq, QK_DIM), lambda b, g, i: (b, g, i, 0)),
                  pl.BlockSpec((1, hps, L, C_NOPE), lambda b, g, i: (b, g, 0, 0)),
                  pl.BlockSpec((1, L, LANE), lambda b, g, i: (b, 0, 0)),
                  pl.BlockSpec((1, hps, L, C_VDIM), lambda b, g, i: (b, g, 0, 0))],
        out_specs=pl.BlockSpec((1, tq, hps * C_VDIM), lambda b, g, i: (b, i, g)),
        scratch_shapes=[pltpu.VMEM((rows, LANE), F32), pltpu.VMEM((rows, LANE), F32),
                        pltpu.VMEM((rows, C_VDIM), F32)],
        compiler_params=_params(("arbitrary", "arbitrary", "arbitrary")),
        name="attn_heads",
    )(q, k, kr2, v)


def _attn_finish(acc, l, wuv_ref, yc_ref, tq):
    o = acc * (1.0 / l)
    for h in range(C_HEADS):
        oh = o[h * tq:(h + 1) * tq].astype(BF16)
        yc_ref[0, :, h * C_VDIM:(h + 1) * C_VDIM] = _mm(oh, wuv_ref[h]).astype(BF16)


def _attn_full_kernel(q_ref, cpast_ref, rpast_ref, knew_ref, wuv_ref, yc_ref, *, tq):
    rows = C_HEADS * tq
    q = q_ref[0].reshape(rows, KCAT)
    rpast = rpast_ref[0].astype(BF16)
    kpast = jnp.concatenate([cpast_ref[0].astype(BF16), rpast, rpast], axis=1)
    knew = knew_ref[0]
    s_past = _mm_nt(q, kpast)
    s_new = _mm_nt(q, knew)
    m = jnp.maximum(jnp.max(s_past, axis=-1, keepdims=True), jnp.max(s_new, axis=-1, keepdims=True))
    p_past = jnp.exp2(s_past - m)
    p_new = jnp.exp2(s_new - m)
    l = jnp.sum(p_past, axis=-1, keepdims=True) + jnp.sum(p_new, axis=-1, keepdims=True)
    acc = _mm(p_past.astype(BF16), kpast[:, 0:KV_LORA]) + _mm(p_new.astype(BF16), knew[:, 0:KV_LORA])
    _attn_finish(acc, l, wuv_ref, yc_ref, tq)


def _attn_full(q, ckv_past, kr_past, kcat, wuv, layer):
    bsz, _, tq, _ = q.shape
    npast = ckv_past.shape[2]
    return pl.pallas_call(
        functools.partial(_attn_full_kernel, tq=tq),
        out_shape=jax.ShapeDtypeStruct((bsz, tq, C_WIDTH), BF16),
        grid=(bsz,),
        in_specs=[pl.BlockSpec((1, C_HEADS, tq, KCAT), lambda b: (b, 0, 0, 0)),
                  pl.BlockSpec((None, 1, npast, KV_LORA), lambda b: (layer, b, 0, 0)),
                  pl.BlockSpec((None, 1, npast, C_ROPE), lambda b: (layer, b, 0, 0)),
                  pl.BlockSpec((1, tq, KCAT), lambda b: (b, 0, 0)),
                  pl.BlockSpec((C_HEADS, KV_LORA, C_VDIM), lambda b: (0, 0, 0))],
        out_specs=pl.BlockSpec((1, tq, C_WIDTH), lambda b: (b, 0, 0)),
        compiler_params=_params(("arbitrary",)),
        name="attn_full",
    )(q, ckv_past, kr_past, kcat, wuv)


def _outproj_kernel(x_ref, ya_ref, yb_ref, yc_ref, wo_ref, o_ref):
    tn = 512
    for j in range(0, D_MODEL, tn):
        acc = _mm(ya_ref[...], wo_ref[0:A_WIDTH, j:j + tn])
        acc = acc + _mm(yb_ref[...], wo_ref[A_WIDTH:A_WIDTH + B_WIDTH, j:j + tn])
        acc = acc + _mm(yc_ref[...], wo_ref[A_WIDTH + B_WIDTH:D_MODEL, j:j + tn])
        o_ref[:, j:j + tn] = x_ref[:, j:j + tn] + acc


def _outproj(x, ya, yb, yc, wo, layer):
    m = x.shape[0]
    tm = min(m, 256)
    row = lambda i: (i, 0)
    return pl.pallas_call(
        _outproj_kernel,
        out_shape=jax.ShapeDtypeStruct((m, D_MODEL), F32),
        grid=(m // tm,),
        in_specs=[pl.BlockSpec((tm, D_MODEL), row),
                  pl.BlockSpec((tm, A_WIDTH), row),
                  pl.BlockSpec((tm, B_WIDTH), row),
                  pl.BlockSpec((tm, C_WIDTH), row),
                  pl.BlockSpec((None, D_MODEL, D_MODEL), lambda i: (layer, 0, 0))],
        out_specs=pl.BlockSpec((tm, D_MODEL), row),
        compiler_params=_params(("arbitrary",)),
        name="outproj",
    )(x, ya, yb, yc, wo)


def _ffn_kernel(x_ref, nw_ref, wu_ref, wd_ref, fw_ref, o_ref, xn_scr, *, final_norm):
    f = pl.program_id(1)

    @pl.when(f == 0)
    def _():
        x = x_ref[...]
        ms = jnp.mean(x * x, axis=-1, keepdims=True)
        xn_scr[...] = (x * lax.rsqrt(ms + EPS) * nw_ref[...]).astype(BF16)
        o_ref[...] = x

    hid = jnp.maximum(_mm(xn_scr[...], wu_ref[...]), 0.0)
    o_ref[...] += _mm((hid * hid).astype(BF16), wd_ref[...])

    if final_norm:
        @pl.when(f == pl.num_programs(1) - 1)
        def _():
            y = o_ref[...]
            ms = jnp.mean(y * y, axis=-1, keepdims=True)
            o_ref[...] = y * lax.rsqrt(ms + EPS) * fw_ref[...]


def _ffn(x, norm_w, wu, wd, final_w, final_norm, layer):
    m = x.shape[0]
    tm = min(m, 1024)
    tf = 512
    return pl.pallas_call(
        functools.partial(_ffn_kernel, final_norm=final_norm),
        out_shape=jax.ShapeDtypeStruct((m, D_MODEL), F32),
        grid=(m // tm, D_FF // tf),
        in_specs=[pl.BlockSpec((tm, D_MODEL), lambda i, f: (i, 0)),
                  pl.BlockSpec((1, D_MODEL), lambda i, f: (0, 0)),
                  pl.BlockSpec((None, D_MODEL, tf), lambda i, f: (layer, 0, f)),
                  pl.BlockSpec((None, tf, D_MODEL), lambda i, f: (layer, f, 0)),
                  pl.BlockSpec((1, D_MODEL), lambda i, f: (0, 0))],
        out_specs=pl.BlockSpec((tm, D_MODEL), lambda i, f: (i, 0)),
        scratch_shapes=[pltpu.VMEM((tm, D_MODEL), BF16)],
        compiler_params=_params(("arbitrary", "arbitrary")),
        name="ffn",
    )(x, norm_w.reshape(1, D_MODEL), wu, wd, final_w.reshape(1, D_MODEL))


def _swap_halves(w):
    half = C_ROPE // 2
    return jnp.concatenate([w[..., half:], w[..., :half]], axis=-1)


def _prep_w_in(w_in):
    lead = w_in.shape[:-1]
    ckr = w_in[..., OFF_CKR:IN_COLS]
    g1 = jnp.concatenate([ckr, w_in[..., OFF_BBETA:OFF_CQ], jnp.zeros(lead + (LANE - C_ROPE - 2 * B_HEADS,), F32)],
                         axis=-1)
    g2 = jnp.concatenate([_swap_halves(ckr), jnp.zeros(lead + (LANE - C_ROPE,), F32)], axis=-1)
    tail = jnp.concatenate([w_in[..., OFF_CQ:OFF_CKV], w_in[..., OFF_CKV:OFF_CKR], g1, g2], axis=-1)
    return w_in.astype(BF16), tail.astype(BF16)


def _rope_tables(pos):
    half = C_ROPE // 2
    inv = ROPE_THETA ** (-jnp.arange(half, dtype=F32) / half)
    ang = pos.astype(F32)[:, None] * inv[None, :]
    cos, sin = jnp.cos(ang), jnp.sin(ang)
    cos2 = jnp.concatenate([cos, cos, cos, cos], axis=1)
    sin2 = jnp.concatenate([-sin, sin, -sin, sin], axis=1)
    return cos2, sin2


def _layer(x, bsz, L, tabs, lw, past, final_w, final_norm, layer, depth, stacks):
    proj = _inproj(x, lw['attn_norm_w'], lw['w_in'][0], layer, P1_COLS).reshape(bsz, L, P1_COLS)
    proj2 = _inproj(x, lw['attn_norm_w'], lw['w_in'][1], layer, P2_COLS).reshape(bsz, L, P2_COLS)

    ya, va = _mixer_a(proj, lw['a_ln_w'], lw['a_ln_b'], lw['a_w_s'], lw['a_b_s'], past is not None)

    if past is None:
        rows, chunk = 2 * CHUNK, CHUNK
        buf8 = jnp.zeros((bsz, 8, B_QKV), F32)
        s0 = jnp.zeros((bsz, B_HEADS, B_DK, B_DV), F32)
    else:
        rows, chunk = L, L
        ckv_past, kr_past, s0, conv_buf = past
        buf8 = jnp.concatenate([jnp.zeros((bsz, 8 - (B_CONV - 1), B_QKV), F32), conv_buf], axis=1)
    prep = _gdn_prep(proj, proj2, buf8, lw['b_conv_w'], lw['b_a_log'], lw['b_dt_bias'], rows, chunk)
    yb, s_new = _gdn_scan(prep, proj, lw['b_norm_w'], s0, rows, chunk)
    buf_new = proj[:, L - (B_CONV - 1):, P_BQKV:P_BQKV + B_QKV]

    cos8, sin8 = tabs
    mla_args = (proj2, cos8, sin8, lw['c_q_norm_w'], lw['wqn'], lw['wqr'], lw['wqs'], lw['c_kv_norm_w'])
    if past is None:
        ckv, kr, q, k, kr2, v = _mla_proj(*mla_args, (lw['wk'], lw['wv']), False, layer, depth, stacks)
        yc = _attn_heads(q, k, kr2, v)
    else:
        ckv, kr, kcat, q = _mla_proj(*mla_args, (lw['wukt'],), True, layer, depth, stacks)
        yc = _attn_full(q, ckv_past, kr_past, kcat, lw['wuv'], layer)

    m = bsz * L
    x = _outproj(x, ya.reshape(m, A_WIDTH), yb.reshape(m, B_WIDTH), yc.reshape(m, C_WIDTH), lw['w_out'], layer)
    x = _ffn(x, lw['mlp_norm_w'], lw['w_up'], lw['w_down'], final_w, final_norm, layer)
    return x, (ckv, kr, s_new, buf_new, va)


def kernel(x_prompt, x_sample, cache_mla_ckv, cache_mla_krope, state_gdn, state_gdn_conv,
           attn_norm_w, w_in, a_ln_w, a_ln_b, a_w_s, a_b_s, b_conv_w, b_a_log, b_dt_bias, b_norm_w,
           c_q_norm_w, c_w_uq, c_kv_norm_w, c_w_uk, c_w_uv, w_out, mlp_norm_w, w_up, w_down, final_norm_w):
    depth = w_in.shape[0]
    bp, lp, _ = x_prompt.shape
    bs, ls, _ = x_sample.shape
    past_len = cache_mla_ckv.shape[2]
    tabs_p = _rope_tables(jnp.arange(lp, dtype=jnp.int32))
    tabs_s = _rope_tables(past_len + jnp.arange(ls, dtype=jnp.int32))
    hp = x_prompt.reshape(bp * lp, D_MODEL)
    hs = x_sample.reshape(bs * ls, D_MODEL)
    sp_list, ss_list = [], []
    stacks_p = stacks_s = None
    w_in16, w_out16, w_up16, w_down16 = _prep_w_in(w_in), w_out.astype(BF16), w_up.astype(BF16), w_down.astype(BF16)
    for l in range(depth):
        wq_rope = c_w_uq[l][:, :, C_NOPE:]
        lw = {
            'attn_norm_w': attn_norm_w[l], 'w_in': w_in16,
            'a_ln_w': a_ln_w[l], 'a_ln_b': a_ln_b[l], 'a_w_s': a_w_s[l], 'a_b_s': a_b_s[l],
            'b_conv_w': b_conv_w[l], 'b_a_log': b_a_log[l], 'b_dt_bias': b_dt_bias[l], 'b_norm_w': b_norm_w[l],
            'c_q_norm_w': c_q_norm_w[l], 'c_kv_norm_w': c_kv_norm_w[l],
            'wqn': c_w_uq[l][:, :, :C_NOPE].reshape(Q_LORA, C_HEADS * C_NOPE).astype(BF16),
            'wqr': wq_rope.reshape(Q_LORA, C_HEADS * C_ROPE).astype(BF16),
            'wqs': _swap_halves(wq_rope).reshape(Q_LORA, C_HEADS * C_ROPE).astype(BF16),
            'wk': c_w_uk[l].reshape(KV_LORA, C_HEADS * C_NOPE).astype(BF16),
            'wv': c_w_uv[l].reshape(KV_LORA, C_HEADS * C_VDIM).astype(BF16),
            'wukt': jnp.transpose(c_w_uk[l], (1, 2, 0)).astype(BF16),
            'wuv': jnp.transpose(c_w_uv[l], (1, 0, 2)).astype(BF16),
            'w_out': w_out16, 'mlp_norm_w': mlp_norm_w[l], 'w_up': w_up16, 'w_down': w_down16,
        }
        last = l == depth - 1
        hp, sp = _layer(hp, bp, lp, tabs_p, lw, None, final_norm_w, last, l, depth, stacks_p)
        hs, ss = _layer(hs, bs, ls, tabs_s, lw,
                        (cache_mla_ckv, cache_mla_krope, state_gdn[l], state_gdn_conv[l]), final_norm_w, last,
                        l, depth, stacks_s)
        stacks_p, stacks_s = sp[:2], ss[:2]
        sp_list.append(sp)
        ss_list.append(ss)

    def stack(lst, i):
        return jnp.stack([s[i] for s in lst], axis=0)

    return (hp.reshape(bp, lp, D_MODEL), hs.reshape(bs, ls, D_MODEL),
            stacks_p[0], stacks_p[1], stack(sp_list, 2), stack(sp_list, 3),
            stacks_s[0], stacks_s[1], stack(ss_list, 2), stack(ss_list, 3), stack(ss_list, 4))
```

```python
import functools
import math

import jax
import jax.numpy as jnp
from jax import lax
from jax.experimental import pallas as pl
from jax.experimental.pallas import tpu as pltpu

F32 = jnp.float32
BF16 = jnp.bfloat16

D_MODEL = 2048
CHUNK = 64
EPS = 1e-6
LN_EPS = 1e-5
A_GROUPS = 4
A_HEAD = 128
A_WIDTH = 512
A_CHUNK = 128
B_HEADS = 4
B_DK = 128
B_DV = 128
B_WIDTH = 512
B_QKV = 1536
B_CONV = 4
C_HEADS = 8
C_NOPE = 128
C_ROPE = 64
C_VDIM = 128
C_WIDTH = 1024
Q_LORA = 512
KV_LORA = 256
ROPE_THETA = 10000.0
C_SCALE = (C_NOPE + C_ROPE) ** -0.5
Q_SCALE = C_SCALE * math.log2(math.e)
D_FF = 4 * D_MODEL

OFF_AU = 0
OFF_AV = OFF_AU + A_WIDTH
OFF_BQKV = OFF_AV + A_WIDTH
OFF_BZ = OFF_BQKV + B_QKV
OFF_BBETA = OFF_BZ + B_WIDTH
OFF_BALPHA = OFF_BBETA + B_HEADS
OFF_CQ = OFF_BALPHA + B_HEADS
OFF_CKV = OFF_CQ + Q_LORA
OFF_CKR = OFF_CKV + KV_LORA
IN_COLS = OFF_CKR + C_ROPE

P_AU = OFF_AU
P_AV = OFF_AV
P_BQKV = OFF_BQKV
P_BZ = OFF_BZ
P1_COLS = OFF_BBETA
P_CQ = 0
P_CKV = 512
P_G1 = 768
P_G2 = 896
P2_COLS = 1024
LANE = 128
G1_BETA = C_ROPE
G1_ALPHA = C_ROPE + B_HEADS
KCAT = KV_LORA + 2 * C_ROPE
QK_DIM = C_NOPE + 2 * C_ROPE

VMEM_LIMIT = 56 * 1024 * 1024


def _mm(a, b):
    return jnp.dot(a, b, preferred_element_type=F32)


def _mm_nt(a, b):
    return lax.dot_general(a, b, (((1,), (1,)), ((), ())), preferred_element_type=F32)


def _split2(x):
    hi = x.astype(BF16)
    lo = (x - hi.astype(F32)).astype(BF16)
    return hi, lo


def _split3(x):
    hi = x.astype(BF16)
    r = x - hi.astype(F32)
    mid = r.astype(BF16)
    lo = (r - mid.astype(F32)).astype(BF16)
    return hi, mid, lo


def _cat3_lhs(a):
    ah, al = _split2(a)
    return jnp.concatenate([ah, ah, al], axis=1)


def _cat3_rhs(b):
    bh, bl = _split2(b)
    return jnp.concatenate([bh, bl, bh], axis=0)


def _mm3(a, b):
    return _mm(_cat3_lhs(a), _cat3_rhs(b))


def _mm_exact_lhs(a, b):
    return _mm(jnp.concatenate([a, a, a], axis=1), jnp.concatenate(_split3(b), axis=0))


def _mm_nt_exact_lhs(a, b):
    return _mm_nt(jnp.concatenate([a, a, a], axis=1), jnp.concatenate(_split3(b), axis=1))


def _lane_tile(x, n):
    return x if n == 1 else jnp.concatenate([x] * n, axis=1)


def _params(sem):
    return pltpu.CompilerParams(dimension_semantics=sem, vmem_limit_bytes=VMEM_LIMIT)


def _inproj_kernel(x_ref, nw_ref, w_ref, o_ref):
    x = x_ref[...]
    ms = jnp.mean(x * x, axis=-1, keepdims=True)
    xn = (x * lax.rsqrt(ms + EPS) * nw_ref[...]).astype(BF16)
    o_ref[...] = _mm(xn, w_ref[...])


def _inproj(x, norm_w, w_pad, layer, ncols):
    m = x.shape[0]
    tm = min(m, 512)
    tn = ncols
    return pl.pallas_call(
        _inproj_kernel,
        out_shape=jax.ShapeDtypeStruct((m, ncols), F32),
        grid=(ncols // tn, m // tm),
        in_specs=[pl.BlockSpec((tm, D_MODEL), lambda j, i: (i, 0)),
                  pl.BlockSpec((1, D_MODEL), lambda j, i: (0, 0)),
                  pl.BlockSpec((None, D_MODEL, tn), lambda j, i: (layer, 0, j))],
        out_specs=pl.BlockSpec((tm, tn), lambda j, i: (i, j)),
        compiler_params=_params(("arbitrary", "arbitrary")),
        name="inproj",
    )(x, norm_w.reshape(1, D_MODEL), w_pad)


def _gelu(x):
    return 0.5 * x * (1.0 + lax.erf(x * (2.0 ** -0.5)))


def _mixa_kernel(u_ref, v_ref, lnw_ref, lnb_ref, ws_ref, bs_ref, ya_ref, *va_ref, lc, nsub):
    u = _gelu(u_ref[0])
    v = _gelu(v_ref[0])
    mu = jnp.mean(v, axis=-1, keepdims=True)
    vc = v - mu
    var = jnp.mean(vc * vc, axis=-1, keepdims=True)
    vn = vc * lax.rsqrt(var + LN_EPS) * lnw_ref[...] + lnb_ref[...]
    if va_ref:
        va_ref[0][0] = vn
    for s in range(nsub):
        rows = slice(s * lc, (s + 1) * lc)
        for g in range(A_GROUPS):
            cols = slice(g * A_HEAD, (g + 1) * A_HEAD)
            f = _mm(ws_ref[g], vn[rows, cols].astype(BF16)) + bs_ref[g]
            ya_ref[0, rows, cols] = (u[rows, cols] * f).astype(BF16)


def _mixer_a(proj, ln_w, ln_b, w_s, b_s, want_va):
    bsz, L, _ = proj.shape
    lc = min(L, A_CHUNK)
    rows = min(L, 512)
    nsub = rows // lc
    idx = jnp.arange(lc)
    allowed = (idx[:, None] // CHUNK) >= (idx[None, :] // CHUNK)
    ws = jnp.where(allowed[None], w_s[:, :lc, :lc], 0.0).astype(BF16)
    bs = jnp.broadcast_to(b_s[:, :lc, None], (A_GROUPS, lc, A_HEAD)).astype(F32)
    out_shape = [jax.ShapeDtypeStruct((bsz, L, A_WIDTH), BF16)]
    out_specs = [pl.BlockSpec((1, rows, A_WIDTH), lambda b, i: (b, i, 0))]
    if want_va:
        out_shape.append(jax.ShapeDtypeStruct((bsz, L, A_WIDTH), F32))
        out_specs.append(pl.BlockSpec((1, rows, A_WIDTH), lambda b, i: (b, i, 0)))
    res = pl.pallas_call(
        functools.partial(_mixa_kernel, lc=lc, nsub=nsub),
        out_shape=out_shape,
        grid=(bsz, L // rows),
        in_specs=[pl.BlockSpec((1, rows, A_WIDTH), lambda b, i: (b, i, P_AU // A_WIDTH)),
                  pl.BlockSpec((1, rows, A_WIDTH), lambda b, i: (b, i, P_AV // A_WIDTH)),
                  pl.BlockSpec((1, A_WIDTH), lambda b, i: (0, 0)),
                  pl.BlockSpec((1, A_WIDTH), lambda b, i: (0, 0)),
                  pl.BlockSpec((A_GROUPS, lc, lc), lambda b, i: (0, 0, 0)),
                  pl.BlockSpec((A_GROUPS, lc, A_HEAD), lambda b, i: (0, 0, 0))],
        out_specs=out_specs,
        compiler_params=_params(("arbitrary", "arbitrary")),
        name="mixer_a",
    )(proj, proj, ln_w.reshape(1, A_WIDTH), ln_b.reshape(1, A_WIDTH), ws, bs)
    return res if want_va else (res[0], None)


def _softplus(x):
    return jnp.maximum(x, 0.0) + jnp.log1p(jnp.exp(-jnp.abs(x)))


def _gdn_prep_kernel(xq_ref, xk_ref, xv_ref, hq_ref, hk_ref, hv_ref, buf_ref, g1_ref, cw_ref, alog_ref, dtb_ref,
                     u_ref, w_ref, qg_ref, kgt_ref, qk_ref, eg_ref, xp_scr, *, nb, rows, chunk):
    i = pl.program_id(1)
    nsub = rows // chunk
    ri = lax.broadcasted_iota(jnp.int32, (rows, rows), 0)
    ci = lax.broadcasted_iota(jnp.int32, (rows, rows), 1)
    same = (ri // chunk) == (ci // chunk)
    causal = same & (ri >= ci)
    strict = same & (ri > ci)
    tri = jnp.where(causal, 1.0, 0.0).astype(BF16)
    eye_r = jnp.where(ri == ci, 1.0, 0.0).astype(F32)
    li = lax.broadcasted_iota(jnp.int32, (LANE, LANE), 0)
    lj = lax.broadcasted_iota(jnp.int32, (LANE, LANE), 1)
    eye_l = jnp.where(li == lj, 1.0, 0.0).astype(BF16)

    ys, betas, gts = [], [], []
    for bb in range(nb):
        x = jnp.concatenate([xq_ref[bb], xk_ref[bb], xv_ref[bb]], axis=1)
        halo = jnp.concatenate([hq_ref[bb], hk_ref[bb], hv_ref[bb]], axis=1)
        xp_scr[bb, 0:8, :] = jnp.where(i == 0, buf_ref[bb], halo)
        xp_scr[bb, 8:8 + rows, :] = x
        y = x * cw_ref[B_CONV - 1:B_CONV, :]
        for k in range(1, B_CONV):
            y = y + xp_scr[bb, 8 - k:8 - k + rows, :] * cw_ref[B_CONV - 1 - k:B_CONV - k, :]
        ys.append(y * jax.nn.sigmoid(y))
        g1 = g1_ref[bb]
        betas.append(jax.nn.sigmoid(g1))
        gts.append(-jnp.exp(alog_ref[...]) * _softplus(g1 + dtb_ref[...]))

    gcs = [_mm_exact_lhs(tri, g) for g in gts]
    gcts = [_mm_nt_exact_lhs(eye_l, gc) for gc in gcs]
    glasts = [jnp.concatenate(
        [jnp.broadcast_to(gc[(s + 1) * chunk - 1:(s + 1) * chunk, :], (chunk, LANE)) for s in range(nsub)], axis=0)
        for gc in gcs]

    pairs = [(bb, h) for bb in range(nb) for h in range(B_HEADS)]
    qs, ks, kbs, rhss, decs, egcs = [], [], [], [], [], []
    for bb, h in pairs:
        y, gc = ys[bb], gcs[bb]
        bcol = betas[bb][:, G1_BETA + h:G1_BETA + h + 1]
        gcol = gc[:, G1_ALPHA + h:G1_ALPHA + h + 1]
        grow = gcts[bb][G1_ALPHA + h:G1_ALPHA + h + 1, :]
        qh = y[:, h * B_DK:(h + 1) * B_DK]
        qh = qh * lax.rsqrt(jnp.sum(qh * qh, axis=-1, keepdims=True) + EPS) * (B_DK ** -0.5)
        kh = y[:, B_WIDTH + h * B_DK:B_WIDTH + (h + 1) * B_DK]
        kh = kh * lax.rsqrt(jnp.sum(kh * kh, axis=-1, keepdims=True) + EPS)
        vh = y[:, 2 * B_WIDTH + h * B_DV:2 * B_WIDTH + (h + 1) * B_DV]
        kb = kh * bcol
        egc = jnp.exp(gcol)
        qs.append(qh)
        ks.append(kh)
        kbs.append(kb)
        egcs.append(egc)
        rhss.append(jnp.concatenate([vh * bcol, kb * egc], axis=1))
        decs.append(jnp.where(causal, jnp.exp(jnp.minimum(gcol - grow, 0.0)), 0.0))

    khb = [k.astype(BF16) for k in ks]
    kk = [_mm_nt(kb.astype(BF16), kx) for kb, kx in zip(kbs, khb)]
    qkm = [_mm_nt(q.astype(BF16), kx) for q, kx in zip(qs, khb)]

    ps = [jnp.where(strict, -(a * d), 0.0) for a, d in zip(kk, decs)]
    ts = [eye_r + p for p in ps]
    ms = [_mm(_cat3_lhs(p), _cat3_rhs(p)) for p in ps]
    nsq = int(math.log2(chunk)) - 1
    for step in range(nsq):
        mr = [_cat3_rhs(m) for m in ms]
        upd = [_mm(_cat3_lhs(t), b) for t, b in zip(ts, mr)]
        if step + 1 < nsq:
            ms = [_mm(_cat3_lhs(m), b) for m, b in zip(ms, mr)]
        ts = [t + d for t, d in zip(ts, upd)]
    sols = [_mm3(t, r) for t, r in zip(ts, rhss)]

    for n, (bb, h) in enumerate(pairs):
        hs = slice(h * B_DK, (h + 1) * B_DK)
        gcol = gcs[bb][:, G1_ALPHA + h:G1_ALPHA + h + 1]
        glcol = glasts[bb][:, G1_ALPHA + h:G1_ALPHA + h + 1]
        u_ref[bb, :, hs] = sols[n][:, :B_DV]
        w_ref[bb, :, hs] = sols[n][:, B_DV:].astype(BF16)
        qg_ref[bb, :, hs] = (qs[n] * egcs[n]).astype(BF16)
        kg = (ks[n] * jnp.exp(glcol - gcol)).astype(BF16)
        kgt_ref[bb, hs, :] = _mm_nt(eye_l, kg).astype(BF16)
        qk_ref[bb, :, h * rows:(h + 1) * rows] = (qkm[n] * decs[n]).astype(BF16)
        for s in range(nsub):
            eg_ref[bb, s, h:h + 1, :] = jnp.broadcast_to(
                jnp.exp(glasts[bb][s * chunk:s * chunk + 1, G1_ALPHA + h:G1_ALPHA + h + 1]), (1, LANE))


def _gdn_prep(proj, proj2, buf8, conv_w, a_log, dt_bias, rows, chunk):
    bsz, L, _ = proj.shape
    nb = 2
    nblk = L // rows
    rpb = max(rows // 8, 1)
    alog = jnp.zeros((1, LANE), F32).at[0, G1_ALPHA:G1_ALPHA + B_HEADS].set(a_log)
    dtb = jnp.zeros((1, LANE), F32).at[0, G1_ALPHA:G1_ALPHA + B_HEADS].set(dt_bias)
    return pl.pallas_call(
        functools.partial(_gdn_prep_kernel, nb=nb, rows=rows, chunk=chunk),
        out_shape=[jax.ShapeDtypeStruct((bsz, L, B_WIDTH), F32),
                   jax.ShapeDtypeStruct((bsz, L, B_WIDTH), BF16),
                   jax.ShapeDtypeStruct((bsz, L, B_WIDTH), BF16),
                   jax.ShapeDtypeStruct((bsz, B_WIDTH, L), BF16),
                   jax.ShapeDtypeStruct((bsz, L, B_HEADS * rows), BF16),
                   jax.ShapeDtypeStruct((bsz, L // chunk, B_HEADS, LANE), F32)],
        grid=(bsz // nb, nblk),
        in_specs=[pl.BlockSpec((nb, rows, B_WIDTH), lambda b, i, c=P_BQKV // B_WIDTH + part: (b, i, c))
                  for part in range(3)] + [
                  pl.BlockSpec((nb, 8, B_WIDTH),
                               lambda b, i, c=P_BQKV // B_WIDTH + part: (b, jnp.maximum(i * rpb - 1, 0), c))
                  for part in range(3)] + [
                  pl.BlockSpec((nb, 8, B_QKV), lambda b, i: (b, 0, 0)),
                  pl.BlockSpec((nb, rows, LANE), lambda b, i: (b, i, P_G1 // LANE)),
                  pl.BlockSpec((B_CONV, B_QKV), lambda b, i: (0, 0)),
                  pl.BlockSpec((1, LANE), lambda b, i: (0, 0)),
                  pl.BlockSpec((1, LANE), lambda b, i: (0, 0))],
        out_specs=[pl.BlockSpec((nb, rows, B_WIDTH), lambda b, i: (b, i, 0)),
                   pl.BlockSpec((nb, rows, B_WIDTH), lambda b, i: (b, i, 0)),
                   pl.BlockSpec((nb, rows, B_WIDTH), lambda b, i: (b, i, 0)),
                   pl.BlockSpec((nb, B_WIDTH, rows), lambda b, i: (b, 0, i)),
                   pl.BlockSpec((nb, rows, B_HEADS * rows), lambda b, i: (b, i, 0)),
                   pl.BlockSpec((nb, rows // chunk, B_HEADS, LANE), lambda b, i: (b, i, 0, 0))],
        scratch_shapes=[pltpu.VMEM((nb, rows + 8, B_QKV), F32)],
        compiler_params=_params(("arbitrary", "arbitrary")),
        name="gdn_prep",
    )(proj, proj, proj, proj, proj, proj, buf8, proj2, conv_w, alog, dtb)


def _gdn_scan_kernel(u_ref, w_ref, qg_ref, kgt_ref, qk_ref, eg_ref, z_ref, nw_ref, s0_ref,
                     yb_ref, s_ref, *, bsz, rows, chunk):
    @pl.when(pl.program_id(0) == 0)
    def _():
        s_ref[...] = s0_ref[...]

    nsub = rows // chunk
    pairs = [(b, h) for b in range(bsz) for h in range(B_HEADS)]
    st = [s_ref[b, h] for b, h in pairs]
    for s in range(nsub):
        rs = slice(s * chunk, (s + 1) * chunk)
        r1 = [_mm(jnp.concatenate([w_ref[b, rs, h * B_DV:(h + 1) * B_DV], qg_ref[b, rs, h * B_DV:(h + 1) * B_DV]],
                                  axis=0), x.astype(BF16))
              for (b, h), x in zip(pairs, st)]
        vb = [(u_ref[b, rs, h * B_DV:(h + 1) * B_DV] - r[:chunk]).astype(BF16) for (b, h), r in zip(pairs, r1)]
        r2 = [_mm(jnp.concatenate([qk_ref[b, rs, h * rows + s * chunk:h * rows + (s + 1) * chunk],
                                   kgt_ref[b, h * B_DK:(h + 1) * B_DK, rs]], axis=0), v)
              for (b, h), v in zip(pairs, vb)]
        for n, (b, h) in enumerate(pairs):
            hs = slice(h * B_DV, (h + 1) * B_DV)
            o = r1[n][chunk:] + r2[n][:chunk]
            st[n] = st[n] * eg_ref[b, s, h:h + 1, :] + r2[n][chunk:]
            z = z_ref[b, rs, hs]
            on = o * lax.rsqrt(jnp.mean(o * o, axis=-1, keepdims=True) + EPS) * nw_ref[...]
            yb_ref[b, rs, hs] = (on * (z * jax.nn.sigmoid(z))).astype(BF16)
    for n, (b, h) in enumerate(pairs):
        s_ref[b, h] = st[n]


def _gdn_scan(prep, proj, norm_w, s0, rows, chunk):
    u, w, qg, kgt, qk, eg = prep
    bsz, L, _ = u.shape
    nblk = L // rows
    return pl.pallas_call(
        functools.partial(_gdn_scan_kernel, bsz=bsz, rows=rows, chunk=chunk),
        out_shape=[jax.ShapeDtypeStruct((bsz, L, B_WIDTH), BF16),
                   jax.ShapeDtypeStruct((bsz, B_HEADS, B_DK, B_DV), F32)],
        grid=(nblk,),
        in_specs=[pl.BlockSpec((bsz, rows, B_WIDTH), lambda i: (0, i, 0)),
                  pl.BlockSpec((bsz, rows, B_WIDTH), lambda i: (0, i, 0)),
                  pl.BlockSpec((bsz, rows, B_WIDTH), lambda i: (0, i, 0)),
                  pl.BlockSpec((bsz, B_WIDTH, rows), lambda i: (0, 0, i)),
                  pl.BlockSpec((bsz, rows, B_HEADS * rows), lambda i: (0, i, 0)),
                  pl.BlockSpec((bsz, rows // chunk, B_HEADS, LANE), lambda i: (0, i, 0, 0)),
                  pl.BlockSpec((bsz, rows, B_WIDTH), lambda i: (0, i, P_BZ // B_WIDTH)),
                  pl.BlockSpec((1, B_DV), lambda i: (0, 0)),
                  pl.BlockSpec((bsz, B_HEADS, B_DK, B_DV), lambda i: (0, 0, 0, 0))],
        out_specs=[pl.BlockSpec((bsz, rows, B_WIDTH), lambda i: (0, i, 0)),
                   pl.BlockSpec((bsz, B_HEADS, B_DK, B_DV), lambda i: (0, 0, 0, 0))],
        compiler_params=_params(("arbitrary",)),
        name="gdn_scan",
    )(u, w, qg, kgt, qk, eg, proj, norm_w.reshape(1, B_DV), s0)


def _mla_common(cq_ref, ckv_ref, g1_ref, g2_ref, cos8_ref, sin8_ref, qnw_ref, wqn_ref, wqr_ref, wqs_ref, kvnw_ref):
    cq = cq_ref[0]
    cqn = (cq * lax.rsqrt(jnp.mean(cq * cq, axis=-1, keepdims=True) + EPS) * qnw_ref[...]).astype(BF16)
    c = ckv_ref[0]
    ckv = c * lax.rsqrt(jnp.mean(c * c, axis=-1, keepdims=True) + EPS) * kvnw_ref[...]
    cos2, sin2 = cos8_ref[...], sin8_ref[...]
    kr = g1_ref[0][:, 0:C_ROPE] * cos2[:, 0:C_ROPE] + g2_ref[0][:, 0:C_ROPE] * sin2[:, 0:C_ROPE]
    qn = _mm(cqn, wqn_ref[...])
    ntile = C_HEADS * C_ROPE // LANE
    qr = (_mm(cqn, wqr_ref[...]) * _lane_tile(cos2, ntile) + _mm(cqn, wqs_ref[...]) * _lane_tile(sin2, ntile)) * Q_SCALE
    return ckv, kr, qn, qr


def _rope_slot(qr, h):
    lane = lax.broadcasted_iota(jnp.int32, (1, LANE), 1)
    pair = qr[:, (h // 2) * LANE:(h // 2 + 1) * LANE]
    return jnp.where((lane // C_ROPE) == (h % 2), pair, 0.0).astype(BF16)


def _write_stacked(out_ref, val, layer, prev_ref):
    for d in range(out_ref.shape[0]):
        if d == layer:
            out_ref[d] = val
        elif prev_ref is None:
            out_ref[d] = jnp.zeros_like(val)
        else:
            out_ref[d] = prev_ref[d]


def _mla_proj_kernel(cq_ref, ckv_ref, g1_ref, g2_ref, cos8_ref, sin8_ref, qnw_ref, wqn_ref, wqr_ref, wqs_ref,
                     kvnw_ref, wukt_ref, ckv_out, kr_out, kcat_out, q_out, *, layer, prev):
    ckv, kr, qn, qr = _mla_common(cq_ref, ckv_ref, g1_ref, g2_ref, cos8_ref, sin8_ref, qnw_ref, wqn_ref, wqr_ref,
                                  wqs_ref, kvnw_ref)
    _write_stacked(ckv_out, ckv, layer, prev and prev[0])
    _write_stacked(kr_out, kr, layer, prev and prev[1])
    kcat_out[0, :, 0:KV_LORA] = ckv.astype(BF16)
    kcat_out[0, :, KV_LORA:KV_LORA + C_ROPE] = kr.astype(BF16)
    kcat_out[0, :, KV_LORA + C_ROPE:KCAT] = kr.astype(BF16)
    for h in range(C_HEADS):
        ql = _mm(qn[:, h * C_NOPE:(h + 1) * C_NOPE].astype(BF16), wukt_ref[h]) * Q_SCALE
        q_out[0, h, :, 0:KV_LORA] = ql.astype(BF16)
        q_out[0, h, :, KV_LORA:KCAT] = _rope_slot(qr, h)


def _mla_heads_kernel(cq_ref, ckv_ref, g1_ref, g2_ref, cos8_ref, sin8_ref, qnw_ref, wqn_ref, wqr_ref, wqs_ref,
                      kvnw_ref, wk_ref, wv_ref, ckv_out, kr_out, q_out, k_out, kr2_out, v_out, *, layer, prev):
    ckv, kr, qn, qr = _mla_common(cq_ref, ckv_ref, g1_ref, g2_ref, cos8_ref, sin8_ref, qnw_ref, wqn_ref, wqr_ref,
                                  wqs_ref, kvnw_ref)
    _write_stacked(ckv_out, ckv, layer, prev and prev[0])
    _write_stacked(kr_out, kr, layer, prev and prev[1])
    ckvb = ckv.astype(BF16)
    kn = _mm(ckvb, wk_ref[...])
    vv = _mm(ckvb, wv_ref[...])
    krb = kr.astype(BF16)
    kr2_out[0] = jnp.concatenate([krb, krb], axis=1)
    for h in range(C_HEADS):
        hs = slice(h * C_NOPE, (h + 1) * C_NOPE)
        q_out[0, h, :, 0:C_NOPE] = (qn[:, hs] * Q_SCALE).astype(BF16)
        q_out[0, h, :, C_NOPE:QK_DIM] = _rope_slot(qr, h)
        k_out[0, h] = kn[:, hs].astype(BF16)
        v_out[0, h] = vv[:, h * C_VDIM:(h + 1) * C_VDIM].astype(BF16)


def _mla_proj(proj, cos8, sin8, q_norm_w, wqn, wqr, wqs, kv_norm_w, head_w, absorbed, layer, depth, stacks):
    bsz, L, _ = proj.shape
    tm = min(L, 512)
    const2 = lambda b, i: (0, 0)
    head_out = lambda width: (jax.ShapeDtypeStruct((bsz, C_HEADS, L, width), BF16),
                              pl.BlockSpec((1, C_HEADS, tm, width), lambda b, i: (b, 0, i, 0)))
    stack_out = lambda width: (jax.ShapeDtypeStruct((depth, bsz, L, width), F32),
                               pl.BlockSpec((depth, None, tm, width), lambda b, i: (0, b, i, 0)))
    outs = [stack_out(KV_LORA), stack_out(C_ROPE)]
    if absorbed:
        body = _mla_proj_kernel
        w_specs = [pl.BlockSpec((C_HEADS, C_NOPE, KV_LORA), lambda b, i: (0, 0, 0))]
        outs += [(jax.ShapeDtypeStruct((bsz, L, KCAT), BF16), pl.BlockSpec((1, tm, KCAT), lambda b, i: (b, i, 0))),
                 head_out(KCAT)]
    else:
        body = _mla_heads_kernel
        w_specs = [pl.BlockSpec((KV_LORA, C_HEADS * C_NOPE), const2), pl.BlockSpec((KV_LORA, C_HEADS * C_VDIM), const2)]
        outs += [head_out(QK_DIM), head_out(C_NOPE),
                 (jax.ShapeDtypeStruct((bsz, L, LANE), BF16), pl.BlockSpec((1, tm, LANE), lambda b, i: (b, i, 0))),
                 head_out(C_VDIM)]
    if stacks is None:
        kern, prev_specs, prev_args = functools.partial(body, layer=layer, prev=None), [], ()
    else:
        kern = lambda ckv_prev, kr_prev, *refs: body(*refs, layer=layer, prev=(ckv_prev, kr_prev))
        prev_specs, prev_args = [outs[0][1], outs[1][1]], tuple(stacks)
    return pl.pallas_call(
        kern,
        out_shape=[o[0] for o in outs],
        grid=(bsz, L // tm),
        in_specs=prev_specs + [pl.BlockSpec((1, tm, Q_LORA), lambda b, i: (b, i, P_CQ // Q_LORA)),
                  pl.BlockSpec((1, tm, KV_LORA), lambda b, i: (b, i, P_CKV // KV_LORA)),
                  pl.BlockSpec((1, tm, LANE), lambda b, i: (b, i, P_G1 // LANE)),
                  pl.BlockSpec((1, tm, LANE), lambda b, i: (b, i, P_G2 // LANE)),
                  pl.BlockSpec((tm, LANE), lambda b, i: (i, 0)),
                  pl.BlockSpec((tm, LANE), lambda b, i: (i, 0)),
                  pl.BlockSpec((1, Q_LORA), const2),
                  pl.BlockSpec((Q_LORA, C_HEADS * C_NOPE), const2),
                  pl.BlockSpec((Q_LORA, C_HEADS * C_ROPE), const2),
                  pl.BlockSpec((Q_LORA, C_HEADS * C_ROPE), const2),
                  pl.BlockSpec((1, KV_LORA), const2)] + w_specs,
        out_specs=[o[1] for o in outs],
        compiler_params=_params(("arbitrary", "arbitrary")),
        name="mla_proj" if absorbed else "mla_heads",
    )(*prev_args, proj, proj, proj, proj, cos8, sin8, q_norm_w.reshape(1, Q_LORA), wqn, wqr, wqs,
      kv_norm_w.reshape(1, KV_LORA), *head_w)


def _attn_heads_kernel(q_ref, k_ref, kr2_ref, v_ref, yc_ref, m_scr, l_scr, acc_scr, *, tq, tk, hps):
    qi = pl.program_id(2)
    rows = hps * tq
    m_scr[...] = jnp.full((rows, LANE), -1e30, F32)
    l_scr[...] = jnp.zeros((rows, LANE), F32)
    acc_scr[...] = jnp.zeros((rows, C_VDIM), F32)

    def block(kj, masked):
        start = pl.multiple_of(kj * tk, tk)
        kr2 = kr2_ref[0, pl.ds(start, tk), :]
        if masked:
            ri = lax.broadcasted_iota(jnp.int32, (tq, tk), 0)
            ci = lax.broadcasted_iota(jnp.int32, (tq, tk), 1)
            ok = ((qi * tq + ri) // CHUNK) >= ((start + ci) // CHUNK)

        def scores(g):
            return _mm_nt(q_ref[0, g], jnp.concatenate([k_ref[0, g, pl.ds(start, tk), :], kr2], axis=1))

        s_next = scores(0)
        for g in range(hps):
            rs = slice(g * tq, (g + 1) * tq)
            s = s_next
            if g + 1 < hps:
                s_next = scores(g + 1)
            if masked:
                s = jnp.where(ok, s, -jnp.inf)
            m_prev = m_scr[rs]
            m_new = jnp.maximum(m_prev, jnp.max(s, axis=-1, keepdims=True))
            alpha = jnp.exp2(m_prev - m_new)
            p = jnp.exp2(s - _lane_tile(m_new, tk // LANE))
            l_scr[rs] = alpha * l_scr[rs] + jnp.sum(p, axis=-1, keepdims=True)
            acc_scr[rs] = acc_scr[rs] * alpha + _mm(p.astype(BF16), v_ref[0, g, pl.ds(start, tk), :])
            m_scr[rs] = m_new

    def body(kj, carry):
        block(kj, False)
        return carry

    nfull = (qi * tq) // tk
    lax.fori_loop(0, nfull, body, 0)
    block(nfull, True)
    for g in range(hps):
        rs = slice(g * tq, (g + 1) * tq)
        yc_ref[0, :, g * C_VDIM:(g + 1) * C_VDIM] = (acc_scr[rs] * (1.0 / l_scr[rs])).astype(BF16)


def _attn_heads(q, k, kr2, v):
    bsz, _, L, _ = q.shape
    tq = min(L, 512)
    tk = min(L, 512)
    hps = 4
    rows = hps * tq
    return pl.pallas_call(
        functools.partial(_attn_heads_kernel, tq=tq, tk=tk, hps=hps),
        out_shape=jax.ShapeDtypeStruct((bsz, L, C_WIDTH), BF16),
        grid=(bsz, C_HEADS // hps, L // tq),
        in_specs=[pl.BlockSpec((1, hps, tq, QK_DIM), lambda b, g, i: (b, g, i, 0)),
                  pl.BlockSpec((1, hps, L, C_NOPE), lambda b, g, i: (b, g, 0, 0)),
                  pl.BlockSpec((1, L, LANE), lambda b, g, i: (b, 0, 0)),
                  pl.BlockSpec((1, hps, L, C_VDIM), lambda b, g, i: (b, g, 0, 0))],
        out_specs=pl.BlockSpec((1, tq, hps * C_VDIM), lambda b, g, i: (b, i, g)),
        scratch_shapes=[pltpu.VMEM((rows, LANE), F32), pltpu.VMEM((rows, LANE), F32),
                        pltpu.VMEM((rows, C_VDIM), F32)],
        compiler_params=_params(("arbitrary", "arbitrary", "arbitrary")),
        name="attn_heads",
    )(q, k, kr2, v)


def _attn_finish(acc, l, wuv_ref, yc_ref, tq):
    o = acc * (1.0 / l)
    for h in range(C_HEADS):
        oh = o[h * tq:(h + 1) * tq].astype(BF16)
        yc_ref[0, :, h * C_VDIM:(h + 1) * C_VDIM] = _mm(oh, wuv_ref[h]).astype(BF16)


def _attn_full_kernel(q_ref, cpast_ref, rpast_ref, knew_ref, wuv_ref, yc_ref, *, tq):
    rows = C_HEADS * tq
    q = q_ref[0].reshape(rows, KCAT)
    rpast = rpast_ref[0].astype(BF16)
    kpast = jnp.concatenate([cpast_ref[0].astype(BF16), rpast, rpast], axis=1)
    knew = knew_ref[0]
    s_past = _mm_nt(q, kpast)
    s_new = _mm_nt(q, knew)
    m = jnp.maximum(jnp.max(s_past, axis=-1, keepdims=True), jnp.max(s_new, axis=-1, keepdims=True))
    p_past = jnp.exp2(s_past - m)
    p_new = jnp.exp2(s_new - m)
    l = jnp.sum(p_past, axis=-1, keepdims=True) + jnp.sum(p_new, axis=-1, keepdims=True)
    acc = _mm(p_past.astype(BF16), kpast[:, 0:KV_LORA]) + _mm(p_new.astype(BF16), knew[:, 0:KV_LORA])
    _attn_finish(acc, l, wuv_ref, yc_ref, tq)


def _attn_full(q, ckv_past, kr_past, kcat, wuv, layer):
    bsz, _, tq, _ = q.shape
    npast = ckv_past.shape[2]
    return pl.pallas_call(
        functools.partial(_attn_full_kernel, tq=tq),
        out_shape=jax.ShapeDtypeStruct((bsz, tq, C_WIDTH), BF16),
        grid=(bsz,),
        in_specs=[pl.BlockSpec((1, C_HEADS, tq, KCAT), lambda b: (b, 0, 0, 0)),
                  pl.BlockSpec((None, 1, npast, KV_LORA), lambda b: (layer, b, 0, 0)),
                  pl.BlockSpec((None, 1, npast, C_ROPE), lambda b: (layer, b, 0, 0)),
                  pl.BlockSpec((1, tq, KCAT), lambda b: (b, 0, 0)),
                  pl.BlockSpec((C_HEADS, KV_LORA, C_VDIM), lambda b: (0, 0, 0))],
        out_specs=pl.BlockSpec((1, tq, C_WIDTH), lambda b: (b, 0, 0)),
        compiler_params=_params(("arbitrary",)),
        name="attn_full",
    )(q, ckv_past, kr_past, kcat, wuv)


def _outproj_kernel(x_ref, ya_ref, yb_ref, yc_ref, wo_ref, o_ref):
    tn = 512
    for j in range(0, D_MODEL, tn):
        acc = _mm(ya_ref[...], wo_ref[0:A_WIDTH, j:j + tn])
        acc = acc + _mm(yb_ref[...], wo_ref[A_WIDTH:A_WIDTH + B_WIDTH, j:j + tn])
        acc = acc + _mm(yc_ref[...], wo_ref[A_WIDTH + B_WIDTH:D_MODEL, j:j + tn])
        o_ref[:, j:j + tn] = x_ref[:, j:j + tn] + acc


def _outproj(x, ya, yb, yc, wo, layer):
    m = x.shape[0]
    tm = min(m, 512)
    row = lambda i: (i, 0)
    return pl.pallas_call(
        _outproj_kernel,
        out_shape=jax.ShapeDtypeStruct((m, D_MODEL), F32),
        grid=(m // tm,),
        in_specs=[pl.BlockSpec((tm, D_MODEL), row),
                  pl.BlockSpec((tm, A_WIDTH), row),
                  pl.BlockSpec((tm, B_WIDTH), row),
                  pl.BlockSpec((tm, C_WIDTH), row),
                  pl.BlockSpec((None, D_MODEL, D_MODEL), lambda i: (layer, 0, 0))],
        out_specs=pl.BlockSpec((tm, D_MODEL), row),
        compiler_params=_params(("arbitrary",)),
        name="outproj",
    )(x, ya, yb, yc, wo)


def _ffn_kernel(x_ref, nw_ref, wu_ref, wd_ref, fw_ref, o_ref, xn_scr, *, final_norm):
    f = pl.program_id(1)

    @pl.when(f == 0)
    def _():
        x = x_ref[...]
        ms = jnp.mean(x * x, axis=-1, keepdims=True)
        xn_scr[...] = (x * lax.rsqrt(ms + EPS) * nw_ref[...]).astype(BF16)
        o_ref[...] = x

    hid = jnp.maximum(_mm(xn_scr[...], wu_ref[...]), 0.0)
    o_ref[...] += _mm((hid * hid).astype(BF16), wd_ref[...])

    if final_norm:
        @pl.when(f == pl.num_programs(1) - 1)
        def _():
            y = o_ref[...]
            ms = jnp.mean(y * y, axis=-1, keepdims=True)
            o_ref[...] = y * lax.rsqrt(ms + EPS) * fw_ref[...]


def _ffn(x, norm_w, wu, wd, final_w, final_norm, layer):
    m = x.shape[0]
    tm = min(m, 1024)
    tf = 512
    return pl.pallas_call(
        functools.partial(_ffn_kernel, final_norm=final_norm),
        out_shape=jax.ShapeDtypeStruct((m, D_MODEL), F32),
        grid=(m // tm, D_FF // tf),
        in_specs=[pl.BlockSpec((tm, D_MODEL), lambda i, f: (i, 0)),
                  pl.BlockSpec((1, D_MODEL), lambda i, f: (0, 0)),
                  pl.BlockSpec((None, D_MODEL, tf), lambda i, f: (layer, 0, f)),
                  pl.BlockSpec((None, tf, D_MODEL), lambda i, f: (layer, f, 0)),
                  pl.BlockSpec((1, D_MODEL), lambda i, f: (0, 0))],
        out_specs=pl.BlockSpec((tm, D_MODEL), lambda i, f: (i, 0)),
        scratch_shapes=[pltpu.VMEM((tm, D_MODEL), BF16)],
        compiler_params=_params(("arbitrary", "arbitrary")),
        name="ffn",
    )(x, norm_w.reshape(1, D_MODEL), wu, wd, final_w.reshape(1, D_MODEL))


def _swap_halves(w):
    half = C_ROPE // 2
    return jnp.concatenate([w[..., half:], w[..., :half]], axis=-1)


def _prep_w_in(w_in):
    lead = w_in.shape[:-1]
    ckr = w_in[..., OFF_CKR:IN_COLS]
    g1 = jnp.concatenate([ckr, w_in[..., OFF_BBETA:OFF_CQ], jnp.zeros(lead + (LANE - C_ROPE - 2 * B_HEADS,), F32)],
                         axis=-1)
    g2 = jnp.concatenate([_swap_halves(ckr), jnp.zeros(lead + (LANE - C_ROPE,), F32)], axis=-1)
    tail = jnp.concatenate([w_in[..., OFF_CQ:OFF_CKV], w_in[..., OFF_CKV:OFF_CKR], g1, g2], axis=-1)
    return w_in.astype(BF16), tail.astype(BF16)


def _rope_tables(pos):
    half = C_ROPE // 2
    inv = ROPE_THETA ** (-jnp.arange(half, dtype=F32) / half)
    ang = pos.astype(F32)[:, None] * inv[None, :]
    cos, sin = jnp.cos(ang), jnp.sin(ang)
    cos2 = jnp.concatenate([cos, cos, cos, cos], axis=1)
    sin2 = jnp.concatenate([-sin, sin, -sin, sin], axis=1)
    return cos2, sin2


def _layer(x, bsz, L, tabs, lw, past, final_w, final_norm, layer, depth, stacks):
    proj = _inproj(x, lw['attn_norm_w'], lw['w_in'][0], layer, P1_COLS).reshape(bsz, L, P1_COLS)
    proj2 = _inproj(x, lw['attn_norm_w'], lw['w_in'][1], layer, P2_COLS).reshape(bsz, L, P2_COLS)

    ya, va = _mixer_a(proj, lw['a_ln_w'], lw['a_ln_b'], lw['a_w_s'], lw['a_b_s'], past is not None)

    if past is None:
        rows, chunk = 2 * CHUNK, CHUNK
        buf8 = jnp.zeros((bsz, 8, B_QKV), F32)
        s0 = jnp.zeros((bsz, B_HEADS, B_DK, B_DV), F32)
    else:
        rows, chunk = L, L
        ckv_past, kr_past, s0, conv_buf = past
        buf8 = jnp.concatenate([jnp.zeros((bsz, 8 - (B_CONV - 1), B_QKV), F32), conv_buf], axis=1)
    prep = _gdn_prep(proj, proj2, buf8, lw['b_conv_w'], lw['b_a_log'], lw['b_dt_bias'], rows, chunk)
    yb, s_new = _gdn_scan(prep, proj, lw['b_norm_w'], s0, rows, chunk)
    buf_new = proj[:, L - (B_CONV - 1):, P_BQKV:P_BQKV + B_QKV]

    cos8, sin8 = tabs
    mla_args = (proj2, cos8, sin8, lw['c_q_norm_w'], lw['wqn'], lw['wqr'], lw['wqs'], lw['c_kv_norm_w'])
    if past is None:
        ckv, kr, q, k, kr2, v = _mla_proj(*mla_args, (lw['wk'], lw['wv']), False, layer, depth, stacks)
        yc = _attn_heads(q, k, kr2, v)
    else:
        ckv, kr, kcat, q = _mla_proj(*mla_args, (lw['wukt'],), True, layer, depth, stacks)
        yc = _attn_full(q, ckv_past, kr_past, kcat, lw['wuv'], layer)

    m = bsz * L
    x = _outproj(x, ya.reshape(m, A_WIDTH), yb.reshape(m, B_WIDTH), yc.reshape(m, C_WIDTH), lw['w_out'], layer)
    x = _ffn(x, lw['mlp_norm_w'], lw['w_up'], lw['w_down'], final_w, final_norm, layer)
    return x, (ckv, kr, s_new, buf_new, va)


def kernel(x_prompt, x_sample, cache_mla_ckv, cache_mla_krope, state_gdn, state_gdn_conv,
           attn_norm_w, w_in, a_ln_w, a_ln_b, a_w_s, a_b_s, b_conv_w, b_a_log, b_dt_bias, b_norm_w,
           c_q_norm_w, c_w_uq, c_kv_norm_w, c_w_uk, c_w_uv, w_out, mlp_norm_w, w_up, w_down, final_norm_w):
    depth = w_in.shape[0]
    bp, lp, _ = x_prompt.shape
    bs, ls, _ = x_sample.shape
    past_len = cache_mla_ckv.shape[2]
    tabs_p = _rope_tables(jnp.arange(lp, dtype=jnp.int32))
    tabs_s = _rope_tables(past_len + jnp.arange(ls, dtype=jnp.int32))
    hp = x_prompt.reshape(bp * lp, D_MODEL)
    hs = x_sample.reshape(bs * ls, D_MODEL)
    sp_list, ss_list = [], []
    stacks_p = stacks_s = None
    w_in16, w_out16, w_up16, w_down16 = _prep_w_in(w_in), w_out.astype(BF16), w_up.astype(BF16), w_down.astype(BF16)
    for l in range(depth):
        wq_rope = c_w_uq[l][:, :, C_NOPE:]
        lw = {
            'attn_norm_w': attn_norm_w[l], 'w_in': w_in16,
            'a_ln_w': a_ln_w[l], 'a_ln_b': a_ln_b[l], 'a_w_s': a_w_s[l], 'a_b_s': a_b_s[l],
            'b_conv_w': b_conv_w[l], 'b_a_log': b_a_log[l], 'b_dt_bias': b_dt_bias[l], 'b_norm_w': b_norm_w[l],
            'c_q_norm_w': c_q_norm_w[l], 'c_kv_norm_w': c_kv_norm_w[l],
            'wqn': c_w_uq[l][:, :, :C_NOPE].reshape(Q_LORA, C_HEADS * C_NOPE).astype(BF16),
            'wqr': wq_rope.reshape(Q_LORA, C_HEADS * C_ROPE).astype(BF16),
            'wqs': _swap_halves(wq_rope).reshape(Q_LORA, C_HEADS * C_ROPE).astype(BF16),
            'wk': c_w_uk[l].reshape(KV_LORA, C_HEADS * C_NOPE).astype(BF16),
            'wv': c_w_uv[l].reshape(KV_LORA, C_HEADS * C_VDIM).astype(BF16),
            'wukt': jnp.transpose(c_w_uk[l], (1, 2, 0)).astype(BF16),
            'wuv': jnp.transpose(c_w_uv[l], (1, 0, 2)).astype(BF16),
            'w_out': w_out16, 'mlp_norm_w': mlp_norm_w[l], 'w_up': w_up16, 'w_down': w_down16,
        }
        last = l == depth - 1
        hp, sp = _layer(hp, bp, lp, tabs_p, lw, None, final_norm_w, last, l, depth, stacks_p)
        hs, ss = _layer(hs, bs, ls, tabs_s, lw,
                        (cache_mla_ckv, cache_mla_krope, state_gdn[l], state_gdn_conv[l]), final_norm_w, last,
                        l, depth, stacks_s)
        stacks_p, stacks_s = sp[:2], ss[:2]
        sp_list.append(sp)
        ss_list.append(ss)

    def stack(lst, i):
        return jnp.stack([s[i] for s in lst], axis=0)

    return (hp.reshape(bp, lp, D_MODEL), hs.reshape(bs, ls, D_MODEL),
            stacks_p[0], stacks_p[1], stack(sp_list, 2), stack(sp_list, 3),
            stacks_s[0], stacks_s[1], stack(ss_list, 2), stack(ss_list, 3), stack(ss_list, 4))
```

```python
import functools
import math

import jax
import jax.numpy as jnp
from jax import lax
from jax.experimental import pallas as pl
from jax.experimental.pallas import tpu as pltpu

F32 = jnp.float32
BF16 = jnp.bfloat16

D_MODEL = 2048
CHUNK = 64
EPS = 1e-6
LN_EPS = 1e-5
A_GROUPS = 4
A_HEAD = 128
A_WIDTH = 512
A_CHUNK = 128
B_HEADS = 4
B_DK = 128
B_DV = 128
B_WIDTH = 512
B_QKV = 1536
B_CONV = 4
C_HEADS = 8
C_NOPE = 128
C_ROPE = 64
C_VDIM = 128
C_WIDTH = 1024
Q_LORA = 512
KV_LORA = 256
ROPE_THETA = 10000.0
C_SCALE = (C_NOPE + C_ROPE) ** -0.5
Q_SCALE = C_SCALE * math.log2(math.e)
D_FF = 4 * D_MODEL

OFF_AU = 0
OFF_AV = OFF_AU + A_WIDTH
OFF_BQKV = OFF_AV + A_WIDTH
OFF_BZ = OFF_BQKV + B_QKV
OFF_BBETA = OFF_BZ + B_WIDTH
OFF_BALPHA = OFF_BBETA + B_HEADS
OFF_CQ = OFF_BALPHA + B_HEADS
OFF_CKV = OFF_CQ + Q_LORA
OFF_CKR = OFF_CKV + KV_LORA
IN_COLS = OFF_CKR + C_ROPE

P_AU = OFF_AU
P_AV = OFF_AV
P_BQKV = OFF_BQKV
P_BZ = OFF_BZ
P1_COLS = OFF_BBETA
P_CQ = 0
P_CKV = 512
P_G1 = 768
P_G2 = 896
P2_COLS = 1024
LANE = 128
G1_BETA = C_ROPE
G1_ALPHA = C_ROPE + B_HEADS
KCAT = KV_LORA + 2 * C_ROPE
QK_DIM = C_NOPE + 2 * C_ROPE

VMEM_LIMIT = 56 * 1024 * 1024


def _mm(a, b):
    return jnp.dot(a, b, preferred_element_type=F32)


def _mm_nt(a, b):
    return lax.dot_general(a, b, (((1,), (1,)), ((), ())), preferred_element_type=F32)


def _split2(x):
    hi = x.astype(BF16)
    lo = (x - hi.astype(F32)).astype(BF16)
    return hi, lo


def _split3(x):
    hi = x.astype(BF16)
    r = x - hi.astype(F32)
    mid = r.astype(BF16)
    lo = (r - mid.astype(F32)).astype(BF16)
    return hi, mid, lo


def _cat3_lhs(a):
    ah, al = _split2(a)
    return jnp.concatenate([ah, ah, al], axis=1)


def _cat3_rhs(b):
    bh, bl = _split2(b)
    return jnp.concatenate([bh, bl, bh], axis=0)


def _mm3(a, b):
    return _mm(_cat3_lhs(a), _cat3_rhs(b))


def _mm_exact_lhs(a, b):
    return _mm(jnp.concatenate([a, a, a], axis=1), jnp.concatenate(_split3(b), axis=0))


def _mm_nt_exact_lhs(a, b):
    return _mm_nt(jnp.concatenate([a, a, a], axis=1), jnp.concatenate(_split3(b), axis=1))


def _lane_tile(x, n):
    return x if n == 1 else jnp.concatenate([x] * n, axis=1)


def _params(sem):
    return pltpu.CompilerParams(dimension_semantics=sem, vmem_limit_bytes=VMEM_LIMIT)


def _inproj_kernel(x_ref, nw_ref, w_ref, o_ref):
    x = x_ref[...]
    ms = jnp.mean(x * x, axis=-1, keepdims=True)
    xn = (x * lax.rsqrt(ms + EPS) * nw_ref[...]).astype(BF16)
    o_ref[...] = _mm(xn, w_ref[...])


def _inproj(x, norm_w, w_pad, layer, ncols):
    m = x.shape[0]
    tm = min(m, 512)
    tn = ncols
    return pl.pallas_call(
        _inproj_kernel,
        out_shape=jax.ShapeDtypeStruct((m, ncols), F32),
        grid=(ncols // tn, m // tm),
        in_specs=[pl.BlockSpec((tm, D_MODEL), lambda j, i: (i, 0)),
                  pl.BlockSpec((1, D_MODEL), lambda j, i: (0, 0)),
                  pl.BlockSpec((None, D_MODEL, tn), lambda j, i: (layer, 0, j))],
        out_specs=pl.BlockSpec((tm, tn), lambda j, i: (i, j)),
        compiler_params=_params(("arbitrary", "arbitrary")),
        name="inproj",
    )(x, norm_w.reshape(1, D_MODEL), w_pad)


def _gelu(x):
    return 0.5 * x * (1.0 + lax.erf(x * (2.0 ** -0.5)))


def _mixa_kernel(u_ref, v_ref, lnw_ref, lnb_ref, ws_ref, bs_ref, ya_ref, *va_ref, lc, nsub):
    u = _gelu(u_ref[0])
    v = _gelu(v_ref[0])
    mu = jnp.mean(v, axis=-1, keepdims=True)
    vc = v - mu
    var = jnp.mean(vc * vc, axis=-1, keepdims=True)
    vn = vc * lax.rsqrt(var + LN_EPS) * lnw_ref[...] + lnb_ref[...]
    if va_ref:
        va_ref[0][0] = vn
    for s in range(nsub):
        rows = slice(s * lc, (s + 1) * lc)
        for g in range(A_GROUPS):
            cols = slice(g * A_HEAD, (g + 1) * A_HEAD)
            f = _mm(ws_ref[g], vn[rows, cols].astype(BF16)) + bs_ref[g]
            ya_ref[0, rows, cols] = (u[rows, cols] * f).astype(BF16)


def _mixer_a(proj, ln_w, ln_b, w_s, b_s, want_va):
    bsz, L, _ = proj.shape
    lc = min(L, A_CHUNK)
    rows = min(L, 512)
    nsub = rows // lc
    idx = jnp.arange(lc)
    allowed = (idx[:, None] // CHUNK) >= (idx[None, :] // CHUNK)
    ws = jnp.where(allowed[None], w_s[:, :lc, :lc], 0.0).astype(BF16)
    bs = jnp.broadcast_to(b_s[:, :lc, None], (A_GROUPS, lc, A_HEAD)).astype(F32)
    out_shape = [jax.ShapeDtypeStruct((bsz, L, A_WIDTH), BF16)]
    out_specs = [pl.BlockSpec((1, rows, A_WIDTH), lambda b, i: (b, i, 0))]
    if want_va:
        out_shape.append(jax.ShapeDtypeStruct((bsz, L, A_WIDTH), F32))
        out_specs.append(pl.BlockSpec((1, rows, A_WIDTH), lambda b, i: (b, i, 0)))
    res = pl.pallas_call(
        functools.partial(_mixa_kernel, lc=lc, nsub=nsub),
        out_shape=out_shape,
        grid=(bsz, L // rows),
        in_specs=[pl.BlockSpec((1, rows, A_WIDTH), lambda b, i: (b, i, P_AU // A_WIDTH)),
                  pl.BlockSpec((1, rows, A_WIDTH), lambda b, i: (b, i, P_AV // A_WIDTH)),
                  pl.BlockSpec((1, A_WIDTH), lambda b, i: (0, 0)),
                  pl.BlockSpec((1, A_WIDTH), lambda b, i: (0, 0)),
                  pl.BlockSpec((A_GROUPS, lc, lc), lambda b, i: (0, 0, 0)),
                  pl.BlockSpec((A_GROUPS, lc, A_HEAD), lambda b, i: (0, 0, 0))],
        out_specs=out_specs,
        compiler_params=_params(("arbitrary", "arbitrary")),
        name="mixer_a",
    )(proj, proj, ln_w.reshape(1, A_WIDTH), ln_b.reshape(1, A_WIDTH), ws, bs)
    return res if want_va else (res[0], None)


def _softplus(x):
    return jnp.maximum(x, 0.0) + jnp.log1p(jnp.exp(-jnp.abs(x)))


def _gdn_prep_kernel(xq_ref, xk_ref, xv_ref, hq_ref, hk_ref, hv_ref, buf_ref, g1_ref, cw_ref, alog_ref, dtb_ref,
                     u_ref, w_ref, qg_ref, kgt_ref, qk_ref, eg_ref, xp_scr, *, nb, rows, chunk):
    i = pl.program_id(1)
    nsub = rows // chunk
    ri = lax.broadcasted_iota(jnp.int32, (rows, rows), 0)
    ci = lax.broadcasted_iota(jnp.int32, (rows, rows), 1)
    same = (ri // chunk) == (ci // chunk)
    causal = same & (ri >= ci)
    strict = same & (ri > ci)
    tri = jnp.where(causal, 1.0, 0.0).astype(BF16)
    eye_r = jnp.where(ri == ci, 1.0, 0.0).astype(F32)
    li = lax.broadcasted_iota(jnp.int32, (LANE, LANE), 0)
    lj = lax.broadcasted_iota(jnp.int32, (LANE, LANE), 1)
    eye_l = jnp.where(li == lj, 1.0, 0.0).astype(BF16)

    ys, betas, gts = [], [], []
    for bb in range(nb):
        x = jnp.concatenate([xq_ref[bb], xk_ref[bb], xv_ref[bb]], axis=1)
        halo = jnp.concatenate([hq_ref[bb], hk_ref[bb], hv_ref[bb]], axis=1)
        xp_scr[bb, 0:8, :] = jnp.where(i == 0, buf_ref[bb], halo)
        xp_scr[bb, 8:8 + rows, :] = x
        y = x * cw_ref[B_CONV - 1:B_CONV, :]
        for k in range(1, B_CONV):
            y = y + xp_scr[bb, 8 - k:8 - k + rows, :] * cw_ref[B_CONV - 1 - k:B_CONV - k, :]
        ys.append(y * jax.nn.sigmoid(y))
        g1 = g1_ref[bb]
        betas.append(jax.nn.sigmoid(g1))
        gts.append(-jnp.exp(alog_ref[...]) * _softplus(g1 + dtb_ref[...]))

    gcs = [_mm_exact_lhs(tri, g) for g in gts]
    gcts = [_mm_nt_exact_lhs(eye_l, gc) for gc in gcs]
    glasts = [jnp.concatenate(
        [jnp.broadcast_to(gc[(s + 1) * chunk - 1:(s + 1) * chunk, :], (chunk, LANE)) for s in range(nsub)], axis=0)
        for gc in gcs]

    pairs = [(bb, h) for bb in range(nb) for h in range(B_HEADS)]
    qs, ks, kbs, rhss, decs, egcs = [], [], [], [], [], []
    for bb, h in pairs:
        y, gc = ys[bb], gcs[bb]
        bcol = betas[bb][:, G1_BETA + h:G1_BETA + h + 1]
        gcol = gc[:, G1_ALPHA + h:G1_ALPHA + h + 1]
        grow = gcts[bb][G1_ALPHA + h:G1_ALPHA + h + 1, :]
        qh = y[:, h * B_DK:(h + 1) * B_DK]
        qh = qh * lax.rsqrt(jnp.sum(qh * qh, axis=-1, keepdims=True) + EPS) * (B_DK ** -0.5)
        kh = y[:, B_WIDTH + h * B_DK:B_WIDTH + (h + 1) * B_DK]
        kh = kh * lax.rsqrt(jnp.sum(kh * kh, axis=-1, keepdims=True) + EPS)
        vh = y[:, 2 * B_WIDTH + h * B_DV:2 * B_WIDTH + (h + 1) * B_DV]
        kb = kh * bcol
        egc = jnp.exp(gcol)
        qs.append(qh)
        ks.append(kh)
        kbs.append(kb)
        egcs.append(egc)
        rhss.append(jnp.concatenate([vh * bcol, kb * egc], axis=1))
        decs.append(jnp.where(causal, jnp.exp(jnp.minimum(gcol - grow, 0.0)), 0.0))

    khb = [k.astype(BF16) for k in ks]
    kk = [_mm_nt(kb.astype(BF16), kx) for kb, kx in zip(kbs, khb)]
    qkm = [_mm_nt(q.astype(BF16), kx) for q, kx in zip(qs, khb)]

    ps = [jnp.where(strict, -(a * d), 0.0) for a, d in zip(kk, decs)]
    ts = [eye_r + p for p in ps]
    ms = [_mm(_cat3_lhs(p), _cat3_rhs(p)) for p in ps]
    nsq = int(math.log2(chunk)) - 1
    for step in range(nsq):
        mr = [_cat3_rhs(m) for m in ms]
        upd = [_mm(_cat3_lhs(t), b) for t, b in zip(ts, mr)]
        if step + 1 < nsq:
            ms = [_mm(_cat3_lhs(m), b) for m, b in zip(ms, mr)]
        ts = [t + d for t, d in zip(ts, upd)]
    sols = [_mm3(t, r) for t, r in zip(ts, rhss)]

    for n, (bb, h) in enumerate(pairs):
        hs = slice(h * B_DK, (h + 1) * B_DK)
        gcol = gcs[bb][:, G1_ALPHA + h:G1_ALPHA + h + 1]
        glcol = glasts[bb][:, G1_ALPHA + h:G1_ALPHA + h + 1]
        u_ref[bb, :, hs] = sols[n][:, :B_DV]
        w_ref[bb, :, hs] = sols[n][:, B_DV:].astype(BF16)
        qg_ref[bb, :, hs] = (qs[n] * egcs[n]).astype(BF16)
        kg = (ks[n] * jnp.exp(glcol - gcol)).astype(BF16)
        kgt_ref[bb, hs, :] = _mm_nt(eye_l, kg).astype(BF16)
        qk_ref[bb, :, h * rows:(h + 1) * rows] = (qkm[n] * decs[n]).astype(BF16)
        for s in range(nsub):
            eg_ref[bb, s, h:h + 1, :] = jnp.broadcast_to(
                jnp.exp(glasts[bb][s * chunk:s * chunk + 1, G1_ALPHA + h:G1_ALPHA + h + 1]), (1, LANE))


def _gdn_prep(proj, proj2, buf8, conv_w, a_log, dt_bias, rows, chunk):
    bsz, L, _ = proj.shape
    nb = 2
    nblk = L // rows
    rpb = max(rows // 8, 1)
    alog = jnp.zeros((1, LANE), F32).at[0, G1_ALPHA:G1_ALPHA + B_HEADS].set(a_log)
    dtb = jnp.zeros((1, LANE), F32).at[0, G1_ALPHA:G1_ALPHA + B_HEADS].set(dt_bias)
    return pl.pallas_call(
        functools.partial(_gdn_prep_kernel, nb=nb, rows=rows, chunk=chunk),
        out_shape=[jax.ShapeDtypeStruct((bsz, L, B_WIDTH), F32),
                   jax.ShapeDtypeStruct((bsz, L, B_WIDTH), BF16),
                   jax.ShapeDtypeStruct((bsz, L, B_WIDTH), BF16),
                   jax.ShapeDtypeStruct((bsz, B_WIDTH, L), BF16),
                   jax.ShapeDtypeStruct((bsz, L, B_HEADS * rows), BF16),
                   jax.ShapeDtypeStruct((bsz, L // chunk, B_HEADS, LANE), F32)],
        grid=(bsz // nb, nblk),
        in_specs=[pl.BlockSpec((nb, rows, B_WIDTH), lambda b, i, c=P_BQKV // B_WIDTH + part: (b, i, c))
                  for part in range(3)] + [
                  pl.BlockSpec((nb, 8, B_WIDTH),
                               lambda b, i, c=P_BQKV // B_WIDTH + part: (b, jnp.maximum(i * rpb - 1, 0), c))
                  for part in range(3)] + [
                  pl.BlockSpec((nb, 8, B_QKV), lambda b, i: (b, 0, 0)),
                  pl.BlockSpec((nb, rows, LANE), lambda b, i: (b, i, P_G1 // LANE)),
                  pl.BlockSpec((B_CONV, B_QKV), lambda b, i: (0, 0)),
                  pl.BlockSpec((1, LANE), lambda b, i: (0, 0)),
                  pl.BlockSpec((1, LANE), lambda b, i: (0, 0))],
        out_specs=[pl.BlockSpec((nb, rows, B_WIDTH), lambda b, i: (b, i, 0)),
                   pl.BlockSpec((nb, rows, B_WIDTH), lambda b, i: (b, i, 0)),
                   pl.BlockSpec((nb, rows, B_WIDTH), lambda b, i: (b, i, 0)),
                   pl.BlockSpec((nb, B_WIDTH, rows), lambda b, i: (b, 0, i)),
                   pl.BlockSpec((nb, rows, B_HEADS * rows), lambda b, i: (b, i, 0)),
                   pl.BlockSpec((nb, rows // chunk, B_HEADS, LANE), lambda b, i: (b, i, 0, 0))],
        scratch_shapes=[pltpu.VMEM((nb, rows + 8, B_QKV), F32)],
        compiler_params=_params(("arbitrary", "arbitrary")),
        name="gdn_prep",
    )(proj, proj, proj, proj, proj, proj, buf8, proj2, conv_w, alog, dtb)


def _gdn_scan_kernel(u_ref, w_ref, qg_ref, kgt_ref, qk_ref, eg_ref, z_ref, nw_ref, s0_ref,
                     yb_ref, s_ref, *, bsz, rows, prows, chunk):
    @pl.when(pl.program_id(0) == 0)
    def _():
        s_ref[...] = s0_ref[...]

    nsub = rows // chunk
    pairs = [(b, h) for b in range(bsz) for h in range(B_HEADS)]
    st = [s_ref[b, h] for b, h in pairs]
    for s in range(nsub):
        rs = slice(s * chunk, (s + 1) * chunk)
        r1 = [_mm(jnp.concatenate([w_ref[b, rs, h * B_DV:(h + 1) * B_DV], qg_ref[b, rs, h * B_DV:(h + 1) * B_DV]],
                                  axis=0), x.astype(BF16))
              for (b, h), x in zip(pairs, st)]
        vb = [(u_ref[b, rs, h * B_DV:(h + 1) * B_DV] - r[:chunk]).astype(BF16) for (b, h), r in zip(pairs, r1)]
        c0 = (s % (prows // chunk)) * chunk
        r2 = [_mm(jnp.concatenate([qk_ref[b, rs, h * prows + c0:h * prows + c0 + chunk],
                                   kgt_ref[b, h * B_DK:(h + 1) * B_DK, rs]], axis=0), v)
              for (b, h), v in zip(pairs, vb)]
        for n, (b, h) in enumerate(pairs):
            hs = slice(h * B_DV, (h + 1) * B_DV)
            o = r1[n][chunk:] + r2[n][:chunk]
            st[n] = st[n] * eg_ref[b, s, h:h + 1, :] + r2[n][chunk:]
            z = z_ref[b, rs, hs]
            on = o * lax.rsqrt(jnp.mean(o * o, axis=-1, keepdims=True) + EPS) * nw_ref[...]
            yb_ref[b, rs, hs] = (on * (z * jax.nn.sigmoid(z))).astype(BF16)
    for n, (b, h) in enumerate(pairs):
        s_ref[b, h] = st[n]


def _gdn_scan(prep, proj, norm_w, s0, prows, chunk):
    u, w, qg, kgt, qk, eg = prep
    bsz, L, _ = u.shape
    rows = 2 * prows if L % (2 * prows) == 0 else prows
    nblk = L // rows
    return pl.pallas_call(
        functools.partial(_gdn_scan_kernel, bsz=bsz, rows=rows, prows=prows, chunk=chunk),
        out_shape=[jax.ShapeDtypeStruct((bsz, L, B_WIDTH), BF16),
                   jax.ShapeDtypeStruct((bsz, B_HEADS, B_DK, B_DV), F32)],
        grid=(nblk,),
        in_specs=[pl.BlockSpec((bsz, rows, B_WIDTH), lambda i: (0, i, 0)),
                  pl.BlockSpec((bsz, rows, B_WIDTH), lambda i: (0, i, 0)),
                  pl.BlockSpec((bsz, rows, B_WIDTH), lambda i: (0, i, 0)),
                  pl.BlockSpec((bsz, B_WIDTH, rows), lambda i: (0, 0, i)),
                  pl.BlockSpec((bsz, rows, B_HEADS * prows), lambda i: (0, i, 0)),
                  pl.BlockSpec((bsz, rows // chunk, B_HEADS, LANE), lambda i: (0, i, 0, 0)),
                  pl.BlockSpec((bsz, rows, B_WIDTH), lambda i: (0, i, P_BZ // B_WIDTH)),
                  pl.BlockSpec((1, B_DV), lambda i: (0, 0)),
                  pl.BlockSpec((bsz, B_HEADS, B_DK, B_DV), lambda i: (0, 0, 0, 0))],
        out_specs=[pl.BlockSpec((bsz, rows, B_WIDTH), lambda i: (0, i, 0)),
                   pl.BlockSpec((bsz, B_HEADS, B_DK, B_DV), lambda i: (0, 0, 0, 0))],
        compiler_params=_params(("arbitrary",)),
        name="gdn_scan",
    )(u, w, qg, kgt, qk, eg, proj, norm_w.reshape(1, B_DV), s0)


def _mla_common(cq_ref, ckv_ref, g1_ref, g2_ref, cos8_ref, sin8_ref, qnw_ref, wqn_ref, wqr_ref, wqs_ref, kvnw_ref):
    cq = cq_ref[0]
    cqn = (cq * lax.rsqrt(jnp.mean(cq * cq, axis=-1, keepdims=True) + EPS) * qnw_ref[...]).astype(BF16)
    c = ckv_ref[0]
    ckv = c * lax.rsqrt(jnp.mean(c * c, axis=-1, keepdims=True) + EPS) * kvnw_ref[...]
    cos2, sin2 = cos8_ref[...], sin8_ref[...]
    kr = g1_ref[0][:, 0:C_ROPE] * cos2[:, 0:C_ROPE] + g2_ref[0][:, 0:C_ROPE] * sin2[:, 0:C_ROPE]
    qn = _mm(cqn, wqn_ref[...])
    ntile = C_HEADS * C_ROPE // LANE
    qr = (_mm(cqn, wqr_ref[...]) * _lane_tile(cos2, ntile) + _mm(cqn, wqs_ref[...]) * _lane_tile(sin2, ntile)) * Q_SCALE
    return ckv, kr, qn, qr


def _rope_slot(qr, h):
    lane = lax.broadcasted_iota(jnp.int32, (1, LANE), 1)
    pair = qr[:, (h // 2) * LANE:(h // 2 + 1) * LANE]
    return jnp.where((lane // C_ROPE) == (h % 2), pair, 0.0).astype(BF16)


def _write_stacked(out_ref, val, layer, prev_ref):
    for d in range(out_ref.shape[0]):
        if d == layer:
            out_ref[d] = val
        elif prev_ref is None:
            out_ref[d] = jnp.zeros_like(val)
        else:
            out_ref[d] = prev_ref[d]


def _mla_proj_kernel(cq_ref, ckv_ref, g1_ref, g2_ref, cos8_ref, sin8_ref, qnw_ref, wqn_ref, wqr_ref, wqs_ref,
                     kvnw_ref, wukt_ref, ckv_out, kr_out, kcat_out, q_out, *, layer, prev):
    ckv, kr, qn, qr = _mla_common(cq_ref, ckv_ref, g1_ref, g2_ref, cos8_ref, sin8_ref, qnw_ref, wqn_ref, wqr_ref,
                                  wqs_ref, kvnw_ref)
    _write_stacked(ckv_out, ckv, layer, prev and prev[0])
    _write_stacked(kr_out, kr, layer, prev and prev[1])
    kcat_out[0, :, 0:KV_LORA] = ckv.astype(BF16)
    kcat_out[0, :, KV_LORA:KV_LORA + C_ROPE] = kr.astype(BF16)
    kcat_out[0, :, KV_LORA + C_ROPE:KCAT] = kr.astype(BF16)
    for h in range(C_HEADS):
        ql = _mm(qn[:, h * C_NOPE:(h + 1) * C_NOPE].astype(BF16), wukt_ref[h]) * Q_SCALE
        q_out[0, h, :, 0:KV_LORA] = ql.astype(BF16)
        q_out[0, h, :, KV_LORA:KCAT] = _rope_slot(qr, h)


def _mla_heads_kernel(cq_ref, ckv_ref, g1_ref, g2_ref, cos8_ref, sin8_ref, qnw_ref, wqn_ref, wqr_ref, wqs_ref,
                      kvnw_ref, wk_ref, wv_ref, ckv_out, kr_out, q_out, k_out, kr2_out, v_out, *, layer, prev):
    ckv, kr, qn, qr = _mla_common(cq_ref, ckv_ref, g1_ref, g2_ref, cos8_ref, sin8_ref, qnw_ref, wqn_ref, wqr_ref,
                                  wqs_ref, kvnw_ref)
    _write_stacked(ckv_out, ckv, layer, prev and prev[0])
    _write_stacked(kr_out, kr, layer, prev and prev[1])
    ckvb = ckv.astype(BF16)
    kn = _mm(ckvb, wk_ref[...])
    vv = _mm(ckvb, wv_ref[...])
    krb = kr.astype(BF16)
    kr2_out[0] = jnp.concatenate([krb, krb], axis=1)
    for h in range(C_HEADS):
        hs = slice(h * C_NOPE, (h + 1) * C_NOPE)
        q_out[0, h, :, 0:C_NOPE] = (qn[:, hs] * Q_SCALE).astype(BF16)
        q_out[0, h, :, C_NOPE:QK_DIM] = _rope_slot(qr, h)
        k_out[0, h] = kn[:, hs].astype(BF16)
        v_out[0, h] = vv[:, h * C_VDIM:(h + 1) * C_VDIM].astype(BF16)


def _mla_proj(proj, cos8, sin8, q_norm_w, wqn, wqr, wqs, kv_norm_w, head_w, absorbed, layer, depth, stacks):
    bsz, L, _ = proj.shape
    tm = min(L, 512)
    const2 = lambda b, i: (0, 0)
    head_out = lambda width: (jax.ShapeDtypeStruct((bsz, C_HEADS, L, width), BF16),
                              pl.BlockSpec((1, C_HEADS, tm, width), lambda b, i: (b, 0, i, 0)))
    stack_out = lambda width: (jax.ShapeDtypeStruct((depth, bsz, L, width), F32),
                               pl.BlockSpec((depth, None, tm, width), lambda b, i: (0, b, i, 0)))
    outs = [stack_out(KV_LORA), stack_out(C_ROPE)]
    if absorbed:
        body = _mla_proj_kernel
        w_specs = [pl.BlockSpec((C_HEADS, C_NOPE, KV_LORA), lambda b, i: (0, 0, 0))]
        outs += [(jax.ShapeDtypeStruct((bsz, L, KCAT), BF16), pl.BlockSpec((1, tm, KCAT), lambda b, i: (b, i, 0))),
                 head_out(KCAT)]
    else:
        body = _mla_heads_kernel
        w_specs = [pl.BlockSpec((KV_LORA, C_HEADS * C_NOPE), const2), pl.BlockSpec((KV_LORA, C_HEADS * C_VDIM), const2)]
        outs += [head_out(QK_DIM), head_out(C_NOPE),
                 (jax.ShapeDtypeStruct((bsz, L, LANE), BF16), pl.BlockSpec((1, tm, LANE), lambda b, i: (b, i, 0))),
                 head_out(C_VDIM)]
    if stacks is None:
        kern, prev_specs, prev_args = functools.partial(body, layer=layer, prev=None), [], ()
    else:
        kern = lambda ckv_prev, kr_prev, *refs: body(*refs, layer=layer, prev=(ckv_prev, kr_prev))
        prev_specs, prev_args = [outs[0][1], outs[1][1]], tuple(stacks)
    return pl.pallas_call(
        kern,
        out_shape=[o[0] for o in outs],
        grid=(bsz, L // tm),
        in_specs=prev_specs + [pl.BlockSpec((1, tm, Q_LORA), lambda b, i: (b, i, P_CQ // Q_LORA)),
                  pl.BlockSpec((1, tm, KV_LORA), lambda b, i: (b, i, P_CKV // KV_LORA)),
                  pl.BlockSpec((1, tm, LANE), lambda b, i: (b, i, P_G1 // LANE)),
                  pl.BlockSpec((1, tm, LANE), lambda b, i: (b, i, P_G2 // LANE)),
                  pl.BlockSpec((tm, LANE), lambda b, i: (i, 0)),
                  pl.BlockSpec((tm, LANE), lambda b, i: (i, 0)),
                  pl.BlockSpec((1, Q_LORA), const2),
                  pl.BlockSpec((Q_LORA, C_HEADS * C_NOPE), const2),
                  pl.BlockSpec((Q_LORA, C_HEADS * C_ROPE), const2),
                  pl.BlockSpec((Q_LORA, C_HEADS * C_ROPE), const2),
                  pl.BlockSpec((1, KV_LORA), const2)] + w_specs,
        out_specs=[o[1] for o in outs],
        compiler_params=_params(("arbitrary", "arbitrary")),
        name="mla_proj" if absorbed else "mla_heads",
    )(*prev_args, proj, proj, proj, proj, cos8, sin8, q_norm_w.reshape(1, Q_LORA), wqn, wqr, wqs,
      kv_norm_w.reshape(1, KV_LORA), *head_w)


def _attn_heads_kernel(q_ref, k_ref, kr2_ref, v_ref, yc_ref, m_scr, l_scr, acc_scr, *, tq, tk, hps):
    qi = pl.program_id(2)
    rows = hps * tq
    m_scr[...] = jnp.full((rows, LANE), -1e30, F32)
    l_scr[...] = jnp.zeros((rows, LANE), F32)
    acc_scr[...] = jnp.zeros((rows, C_VDIM), F32)

    def block(kj, masked):
        start = pl.multiple_of(kj * tk, tk)
        kr2 = kr2_ref[0, pl.ds(start, tk), :]
        if masked:
            ri = lax.broadcasted_iota(jnp.int32, (tq, tk), 0)
            ci = lax.broadcasted_iota(jnp.int32, (tq, tk), 1)
            ok = ((qi * tq + ri) // CHUNK) >= ((start + ci) // CHUNK)

        def scores(g):
            return _mm_nt(q_ref[0, g], jnp.concatenate([k_ref[0, g, pl.ds(start, tk), :], kr2], axis=1))

        s_next = scores(0)
        for g in range(hps):
            rs = slice(g * tq, (g + 1) * tq)
            s = s_next
            if g + 1 < hps:
                s_next = scores(g + 1)
            if masked:
                s = jnp.where(ok, s, -jnp.inf)
            m_prev = m_scr[rs]
            m_new = jnp.maximum(m_prev, jnp.max(s, axis=-1, keepdims=True))
            alpha = jnp.exp2(m_prev - m_new)
            p = jnp.exp2(s - _lane_tile(m_new, tk // LANE))
            l_scr[rs] = alpha * l_scr[rs] + jnp.sum(p, axis=-1, keepdims=True)
            acc_scr[rs] = acc_scr[rs] * alpha + _mm(p.astype(BF16), v_ref[0, g, pl.ds(start, tk), :])
            m_scr[rs] = m_new

    def body(kj, carry):
        block(kj, False)
        return carry

    nfull = (qi * tq) // tk
    lax.fori_loop(0, nfull, body, 0)
    block(nfull, True)
    for g in range(hps):
        rs = slice(g * tq, (g + 1) * tq)
        yc_ref[0, :, g * C_VDIM:(g + 1) * C_VDIM] = (acc_scr[rs] * (1.0 / l_scr[rs])).astype(BF16)


def _attn_heads(q, k, kr2, v):
    bsz, _, L, _ = q.shape
    tq = min(L, 512)
    tk = min(L, 512)
    hps = 4
    rows = hps * tq
    return pl.pallas_call(
        functools.partial(_attn_heads_kernel, tq=tq, tk=tk, hps=hps),
        out_shape=jax.ShapeDtypeStruct((bsz, L, C_WIDTH), BF16),
        grid=(bsz, C_HEADS // hps, L // tq),
        in_specs=[pl.BlockSpec((1, hps, tq, QK_DIM), lambda b, g, i: (b, g, i, 0)),
                  pl.BlockSpec((1, hps, L, C_NOPE), lambda b, g, i: (b, g, 0, 0)),
                  pl.BlockSpec((1, L, LANE), lambda b, g, i: (b, 0, 0)),
                  pl.BlockSpec((1, hps, L, C_VDIM), lambda b, g, i: (b, g, 0, 0))],
        out_specs=pl.BlockSpec((1, tq, hps * C_VDIM), lambda b, g, i: (b, i, g)),
        scratch_shapes=[pltpu.VMEM((rows, LANE), F32), pltpu.VMEM((rows, LANE), F32),
                        pltpu.VMEM((rows, C_VDIM), F32)],
        compiler_params=_params(("arbitrary", "arbitrary", "arbitrary")),
        name="attn_heads",
    )(q, k, kr2, v)


def _attn_finish(acc, l, wuv_ref, yc_ref, tq):
    o = acc * (1.0 / l)
    for h in range(C_HEADS):
        oh = o[h * tq:(h + 1) * tq].astype(BF16)
        yc_ref[0, :, h * C_VDIM:(h + 1) * C_VDIM] = _mm(oh, wuv_ref[h]).astype(BF16)


def _attn_full_kernel(q_ref, cpast_ref, rpast_ref, knew_ref, wuv_ref, yc_ref, *, tq):
    rows = C_HEADS * tq
    q = q_ref[0].reshape(rows, KCAT)
    rpast = rpast_ref[0].astype(BF16)
    kpast = jnp.concatenate([cpast_ref[0].astype(BF16), rpast, rpast], axis=1)
    knew = knew_ref[0]
    s_past = _mm_nt(q, kpast)
    s_new = _mm_nt(q, knew)
    m = jnp.maximum(jnp.max(s_past, axis=-1, keepdims=True), jnp.max(s_new, axis=-1, keepdims=True))
    p_past = jnp.exp2(s_past - m)
    p_new = jnp.exp2(s_new - m)
    l = jnp.sum(p_past, axis=-1, keepdims=True) + jnp.sum(p_new, axis=-1, keepdims=True)
    acc = _mm(p_past.astype(BF16), kpast[:, 0:KV_LORA]) + _mm(p_new.astype(BF16), knew[:, 0:KV_LORA])
    _attn_finish(acc, l, wuv_ref, yc_ref, tq)


def _attn_full(q, ckv_past, kr_past, kcat, wuv, layer):
    bsz, _, tq, _ = q.shape
    npast = ckv_past.shape[2]
    return pl.pallas_call(
        functools.partial(_attn_full_kernel, tq=tq),
        out_shape=jax.ShapeDtypeStruct((bsz, tq, C_WIDTH), BF16),
        grid=(bsz,),
        in_specs=[pl.BlockSpec((1, C_HEADS, tq, KCAT), lambda b: (b, 0, 0, 0)),
                  pl.BlockSpec((None, 1, npast, KV_LORA), lambda b: (layer, b, 0, 0)),
                  pl.BlockSpec((None, 1, npast, C_ROPE), lambda b: (layer, b, 0, 0)),
                  pl.BlockSpec((1, tq, KCAT), lambda b: (b, 0, 0)),
                  pl.BlockSpec((C_HEADS, KV_LORA, C_VDIM), lambda b: (0, 0, 0))],
        out_specs=pl.BlockSpec((1, tq, C_WIDTH), lambda b: (b, 0, 0)),
        compiler_params=_params(("arbitrary",)),
        name="attn_full",
    )(q, ckv_past, kr_past, kcat, wuv)


def _outproj_kernel(x_ref, ya_ref, yb_ref, yc_ref, wo_ref, o_ref):
    tn = 512
    for j in range(0, D_MODEL, tn):
        acc = _mm(ya_ref[...], wo_ref[0:A_WIDTH, j:j + tn])
        acc = acc + _mm(yb_ref[...], wo_ref[A_WIDTH:A_WIDTH + B_WIDTH, j:j + tn])
        acc = acc + _mm(yc_ref[...], wo_ref[A_WIDTH + B_WIDTH:D_MODEL, j:j + tn])
        o_ref[:, j:j + tn] = x_ref[:, j:j + tn] + acc


def _outproj(x, ya, yb, yc, wo, layer):
    m = x.shape[0]
    tm = min(m, 512)
    row = lambda i: (i, 0)
    return pl.pallas_call(
        _outproj_kernel,
        out_shape=jax.ShapeDtypeStruct((m, D_MODEL), F32),
        grid=(m // tm,),
        in_specs=[pl.BlockSpec((tm, D_MODEL), row),
                  pl.BlockSpec((tm, A_WIDTH), row),
                  pl.BlockSpec((tm, B_WIDTH), row),
                  pl.BlockSpec((tm, C_WIDTH), row),
                  pl.BlockSpec((None, D_MODEL, D_MODEL), lambda i: (layer, 0, 0))],
        out_specs=pl.BlockSpec((tm, D_MODEL), row),
        compiler_params=_params(("arbitrary",)),
        name="outproj",
    )(x, ya, yb, yc, wo)


def _ffn_kernel(x_ref, nw_ref, wu_ref, wd_ref, fw_ref, o_ref, xn_scr, *, final_norm):
    f = pl.program_id(1)

    @pl.when(f == 0)
    def _():
        x = x_ref[...]
        ms = jnp.mean(x * x, axis=-1, keepdims=True)
        xn_scr[...] = (x * lax.rsqrt(ms + EPS) * nw_ref[...]).astype(BF16)
        o_ref[...] = x

    hid = jnp.maximum(_mm(xn_scr[...], wu_ref[...]), 0.0)
    o_ref[...] += _mm((hid * hid).astype(BF16), wd_ref[...])

    if final_norm:
        @pl.when(f == pl.num_programs(1) - 1)
        def _():
            y = o_ref[...]
            ms = jnp.mean(y * y, axis=-1, keepdims=True)
            o_ref[...] = y * lax.rsqrt(ms + EPS) * fw_ref[...]


def _ffn(x, norm_w, wu, wd, final_w, final_norm, layer):
    m = x.shape[0]
    tm = min(m, 1024)
    tf = 512 if tm == 1024 else 2048
    return pl.pallas_call(
        functools.partial(_ffn_kernel, final_norm=final_norm),
        out_shape=jax.ShapeDtypeStruct((m, D_MODEL), F32),
        grid=(m // tm, D_FF // tf),
        in_specs=[pl.BlockSpec((tm, D_MODEL), lambda i, f: (i, 0)),
                  pl.BlockSpec((1, D_MODEL), lambda i, f: (0, 0)),
                  pl.BlockSpec((None, D_MODEL, tf), lambda i, f: (layer, 0, f)),
                  pl.BlockSpec((None, tf, D_MODEL), lambda i, f: (layer, f, 0)),
                  pl.BlockSpec((1, D_MODEL), lambda i, f: (0, 0))],
        out_specs=pl.BlockSpec((tm, D_MODEL), lambda i, f: (i, 0)),
        scratch_shapes=[pltpu.VMEM((tm, D_MODEL), BF16)],
        compiler_params=_params(("arbitrary", "arbitrary")),
        name="ffn",
    )(x, norm_w.reshape(1, D_MODEL), wu, wd, final_w.reshape(1, D_MODEL))


def _swap_halves(w):
    half = C_ROPE // 2
    return jnp.concatenate([w[..., half:], w[..., :half]], axis=-1)


def _prep_w_in(w_in):
    lead = w_in.shape[:-1]
    ckr = w_in[..., OFF_CKR:IN_COLS]
    g1 = jnp.concatenate([ckr, w_in[..., OFF_BBETA:OFF_CQ], jnp.zeros(lead + (LANE - C_ROPE - 2 * B_HEADS,), F32)],
                         axis=-1)
    g2 = jnp.concatenate([_swap_halves(ckr), jnp.zeros(lead + (LANE - C_ROPE,), F32)], axis=-1)
    tail = jnp.concatenate([w_in[..., OFF_CQ:OFF_CKV], w_in[..., OFF_CKV:OFF_CKR], g1, g2], axis=-1)
    return w_in.astype(BF16), tail.astype(BF16)


def _rope_tables(pos):
    half = C_ROPE // 2
    inv = ROPE_THETA ** (-jnp.arange(half, dtype=F32) / half)
    ang = pos.astype(F32)[:, None] * inv[None, :]
    cos, sin = jnp.cos(ang), jnp.sin(ang)
    cos2 = jnp.concatenate([cos, cos, cos, cos], axis=1)
    sin2 = jnp.concatenate([-sin, sin, -sin, sin], axis=1)
    return cos2, sin2


def _layer(x, bsz, L, tabs, lw, past, final_w, final_norm, layer, depth, stacks):
    proj = _inproj(x, lw['attn_norm_w'], lw['w_in'][0], layer, P1_COLS).reshape(bsz, L, P1_COLS)
    proj2 = _inproj(x, lw['attn_norm_w'], lw['w_in'][1], layer, P2_COLS).reshape(bsz, L, P2_COLS)

    ya, va = _mixer_a(proj, lw['a_ln_w'], lw['a_ln_b'], lw['a_w_s'], lw['a_b_s'], past is not None)

    if past is None:
        rows, chunk = 2 * CHUNK, CHUNK
        buf8 = jnp.zeros((bsz, 8, B_QKV), F32)
        s0 = jnp.zeros((bsz, B_HEADS, B_DK, B_DV), F32)
    else:
        rows, chunk = L, L
        ckv_past, kr_past, s0, conv_buf = past
        buf8 = jnp.concatenate([jnp.zeros((bsz, 8 - (B_CONV - 1), B_QKV), F32), conv_buf], axis=1)
    prep = _gdn_prep(proj, proj2, buf8, lw['b_conv_w'], lw['b_a_log'], lw['b_dt_bias'], rows, chunk)
    yb, s_new = _gdn_scan(prep, proj, lw['b_norm_w'], s0, rows, chunk)
    buf_new = proj[:, L - (B_CONV - 1):, P_BQKV:P_BQKV + B_QKV]

    cos8, sin8 = tabs
    mla_args = (proj2, cos8, sin8, lw['c_q_norm_w'], lw['wqn'], lw['wqr'], lw['wqs'], lw['c_kv_norm_w'])
    if past is None:
        ckv, kr, q, k, kr2, v = _mla_proj(*mla_args, (lw['wk'], lw['wv']), False, layer, depth, stacks)
        yc = _attn_heads(q, k, kr2, v)
    else:
        ckv, kr, kcat, q = _mla_proj(*mla_args, (lw['wukt'],), True, layer, depth, stacks)
        yc = _attn_full(q, ckv_past, kr_past, kcat, lw['wuv'], layer)

    m = bsz * L
    x = _outproj(x, ya.reshape(m, A_WIDTH), yb.reshape(m, B_WIDTH), yc.reshape(m, C_WIDTH), lw['w_out'], layer)
    x = _ffn(x, lw['mlp_norm_w'], lw['w_up'], lw['w_down'], final_w, final_norm, layer)
    return x, (ckv, kr, s_new, buf_new, va)


def kernel(x_prompt, x_sample, cache_mla_ckv, cache_mla_krope, state_gdn, state_gdn_conv,
           attn_norm_w, w_in, a_ln_w, a_ln_b, a_w_s, a_b_s, b_conv_w, b_a_log, b_dt_bias, b_norm_w,
           c_q_norm_w, c_w_uq, c_kv_norm_w, c_w_uk, c_w_uv, w_out, mlp_norm_w, w_up, w_down, final_norm_w):
    depth = w_in.shape[0]
    bp, lp, _ = x_prompt.shape
    bs, ls, _ = x_sample.shape
    past_len = cache_mla_ckv.shape[2]
    tabs_p = _rope_tables(jnp.arange(lp, dtype=jnp.int32))
    tabs_s = _rope_tables(past_len + jnp.arange(ls, dtype=jnp.int32))
    hp = x_prompt.reshape(bp * lp, D_MODEL)
    hs = x_sample.reshape(bs * ls, D_MODEL)
    sp_list, ss_list = [], []
    stacks_p = stacks_s = None
    w_in16, w_out16, w_up16, w_down16 = _prep_w_in(w_in), w_out.astype(BF16), w_up.astype(BF16), w_down.astype(BF16)
    for l in range(depth):
        wq_rope = c_w_uq[l][:, :, C_NOPE:]
        lw = {
            'attn_norm_w': attn_norm_w[l], 'w_in': w_in16,
            'a_ln_w': a_ln_w[l], 'a_ln_b': a_ln_b[l], 'a_w_s': a_w_s[l], 'a_b_s': a_b_s[l],
            'b_conv_w': b_conv_w[l], 'b_a_log': b_a_log[l], 'b_dt_bias': b_dt_bias[l], 'b_norm_w': b_norm_w[l],
            'c_q_norm_w': c_q_norm_w[l], 'c_kv_norm_w': c_kv_norm_w[l],
            'wqn': c_w_uq[l][:, :, :C_NOPE].reshape(Q_LORA, C_HEADS * C_NOPE).astype(BF16),
            'wqr': wq_rope.reshape(Q_LORA, C_HEADS * C_ROPE).astype(BF16),
            'wqs': _swap_halves(wq_rope).reshape(Q_LORA, C_HEADS * C_ROPE).astype(BF16),
            'wk': c_w_uk[l].reshape(KV_LORA, C_HEADS * C_NOPE).astype(BF16),
            'wv': c_w_uv[l].reshape(KV_LORA, C_HEADS * C_VDIM).astype(BF16),
            'wukt': jnp.transpose(c_w_uk[l], (1, 2, 0)).astype(BF16),
            'wuv': jnp.transpose(c_w_uv[l], (1, 0, 2)).astype(BF16),
            'w_out': w_out16, 'mlp_norm_w': mlp_norm_w[l], 'w_up': w_up16, 'w_down': w_down16,
        }
        last = l == depth - 1
        hp, sp = _layer(hp, bp, lp, tabs_p, lw, None, final_norm_w, last, l, depth, stacks_p)
        hs, ss = _layer(hs, bs, ls, tabs_s, lw,
                        (cache_mla_ckv, cache_mla_krope, state_gdn[l], state_gdn_conv[l]), final_norm_w, last,
                        l, depth, stacks_s)
        stacks_p, stacks_s = sp[:2], ss[:2]
        sp_list.append(sp)
        ss_list.append(ss)

    def stack(lst, i):
        return jnp.stack([s[i] for s in lst], axis=0)

    return (hp.reshape(bp, lp, D_MODEL), hs.reshape(bs, ls, D_MODEL),
            stacks_p[0], stacks_p[1], stack(sp_list, 2), stack(sp_list, 3),
            stacks_s[0], stacks_s[1], stack(ss_list, 2), stack(ss_list, 3), stack(ss_list, 4))
```

```python
import functools
import math

import jax
import jax.numpy as jnp
from jax import lax
from jax.experimental import pallas as pl
from jax.experimental.pallas import tpu as pltpu

F32 = jnp.float32
BF16 = jnp.bfloat16

D_MODEL = 2048
CHUNK = 64
EPS = 1e-6
LN_EPS = 1e-5
A_GROUPS = 4
A_HEAD = 128
A_WIDTH = 512
A_CHUNK = 128
B_HEADS = 4
B_DK = 128
B_DV = 128
B_WIDTH = 512
B_QKV = 1536
B_CONV = 4
C_HEADS = 8
C_NOPE = 128
C_ROPE = 64
C_VDIM = 128
C_WIDTH = 1024
Q_LORA = 512
KV_LORA = 256
ROPE_THETA = 10000.0
C_SCALE = (C_NOPE + C_ROPE) ** -0.5
Q_SCALE = C_SCALE * math.log2(math.e)
D_FF = 4 * D_MODEL

OFF_AU = 0
OFF_AV = OFF_AU + A_WIDTH
OFF_BQKV = OFF_AV + A_WIDTH
OFF_BZ = OFF_BQKV + B_QKV
OFF_BBETA = OFF_BZ + B_WIDTH
OFF_BALPHA = OFF_BBETA + B_HEADS
OFF_CQ = OFF_BALPHA + B_HEADS
OFF_CKV = OFF_CQ + Q_LORA
OFF_CKR = OFF_CKV + KV_LORA
IN_COLS = OFF_CKR + C_ROPE

P_AU = OFF_AU
P_AV = OFF_AV
P_BQKV = OFF_BQKV
P_BZ = OFF_BZ
P1_COLS = OFF_BBETA
P_CQ = 0
P_CKV = 512
P_G1 = 768
P_G2 = 896
P2_COLS = 1024
LANE = 128
G1_BETA = C_ROPE
G1_ALPHA = C_ROPE + B_HEADS
KCAT = KV_LORA + 2 * C_ROPE
QK_DIM = C_NOPE + 2 * C_ROPE

V7X_VMEM_BYTES = 64 * 1024 * 1024
VMEM_LIMIT = V7X_VMEM_BYTES - 8 * 1024 * 1024


def _mm(a, b):
    return jnp.dot(a, b, preferred_element_type=F32)


def _mm_nt(a, b):
    return lax.dot_general(a, b, (((1,), (1,)), ((), ())), preferred_element_type=F32)


def _split2(x):
    hi = x.astype(BF16)
    lo = (x - hi.astype(F32)).astype(BF16)
    return hi, lo


def _split3(x):
    hi = x.astype(BF16)
    r = x - hi.astype(F32)
    mid = r.astype(BF16)
    lo = (r - mid.astype(F32)).astype(BF16)
    return hi, mid, lo


def _cat3_lhs(a):
    ah, al = _split2(a)
    return jnp.concatenate([ah, ah, al], axis=1)


def _cat3_rhs(b):
    bh, bl = _split2(b)
    return jnp.concatenate([bh, bl, bh], axis=0)


def _mm3(a, b):
    return _mm(_cat3_lhs(a), _cat3_rhs(b))


def _mm_exact_lhs(a, b):
    return _mm(jnp.concatenate([a, a, a], axis=1), jnp.concatenate(_split3(b), axis=0))


def _mm_nt_exact_lhs(a, b):
    return _mm_nt(jnp.concatenate([a, a, a], axis=1), jnp.concatenate(_split3(b), axis=1))


def _lane_tile(x, n):
    return x if n == 1 else jnp.concatenate([x] * n, axis=1)


def _params(sem):
    return pltpu.CompilerParams(dimension_semantics=sem, vmem_limit_bytes=VMEM_LIMIT)


def _inproj_kernel(x_ref, nw_ref, w_ref, o_ref):
    x = x_ref[...]
    ms = jnp.mean(x * x, axis=-1, keepdims=True)
    xn = (x * lax.rsqrt(ms + EPS) * nw_ref[...]).astype(BF16)
    o_ref[...] = _mm(xn, w_ref[...])


def _inproj(x, norm_w, w_pad, layer, ncols):
    m = x.shape[0]
    tm = min(m, 1024 if ncols <= 1024 else 512)
    tn = ncols
    return pl.pallas_call(
        _inproj_kernel,
        out_shape=jax.ShapeDtypeStruct((m, ncols), F32),
        grid=(ncols // tn, m // tm),
        in_specs=[pl.BlockSpec((tm, D_MODEL), lambda j, i: (i, 0)),
                  pl.BlockSpec((1, D_MODEL), lambda j, i: (0, 0)),
                  pl.BlockSpec((None, D_MODEL, tn), lambda j, i: (layer, 0, j))],
        out_specs=pl.BlockSpec((tm, tn), lambda j, i: (i, j)),
        compiler_params=_params(("arbitrary", "arbitrary")),
        name="inproj",
    )(x, norm_w.reshape(1, D_MODEL), w_pad)


def _gelu(x):
    return 0.5 * x * (1.0 + lax.erf(x * (2.0 ** -0.5)))


def _mixa_kernel(u_ref, v_ref, lnw_ref, lnb_ref, ws_ref, bs_ref, ya_ref, *va_ref, lc, nsub):
    u = _gelu(u_ref[0])
    v = _gelu(v_ref[0])
    mu = jnp.mean(v, axis=-1, keepdims=True)
    vc = v - mu
    var = jnp.mean(vc * vc, axis=-1, keepdims=True)
    vn = vc * lax.rsqrt(var + LN_EPS) * lnw_ref[...] + lnb_ref[...]
    if va_ref:
        va_ref[0][0] = vn
    for s in range(nsub):
        rows = slice(s * lc, (s + 1) * lc)
        for g in range(A_GROUPS):
            cols = slice(g * A_HEAD, (g + 1) * A_HEAD)
            f = _mm(ws_ref[g], vn[rows, cols].astype(BF16)) + bs_ref[g]
            ya_ref[0, rows, cols] = (u[rows, cols] * f).astype(BF16)


def _mixer_a(proj, ln_w, ln_b, w_s, b_s, want_va):
    bsz, L, _ = proj.shape
    lc = min(L, A_CHUNK)
    rows = min(L, 512)
    nsub = rows // lc
    idx = jnp.arange(lc)
    allowed = (idx[:, None] // CHUNK) >= (idx[None, :] // CHUNK)
    ws = jnp.where(allowed[None], w_s[:, :lc, :lc], 0.0).astype(BF16)
    bs = jnp.broadcast_to(b_s[:, :lc, None], (A_GROUPS, lc, A_HEAD)).astype(F32)
    out_shape = [jax.ShapeDtypeStruct((bsz, L, A_WIDTH), BF16)]
    out_specs = [pl.BlockSpec((1, rows, A_WIDTH), lambda b, i: (b, i, 0))]
    if want_va:
        out_shape.append(jax.ShapeDtypeStruct((bsz, L, A_WIDTH), F32))
        out_specs.append(pl.BlockSpec((1, rows, A_WIDTH), lambda b, i: (b, i, 0)))
    res = pl.pallas_call(
        functools.partial(_mixa_kernel, lc=lc, nsub=nsub),
        out_shape=out_shape,
        grid=(bsz, L // rows),
        in_specs=[pl.BlockSpec((1, rows, A_WIDTH), lambda b, i: (b, i, P_AU // A_WIDTH)),
                  pl.BlockSpec((1, rows, A_WIDTH), lambda b, i: (b, i, P_AV // A_WIDTH)),
                  pl.BlockSpec((1, A_WIDTH), lambda b, i: (0, 0)),
                  pl.BlockSpec((1, A_WIDTH), lambda b, i: (0, 0)),
                  pl.BlockSpec((A_GROUPS, lc, lc), lambda b, i: (0, 0, 0)),
                  pl.BlockSpec((A_GROUPS, lc, A_HEAD), lambda b, i: (0, 0, 0))],
        out_specs=out_specs,
        compiler_params=_params(("arbitrary", "arbitrary")),
        name="mixer_a",
    )(proj, proj, ln_w.reshape(1, A_WIDTH), ln_b.reshape(1, A_WIDTH), ws, bs)
    return res if want_va else (res[0], None)


def _softplus(x):
    return jnp.maximum(x, 0.0) + jnp.log1p(jnp.exp(-jnp.abs(x)))


def _gdn_prep_kernel(xq_ref, xk_ref, xv_ref, hq_ref, hk_ref, hv_ref, buf_ref, g1_ref, cw_ref, alog_ref, dtb_ref,
                     u_ref, w_ref, qg_ref, kgt_ref, qk_ref, eg_ref, xp_scr, *, nb, rows, chunk):
    i = pl.program_id(1)
    nsub = rows // chunk
    ri = lax.broadcasted_iota(jnp.int32, (rows, rows), 0)
    ci = lax.broadcasted_iota(jnp.int32, (rows, rows), 1)
    same = (ri // chunk) == (ci // chunk)
    causal = same & (ri >= ci)
    strict = same & (ri > ci)
    tri = jnp.where(causal, 1.0, 0.0).astype(BF16)
    eye_r = jnp.where(ri == ci, 1.0, 0.0).astype(F32)
    li = lax.broadcasted_iota(jnp.int32, (LANE, LANE), 0)
    lj = lax.broadcasted_iota(jnp.int32, (LANE, LANE), 1)
    eye_l = jnp.where(li == lj, 1.0, 0.0).astype(BF16)

    ys, betas, gts = [], [], []
    for bb in range(nb):
        x = jnp.concatenate([xq_ref[bb], xk_ref[bb], xv_ref[bb]], axis=1)
        halo = jnp.concatenate([hq_ref[bb], hk_ref[bb], hv_ref[bb]], axis=1)
        xp_scr[bb, 0:8, :] = jnp.where(i == 0, buf_ref[bb], halo)
        xp_scr[bb, 8:8 + rows, :] = x
        y = x * cw_ref[B_CONV - 1:B_CONV, :]
        for k in range(1, B_CONV):
            y = y + xp_scr[bb, 8 - k:8 - k + rows, :] * cw_ref[B_CONV - 1 - k:B_CONV - k, :]
        ys.append(y * jax.nn.sigmoid(y))
        g1 = g1_ref[bb]
        betas.append(jax.nn.sigmoid(g1))
        gts.append(-jnp.exp(alog_ref[...]) * _softplus(g1 + dtb_ref[...]))

    gcs = [_mm_exact_lhs(tri, g) for g in gts]
    gcts = [_mm_nt_exact_lhs(eye_l, gc) for gc in gcs]
    glasts = [jnp.concatenate(
        [jnp.broadcast_to(gc[(s + 1) * chunk - 1:(s + 1) * chunk, :], (chunk, LANE)) for s in range(nsub)], axis=0)
        for gc in gcs]

    pairs = [(bb, h) for bb in range(nb) for h in range(B_HEADS)]
    qs, ks, kbs, rhss, decs, egcs = [], [], [], [], [], []
    for bb, h in pairs:
        y, gc = ys[bb], gcs[bb]
        bcol = betas[bb][:, G1_BETA + h:G1_BETA + h + 1]
        gcol = gc[:, G1_ALPHA + h:G1_ALPHA + h + 1]
        grow = gcts[bb][G1_ALPHA + h:G1_ALPHA + h + 1, :]
        qh = y[:, h * B_DK:(h + 1) * B_DK]
        qh = qh * lax.rsqrt(jnp.sum(qh * qh, axis=-1, keepdims=True) + EPS) * (B_DK ** -0.5)
        kh = y[:, B_WIDTH + h * B_DK:B_WIDTH + (h + 1) * B_DK]
        kh = kh * lax.rsqrt(jnp.sum(kh * kh, axis=-1, keepdims=True) + EPS)
        vh = y[:, 2 * B_WIDTH + h * B_DV:2 * B_WIDTH + (h + 1) * B_DV]
        kb = kh * bcol
        egc = jnp.exp(gcol)
        qs.append(qh)
        ks.append(kh)
        kbs.append(kb)
        egcs.append(egc)
        rhss.append(jnp.concatenate([vh * bcol, kb * egc], axis=1))
        decs.append(jnp.where(causal, jnp.exp(jnp.minimum(gcol - grow, 0.0)), 0.0))

    khb = [k.astype(BF16) for k in ks]
    kk = [_mm_nt(kb.astype(BF16), kx) for kb, kx in zip(kbs, khb)]
    qkm = [_mm_nt(q.astype(BF16), kx) for q, kx in zip(qs, khb)]

    ps = [jnp.where(strict, -(a * d), 0.0) for a, d in zip(kk, decs)]
    ts = [eye_r + p for p in ps]
    ms = [_mm(_cat3_lhs(p), _cat3_rhs(p)) for p in ps]
    nsq = int(math.log2(chunk)) - 1
    for step in range(nsq):
        mr = [_cat3_rhs(m) for m in ms]
        upd = [_mm(_cat3_lhs(t), b) for t, b in zip(ts, mr)]
        if step + 1 < nsq:
            ms = [_mm(_cat3_lhs(m), b) for m, b in zip(ms, mr)]
        ts = [t + d for t, d in zip(ts, upd)]
    sols = [_mm3(t, r) for t, r in zip(ts, rhss)]

    for n, (bb, h) in enumerate(pairs):
        hs = slice(h * B_DK, (h + 1) * B_DK)
        gcol = gcs[bb][:, G1_ALPHA + h:G1_ALPHA + h + 1]
        glcol = glasts[bb][:, G1_ALPHA + h:G1_ALPHA + h + 1]
        u_ref[bb, :, hs] = sols[n][:, :B_DV]
        w_ref[bb, :, hs] = sols[n][:, B_DV:].astype(BF16)
        qg_ref[bb, :, hs] = (qs[n] * egcs[n]).astype(BF16)
        kg = (ks[n] * jnp.exp(glcol - gcol)).astype(BF16)
        kgt_ref[bb, hs, :] = _mm_nt(eye_l, kg).astype(BF16)
        qk_ref[bb, :, h * rows:(h + 1) * rows] = (qkm[n] * decs[n]).astype(BF16)
        for s in range(nsub):
            eg_ref[bb, s, h:h + 1, :] = jnp.broadcast_to(
                jnp.exp(glasts[bb][s * chunk:s * chunk + 1, G1_ALPHA + h:G1_ALPHA + h + 1]), (1, LANE))


def _gdn_prep(proj, proj2, buf8, conv_w, a_log, dt_bias, rows, chunk):
    bsz, L, _ = proj.shape
    nb = 2
    nblk = L // rows
    rpb = max(rows // 8, 1)
    alog = jnp.zeros((1, LANE), F32).at[0, G1_ALPHA:G1_ALPHA + B_HEADS].set(a_log)
    dtb = jnp.zeros((1, LANE), F32).at[0, G1_ALPHA:G1_ALPHA + B_HEADS].set(dt_bias)
    return pl.pallas_call(
        functools.partial(_gdn_prep_kernel, nb=nb, rows=rows, chunk=chunk),
        out_shape=[jax.ShapeDtypeStruct((bsz, L, B_WIDTH), F32),
                   jax.ShapeDtypeStruct((bsz, L, B_WIDTH), BF16),
                   jax.ShapeDtypeStruct((bsz, L, B_WIDTH), BF16),
                   jax.ShapeDtypeStruct((bsz, B_WIDTH, L), BF16),
                   jax.ShapeDtypeStruct((bsz, L, B_HEADS * rows), BF16),
                   jax.ShapeDtypeStruct((bsz, L // chunk, B_HEADS, LANE), F32)],
        grid=(bsz // nb, nblk),
        in_specs=[pl.BlockSpec((nb, rows, B_WIDTH), lambda b, i, c=P_BQKV // B_WIDTH + part: (b, i, c))
                  for part in range(3)] + [
                  pl.BlockSpec((nb, 8, B_WIDTH),
                               lambda b, i, c=P_BQKV // B_WIDTH + part: (b, jnp.maximum(i * rpb - 1, 0), c))
                  for part in range(3)] + [
                  pl.BlockSpec((nb, 8, B_QKV), lambda b, i: (b, 0, 0)),
                  pl.BlockSpec((nb, rows, LANE), lambda b, i: (b, i, P_G1 // LANE)),
                  pl.BlockSpec((B_CONV, B_QKV), lambda b, i: (0, 0)),
                  pl.BlockSpec((1, LANE), lambda b, i: (0, 0)),
                  pl.BlockSpec((1, LANE), lambda b, i: (0, 0))],
        out_specs=[pl.BlockSpec((nb, rows, B_WIDTH), lambda b, i: (b, i, 0)),
                   pl.BlockSpec((nb, rows, B_WIDTH), lambda b, i: (b, i, 0)),
                   pl.BlockSpec((nb, rows, B_WIDTH), lambda b, i: (b, i, 0)),
                   pl.BlockSpec((nb, B_WIDTH, rows), lambda b, i: (b, 0, i)),
                   pl.BlockSpec((nb, rows, B_HEADS * rows), lambda b, i: (b, i, 0)),
                   pl.BlockSpec((nb, rows // chunk, B_HEADS, LANE), lambda b, i: (b, i, 0, 0))],
        scratch_shapes=[pltpu.VMEM((nb, rows + 8, B_QKV), F32)],
        compiler_params=_params(("arbitrary", "arbitrary")),
        name="gdn_prep",
    )(proj, proj, proj, proj, proj, proj, buf8, proj2, conv_w, alog, dtb)


def _gdn_scan_kernel(u_ref, w_ref, qg_ref, kgt_ref, qk_ref, eg_ref, z_ref, nw_ref, s0_ref,
                     yb_ref, s_ref, *, bsz, rows, prows, chunk):
    @pl.when(pl.program_id(0) == 0)
    def _():
        s_ref[...] = s0_ref[...]

    nsub = rows // chunk
    pairs = [(b, h) for b in range(bsz) for h in range(B_HEADS)]
    st = [s_ref[b, h] for b, h in pairs]
    for s in range(nsub):
        rs = slice(s * chunk, (s + 1) * chunk)
        r1 = [_mm(jnp.concatenate([w_ref[b, rs, h * B_DV:(h + 1) * B_DV], qg_ref[b, rs, h * B_DV:(h + 1) * B_DV]],
                                  axis=0), x.astype(BF16))
              for (b, h), x in zip(pairs, st)]
        vb = [(u_ref[b, rs, h * B_DV:(h + 1) * B_DV] - r[:chunk]).astype(BF16) for (b, h), r in zip(pairs, r1)]
        c0 = (s % (prows // chunk)) * chunk
        r2 = [_mm(jnp.concatenate([qk_ref[b, rs, h * prows + c0:h * prows + c0 + chunk],
                                   kgt_ref[b, h * B_DK:(h + 1) * B_DK, rs]], axis=0), v)
              for (b, h), v in zip(pairs, vb)]
        for n, (b, h) in enumerate(pairs):
            hs = slice(h * B_DV, (h + 1) * B_DV)
            o = r1[n][chunk:] + r2[n][:chunk]
            st[n] = st[n] * eg_ref[b, s, h:h + 1, :] + r2[n][chunk:]
            z = z_ref[b, rs, hs]
            on = o * lax.rsqrt(jnp.mean(o * o, axis=-1, keepdims=True) + EPS) * nw_ref[...]
            yb_ref[b, rs, hs] = (on * (z * jax.nn.sigmoid(z))).astype(BF16)
    for n, (b, h) in enumerate(pairs):
        s_ref[b, h] = st[n]


def _gdn_scan(prep, proj, norm_w, s0, prows, chunk):
    u, w, qg, kgt, qk, eg = prep
    bsz, L, _ = u.shape
    rows = 2 * prows if L % (2 * prows) == 0 else prows
    nblk = L // rows
    return pl.pallas_call(
        functools.partial(_gdn_scan_kernel, bsz=bsz, rows=rows, prows=prows, chunk=chunk),
        out_shape=[jax.ShapeDtypeStruct((bsz, L, B_WIDTH), BF16),
                   jax.ShapeDtypeStruct((bsz, B_HEADS, B_DK, B_DV), F32)],
        grid=(nblk,),
        in_specs=[pl.BlockSpec((bsz, rows, B_WIDTH), lambda i: (0, i, 0)),
                  pl.BlockSpec((bsz, rows, B_WIDTH), lambda i: (0, i, 0)),
                  pl.BlockSpec((bsz, rows, B_WIDTH), lambda i: (0, i, 0)),
                  pl.BlockSpec((bsz, B_WIDTH, rows), lambda i: (0, 0, i)),
                  pl.BlockSpec((bsz, rows, B_HEADS * prows), lambda i: (0, i, 0)),
                  pl.BlockSpec((bsz, rows // chunk, B_HEADS, LANE), lambda i: (0, i, 0, 0)),
                  pl.BlockSpec((bsz, rows, B_WIDTH), lambda i: (0, i, P_BZ // B_WIDTH)),
                  pl.BlockSpec((1, B_DV), lambda i: (0, 0)),
                  pl.BlockSpec((bsz, B_HEADS, B_DK, B_DV), lambda i: (0, 0, 0, 0))],
        out_specs=[pl.BlockSpec((bsz, rows, B_WIDTH), lambda i: (0, i, 0)),
                   pl.BlockSpec((bsz, B_HEADS, B_DK, B_DV), lambda i: (0, 0, 0, 0))],
        compiler_params=_params(("arbitrary",)),
        name="gdn_scan",
    )(u, w, qg, kgt, qk, eg, proj, norm_w.reshape(1, B_DV), s0)


def _mla_common(cq_ref, ckv_ref, g1_ref, g2_ref, cos_ref, sin_ref, qnw_ref, wqn_ref, wqr_ref, wqs_ref, kvnw_ref):
    cq = cq_ref[0]
    cqn = (cq * lax.rsqrt(jnp.mean(cq * cq, axis=-1, keepdims=True) + EPS) * qnw_ref[...]).astype(BF16)
    c = ckv_ref[0]
    ckv = c * lax.rsqrt(jnp.mean(c * c, axis=-1, keepdims=True) + EPS) * kvnw_ref[...]
    cos2, sin2 = cos_ref[...], sin_ref[...]
    kr = g1_ref[0][:, 0:C_ROPE] * cos2[:, 0:C_ROPE] + g2_ref[0][:, 0:C_ROPE] * sin2[:, 0:C_ROPE]
    qn = _mm(cqn, wqn_ref[...])
    ntile = C_HEADS * C_ROPE // LANE
    qr = (_mm(cqn, wqr_ref[...]) * _lane_tile(cos2, ntile) + _mm(cqn, wqs_ref[...]) * _lane_tile(sin2, ntile)) * Q_SCALE
    return ckv, kr, qn, qr


def _rope_slot(qr, h):
    lane = lax.broadcasted_iota(jnp.int32, (1, LANE), 1)
    pair = qr[:, (h // 2) * LANE:(h // 2 + 1) * LANE]
    return jnp.where((lane // C_ROPE) == (h % 2), pair, 0.0).astype(BF16)


def _write_stacked(out_ref, val, layer, prev_ref):
    for d in range(out_ref.shape[0]):
        if d == layer:
            out_ref[d] = val
        elif prev_ref is None:
            out_ref[d] = jnp.zeros_like(val)
        else:
            out_ref[d] = prev_ref[d]


def _mla_proj_kernel(cq_ref, ckv_ref, g1_ref, g2_ref, cos_ref, sin_ref, qnw_ref, wqn_ref, wqr_ref, wqs_ref,
                     kvnw_ref, wukt_ref, ckv_out, kr_out, kcat_out, q_out, *, layer, prev):
    ckv, kr, qn, qr = _mla_common(cq_ref, ckv_ref, g1_ref, g2_ref, cos_ref, sin_ref, qnw_ref, wqn_ref, wqr_ref,
                                  wqs_ref, kvnw_ref)
    _write_stacked(ckv_out, ckv, layer, prev and prev[0])
    _write_stacked(kr_out, kr, layer, prev and prev[1])
    kcat_out[0, :, 0:KV_LORA] = ckv.astype(BF16)
    kcat_out[0, :, KV_LORA:KV_LORA + C_ROPE] = kr.astype(BF16)
    kcat_out[0, :, KV_LORA + C_ROPE:KCAT] = kr.astype(BF16)
    for h in range(C_HEADS):
        ql = _mm(qn[:, h * C_NOPE:(h + 1) * C_NOPE].astype(BF16), wukt_ref[h]) * Q_SCALE
        q_out[0, h, :, 0:KV_LORA] = ql.astype(BF16)
        q_out[0, h, :, KV_LORA:KCAT] = _rope_slot(qr, h)


def _mla_heads_kernel(cq_ref, ckv_ref, g1_ref, g2_ref, cos_ref, sin_ref, qnw_ref, wqn_ref, wqr_ref, wqs_ref,
                      kvnw_ref, wk_ref, wv_ref, ckv_out, kr_out, q_out, k_out, kr2_out, v_out, *, layer, prev):
    ckv, kr, qn, qr = _mla_common(cq_ref, ckv_ref, g1_ref, g2_ref, cos_ref, sin_ref, qnw_ref, wqn_ref, wqr_ref,
                                  wqs_ref, kvnw_ref)
    _write_stacked(ckv_out, ckv, layer, prev and prev[0])
    _write_stacked(kr_out, kr, layer, prev and prev[1])
    ckvb = ckv.astype(BF16)
    kn = _mm(ckvb, wk_ref[...])
    vv = _mm(ckvb, wv_ref[...])
    krb = kr.astype(BF16)
    kr2_out[0] = jnp.concatenate([krb, krb], axis=1)
    for h in range(C_HEADS):
        hs = slice(h * C_NOPE, (h + 1) * C_NOPE)
        q_out[0, h, :, 0:C_NOPE] = (qn[:, hs] * Q_SCALE).astype(BF16)
        q_out[0, h, :, C_NOPE:QK_DIM] = _rope_slot(qr, h)
        k_out[0, h] = kn[:, hs].astype(BF16)
        v_out[0, h] = vv[:, h * C_VDIM:(h + 1) * C_VDIM].astype(BF16)


def _mla_proj(proj, cos2, sin2, q_norm_w, wqn, wqr, wqs, kv_norm_w, head_w, absorbed, layer, depth, stacks):
    bsz, L, _ = proj.shape
    tm = min(L, 512)
    const2 = lambda b, i: (0, 0)
    head_out = lambda width: (jax.ShapeDtypeStruct((bsz, C_HEADS, L, width), BF16),
                              pl.BlockSpec((1, C_HEADS, tm, width), lambda b, i: (b, 0, i, 0)))
    stack_out = lambda width: (jax.ShapeDtypeStruct((depth, bsz, L, width), F32),
                               pl.BlockSpec((depth, None, tm, width), lambda b, i: (0, b, i, 0)))
    outs = [stack_out(KV_LORA), stack_out(C_ROPE)]
    if absorbed:
        body = _mla_proj_kernel
        w_specs = [pl.BlockSpec((C_HEADS, C_NOPE, KV_LORA), lambda b, i: (0, 0, 0))]
        outs += [(jax.ShapeDtypeStruct((bsz, L, KCAT), BF16), pl.BlockSpec((1, tm, KCAT), lambda b, i: (b, i, 0))),
                 head_out(KCAT)]
    else:
        body = _mla_heads_kernel
        w_specs = [pl.BlockSpec((KV_LORA, C_HEADS * C_NOPE), const2), pl.BlockSpec((KV_LORA, C_HEADS * C_VDIM), const2)]
        outs += [head_out(QK_DIM), head_out(C_NOPE),
                 (jax.ShapeDtypeStruct((bsz, L, LANE), BF16), pl.BlockSpec((1, tm, LANE), lambda b, i: (b, i, 0))),
                 head_out(C_VDIM)]
    if stacks is None:
        kern, prev_specs, prev_args = functools.partial(body, layer=layer, prev=None), [], ()
    else:
        kern = lambda ckv_prev, kr_prev, *refs: body(*refs, layer=layer, prev=(ckv_prev, kr_prev))
        prev_specs, prev_args = [outs[0][1], outs[1][1]], tuple(stacks)
    return pl.pallas_call(
        kern,
        out_shape=[o[0] for o in outs],
        grid=(bsz, L // tm),
        in_specs=prev_specs + [pl.BlockSpec((1, tm, Q_LORA), lambda b, i: (b, i, P_CQ // Q_LORA)),
                  pl.BlockSpec((1, tm, KV_LORA), lambda b, i: (b, i, P_CKV // KV_LORA)),
                  pl.BlockSpec((1, tm, LANE), lambda b, i: (b, i, P_G1 // LANE)),
                  pl.BlockSpec((1, tm, LANE), lambda b, i: (b, i, P_G2 // LANE)),
                  pl.BlockSpec((tm, LANE), lambda b, i: (i, 0)),
                  pl.BlockSpec((tm, LANE), lambda b, i: (i, 0)),
                  pl.BlockSpec((1, Q_LORA), const2),
                  pl.BlockSpec((Q_LORA, C_HEADS * C_NOPE), const2),
                  pl.BlockSpec((Q_LORA, C_HEADS * C_ROPE), const2),
                  pl.BlockSpec((Q_LORA, C_HEADS * C_ROPE), const2),
                  pl.BlockSpec((1, KV_LORA), const2)] + w_specs,
        out_specs=[o[1] for o in outs],
        compiler_params=_params(("arbitrary", "arbitrary")),
        name="mla_proj" if absorbed else "mla_heads",
    )(*prev_args, proj, proj, proj, proj, cos2, sin2, q_norm_w.reshape(1, Q_LORA), wqn, wqr, wqs,
      kv_norm_w.reshape(1, KV_LORA), *head_w)


def _attn_heads_kernel(q_ref, k_ref, kr2_ref, v_ref, yc_ref, m_scr, l_scr, acc_scr, *, tq, tk, hps):
    qi = pl.program_id(2)
    rows = hps * tq
    m_scr[...] = jnp.full((rows, LANE), -1e30, F32)
    l_scr[...] = jnp.zeros((rows, LANE), F32)
    acc_scr[...] = jnp.zeros((rows, C_VDIM), F32)

    def block(kj, masked):
        start = pl.multiple_of(kj * tk, tk)
        kr2 = kr2_ref[0, pl.ds(start, tk), :]
        if masked:
            ri = lax.broadcasted_iota(jnp.int32, (tq, tk), 0)
            ci = lax.broadcasted_iota(jnp.int32, (tq, tk), 1)
            ok = ((qi * tq + ri) // CHUNK) >= ((start + ci) // CHUNK)

        def scores(g):
            return _mm_nt(q_ref[0, g], jnp.concatenate([k_ref[0, g, pl.ds(start, tk), :], kr2], axis=1))

        s_next = scores(0)
        for g in range(hps):
            rs = slice(g * tq, (g + 1) * tq)
            s = s_next
            if g + 1 < hps:
                s_next = scores(g + 1)
            if masked:
                s = jnp.where(ok, s, -jnp.inf)
            m_prev = m_scr[rs]
            m_new = jnp.maximum(m_prev, jnp.max(s, axis=-1, keepdims=True))
            alpha = jnp.exp2(m_prev - m_new)
            p = jnp.exp2(s - _lane_tile(m_new, tk // LANE))
            l_scr[rs] = alpha * l_scr[rs] + jnp.sum(p, axis=-1, keepdims=True)
            acc_scr[rs] = acc_scr[rs] * alpha + _mm(p.astype(BF16), v_ref[0, g, pl.ds(start, tk), :])
            m_scr[rs] = m_new

    def body(kj, carry):
        block(kj, False)
        return carry

    nfull = (qi * tq) // tk
    lax.fori_loop(0, nfull, body, 0)
    block(nfull, True)
    for g in range(hps):
        rs = slice(g * tq, (g + 1) * tq)
        yc_ref[0, :, g * C_VDIM:(g + 1) * C_VDIM] = (acc_scr[rs] * (1.0 / l_scr[rs])).astype(BF16)


def _attn_heads(q, k, kr2, v):
    bsz, _, L, _ = q.shape
    tq = min(L, 512)
    tk = min(L, 512)
    hps = 4
    rows = hps * tq
    return pl.pallas_call(
        functools.partial(_attn_heads_kernel, tq=tq, tk=tk, hps=hps),
        out_shape=jax.ShapeDtypeStruct((bsz, L, C_WIDTH), BF16),
        grid=(bsz, C_HEADS // hps, L // tq),
        in_specs=[pl.BlockSpec((1, hps, tq, QK_DIM), lambda b, g, i: (b, g, i, 0)),
                  pl.BlockSpec((1, hps, L, C_NOPE), lambda b, g, i: (b, g, 0, 0)),
                  pl.BlockSpec((1, L, LANE), lambda b, g, i: (b, 0, 0)),
                  pl.BlockSpec((1, hps, L, C_VDIM), lambda b, g, i: (b, g, 0, 0))],
        out_specs=pl.BlockSpec((1, tq, hps * C_VDIM), lambda b, g, i: (b, i, g)),
        scratch_shapes=[pltpu.VMEM((rows, LANE), F32), pltpu.VMEM((rows, LANE), F32),
                        pltpu.VMEM((rows, C_VDIM), F32)],
        compiler_params=_params(("arbitrary", "arbitrary", "arbitrary")),
        name="attn_heads",
    )(q, k, kr2, v)


def _attn_finish(acc, l, wuv_ref, yc_ref, tq):
    o = acc * (1.0 / l)
    for h in range(C_HEADS):
        oh = o[h * tq:(h + 1) * tq].astype(BF16)
        yc_ref[0, :, h * C_VDIM:(h + 1) * C_VDIM] = _mm(oh, wuv_ref[h]).astype(BF16)


def _attn_full_kernel(q_ref, cpast_ref, rpast_ref, knew_ref, wuv_ref, yc_ref, *, tq):
    rows = C_HEADS * tq
    q = q_ref[0].reshape(rows, KCAT)
    rpast = rpast_ref[0].astype(BF16)
    kpast = jnp.concatenate([cpast_ref[0].astype(BF16), rpast, rpast], axis=1)
    knew = knew_ref[0]
    s_past = _mm_nt(q, kpast)
    s_new = _mm_nt(q, knew)
    m = jnp.maximum(jnp.max(s_past, axis=-1, keepdims=True), jnp.max(s_new, axis=-1, keepdims=True))
    p_past = jnp.exp2(s_past - m)
    p_new = jnp.exp2(s_new - m)
    l = jnp.sum(p_past, axis=-1, keepdims=True) + jnp.sum(p_new, axis=-1, keepdims=True)
    acc = _mm(p_past.astype(BF16), kpast[:, 0:KV_LORA]) + _mm(p_new.astype(BF16), knew[:, 0:KV_LORA])
    _attn_finish(acc, l, wuv_ref, yc_ref, tq)


def _attn_full(q, ckv_past, kr_past, kcat, wuv, layer):
    bsz, _, tq, _ = q.shape
    npast = ckv_past.shape[2]
    return pl.pallas_call(
        functools.partial(_attn_full_kernel, tq=tq),
        out_shape=jax.ShapeDtypeStruct((bsz, tq, C_WIDTH), BF16),
        grid=(bsz,),
        in_specs=[pl.BlockSpec((1, C_HEADS, tq, KCAT), lambda b: (b, 0, 0, 0)),
                  pl.BlockSpec((None, 1, npast, KV_LORA), lambda b: (layer, b, 0, 0)),
                  pl.BlockSpec((None, 1, npast, C_ROPE), lambda b: (layer, b, 0, 0)),
                  pl.BlockSpec((1, tq, KCAT), lambda b: (b, 0, 0)),
                  pl.BlockSpec((C_HEADS, KV_LORA, C_VDIM), lambda b: (0, 0, 0))],
        out_specs=pl.BlockSpec((1, tq, C_WIDTH), lambda b: (b, 0, 0)),
        compiler_params=_params(("arbitrary",)),
        name="attn_full",
    )(q, ckv_past, kr_past, kcat, wuv)


def _outproj_kernel(x_ref, ya_ref, yb_ref, yc_ref, wo_ref, o_ref):
    tn = 512
    for j in range(0, D_MODEL, tn):
        acc = _mm(ya_ref[...], wo_ref[0:A_WIDTH, j:j + tn])
        acc = acc + _mm(yb_ref[...], wo_ref[A_WIDTH:A_WIDTH + B_WIDTH, j:j + tn])
        acc = acc + _mm(yc_ref[...], wo_ref[A_WIDTH + B_WIDTH:D_MODEL, j:j + tn])
        o_ref[:, j:j + tn] = x_ref[:, j:j + tn] + acc


def _outproj(x, ya, yb, yc, wo, layer):
    m = x.shape[0]
    tm = min(m, 512)
    row = lambda i: (i, 0)
    return pl.pallas_call(
        _outproj_kernel,
        out_shape=jax.ShapeDtypeStruct((m, D_MODEL), F32),
        grid=(m // tm,),
        in_specs=[pl.BlockSpec((tm, D_MODEL), row),
                  pl.BlockSpec((tm, A_WIDTH), row),
                  pl.BlockSpec((tm, B_WIDTH), row),
                  pl.BlockSpec((tm, C_WIDTH), row),
                  pl.BlockSpec((None, D_MODEL, D_MODEL), lambda i: (layer, 0, 0))],
        out_specs=pl.BlockSpec((tm, D_MODEL), row),
        compiler_params=_params(("arbitrary",)),
        name="outproj",
    )(x, ya, yb, yc, wo)


def _ffn_kernel(x_ref, nw_ref, wu_ref, wd_ref, fw_ref, o_ref, xn_scr, *, final_norm):
    f = pl.program_id(1)

    @pl.when(f == 0)
    def _():
        x = x_ref[...]
        ms = jnp.mean(x * x, axis=-1, keepdims=True)
        xn_scr[...] = (x * lax.rsqrt(ms + EPS) * nw_ref[...]).astype(BF16)
        o_ref[...] = x

    hid = jnp.maximum(_mm(xn_scr[...], wu_ref[...]), 0.0)
    o_ref[...] += _mm((hid * hid).astype(BF16), wd_ref[...])

    if final_norm:
        @pl.when(f == pl.num_programs(1) - 1)
        def _():
            y = o_ref[...]
            ms = jnp.mean(y * y, axis=-1, keepdims=True)
            o_ref[...] = y * lax.rsqrt(ms + EPS) * fw_ref[...]


def _ffn(x, norm_w, wu, wd, final_w, final_norm, layer):
    m = x.shape[0]
    tm = min(m, 1024)
    tf = 512 if tm == 1024 else 2048
    return pl.pallas_call(
        functools.partial(_ffn_kernel, final_norm=final_norm),
        out_shape=jax.ShapeDtypeStruct((m, D_MODEL), F32),
        grid=(m // tm, D_FF // tf),
        in_specs=[pl.BlockSpec((tm, D_MODEL), lambda i, f: (i, 0)),
                  pl.BlockSpec((1, D_MODEL), lambda i, f: (0, 0)),
                  pl.BlockSpec((None, D_MODEL, tf), lambda i, f: (layer, 0, f)),
                  pl.BlockSpec((None, tf, D_MODEL), lambda i, f: (layer, f, 0)),
                  pl.BlockSpec((1, D_MODEL), lambda i, f: (0, 0))],
        out_specs=pl.BlockSpec((tm, D_MODEL), lambda i, f: (i, 0)),
        scratch_shapes=[pltpu.VMEM((tm, D_MODEL), BF16)],
        compiler_params=_params(("arbitrary", "arbitrary")),
        name="ffn",
    )(x, norm_w.reshape(1, D_MODEL), wu, wd, final_w.reshape(1, D_MODEL))


def _swap_halves(w):
    half = C_ROPE // 2
    return jnp.concatenate([w[..., half:], w[..., :half]], axis=-1)


def _prep_w_in(w_in):
    lead = w_in.shape[:-1]
    ckr = w_in[..., OFF_CKR:IN_COLS]
    g1 = jnp.concatenate([ckr, w_in[..., OFF_BBETA:OFF_CQ], jnp.zeros(lead + (LANE - C_ROPE - 2 * B_HEADS,), F32)],
                         axis=-1)
    g2 = jnp.concatenate([_swap_halves(ckr), jnp.zeros(lead + (LANE - C_ROPE,), F32)], axis=-1)
    tail = jnp.concatenate([w_in[..., OFF_CQ:OFF_CKV], w_in[..., OFF_CKV:OFF_CKR], g1, g2], axis=-1)
    return w_in.astype(BF16), tail.astype(BF16)


def _rope_tables(pos):
    half = C_ROPE // 2
    inv = ROPE_THETA ** (-jnp.arange(half, dtype=F32) / half)
    ang = pos.astype(F32)[:, None] * inv[None, :]
    cos, sin = jnp.cos(ang), jnp.sin(ang)
    cos2 = jnp.concatenate([cos, cos, cos, cos], axis=1)
    sin2 = jnp.concatenate([-sin, sin, -sin, sin], axis=1)
    return cos2, sin2


def _layer(x, bsz, L, tabs, lw, past, final_w, final_norm, layer, depth, stacks):
    proj = _inproj(x, lw['attn_norm_w'], lw['w_in'][0], layer, P1_COLS).reshape(bsz, L, P1_COLS)
    proj2 = _inproj(x, lw['attn_norm_w'], lw['w_in'][1], layer, P2_COLS).reshape(bsz, L, P2_COLS)

    ya, va = _mixer_a(proj, lw['a_ln_w'], lw['a_ln_b'], lw['a_w_s'], lw['a_b_s'], past is not None)

    if past is None:
        rows, chunk = 2 * CHUNK, CHUNK
        buf8 = jnp.zeros((bsz, 8, B_QKV), F32)
        s0 = jnp.zeros((bsz, B_HEADS, B_DK, B_DV), F32)
    else:
        rows, chunk = L, L
        ckv_past, kr_past, s0, conv_buf = past
        buf8 = jnp.concatenate([jnp.zeros((bsz, 8 - (B_CONV - 1), B_QKV), F32), conv_buf], axis=1)
    prep = _gdn_prep(proj, proj2, buf8, lw['b_conv_w'], lw['b_a_log'], lw['b_dt_bias'], rows, chunk)
    yb, s_new = _gdn_scan(prep, proj, lw['b_norm_w'], s0, rows, chunk)
    buf_new = proj[:, L - (B_CONV - 1):, P_BQKV:P_BQKV + B_QKV]

    cos2, sin2 = tabs
    mla_args = (proj2, cos2, sin2, lw['c_q_norm_w'], lw['wqn'], lw['wqr'], lw['wqs'], lw['c_kv_norm_w'])
    if past is None:
        ckv, kr, q, k, kr2, v = _mla_proj(*mla_args, (lw['wk'], lw['wv']), False, layer, depth, stacks)
        yc = _attn_heads(q, k, kr2, v)
    else:
        ckv, kr, kcat, q = _mla_proj(*mla_args, (lw['wukt'],), True, layer, depth, stacks)
        yc = _attn_full(q, ckv_past, kr_past, kcat, lw['wuv'], layer)

    m = bsz * L
    x = _outproj(x, ya.reshape(m, A_WIDTH), yb.reshape(m, B_WIDTH), yc.reshape(m, C_WIDTH), lw['w_out'], layer)
    x = _ffn(x, lw['mlp_norm_w'], lw['w_up'], lw['w_down'], final_w, final_norm, layer)
    return x, (ckv, kr, s_new, buf_new, va)


def kernel(x_prompt, x_sample, cache_mla_ckv, cache_mla_krope, state_gdn, state_gdn_conv,
           attn_norm_w, w_in, a_ln_w, a_ln_b, a_w_s, a_b_s, b_conv_w, b_a_log, b_dt_bias, b_norm_w,
           c_q_norm_w, c_w_uq, c_kv_norm_w, c_w_uk, c_w_uv, w_out, mlp_norm_w, w_up, w_down, final_norm_w):
    depth = w_in.shape[0]
    bp, lp, _ = x_prompt.shape
    bs, ls, _ = x_sample.shape
    past_len = cache_mla_ckv.shape[2]
    tabs_p = _rope_tables(jnp.arange(lp, dtype=jnp.int32))
    tabs_s = _rope_tables(past_len + jnp.arange(ls, dtype=jnp.int32))
    hp = x_prompt.reshape(bp * lp, D_MODEL)
    hs = x_sample.reshape(bs * ls, D_MODEL)
    sp_list, ss_list = [], []
    stacks_p = stacks_s = None
    w_in16, w_out16, w_up16, w_down16 = _prep_w_in(w_in), w_out.astype(BF16), w_up.astype(BF16), w_down.astype(BF16)
    for l in range(depth):
        wq_rope = c_w_uq[l][:, :, C_NOPE:]
        lw = {
            'attn_norm_w': attn_norm_w[l], 'w_in': w_in16,
            'a_ln_w': a_ln_w[l], 'a_ln_b': a_ln_b[l], 'a_w_s': a_w_s[l], 'a_b_s': a_b_s[l],
            'b_conv_w': b_conv_w[l], 'b_a_log': b_a_log[l], 'b_dt_bias': b_dt_bias[l], 'b_norm_w': b_norm_w[l],
            'c_q_norm_w': c_q_norm_w[l], 'c_kv_norm_w': c_kv_norm_w[l],
            'wqn': c_w_uq[l][:, :, :C_NOPE].reshape(Q_LORA, C_HEADS * C_NOPE).astype(BF16),
            'wqr': wq_rope.reshape(Q_LORA, C_HEADS * C_ROPE).astype(BF16),
            'wqs': _swap_halves(wq_rope).reshape(Q_LORA, C_HEADS * C_ROPE).astype(BF16),
            'wk': c_w_uk[l].reshape(KV_LORA, C_HEADS * C_NOPE).astype(BF16),
            'wv': c_w_uv[l].reshape(KV_LORA, C_HEADS * C_VDIM).astype(BF16),
            'wukt': jnp.transpose(c_w_uk[l], (1, 2, 0)).astype(BF16),
            'wuv': jnp.transpose(c_w_uv[l], (1, 0, 2)).astype(BF16),
            'w_out': w_out16, 'mlp_norm_w': mlp_norm_w[l], 'w_up': w_up16, 'w_down': w_down16,
        }
        last = l == depth - 1
        hp, sp = _layer(hp, bp, lp, tabs_p, lw, None, final_norm_w, last, l, depth, stacks_p)
        hs, ss = _layer(hs, bs, ls, tabs_s, lw,
                        (cache_mla_ckv, cache_mla_krope, state_gdn[l], state_gdn_conv[l]), final_norm_w, last,
                        l, depth, stacks_s)
        stacks_p, stacks_s = sp[:2], ss[:2]
        sp_list.append(sp)
        ss_list.append(ss)

    def stack(lst, i):
        return jnp.stack([s[i] for s in lst], axis=0)

    return (hp.reshape(bp, lp, D_MODEL), hs.reshape(bs, ls, D_MODEL),
            stacks_p[0], stacks_p[1], stack(sp_list, 2), stack(sp_list, 3),
            stacks_s[0], stacks_s[1], stack(ss_list, 2), stack(ss_list, 3), stack(ss_list, 4))
```
